```python
import jax, jax.numpy as jnp
from jax import lax
import numpy as np

D_MODEL = 1024
BATCH = 4
SEQ = 8192
DEPTH = 2
DEC_BATCH = 32
DEC_SEQ = 4
PAST_LEN = 16384
PAGE_SIZE = 128

HEAD_DIM = 64
ATTN_SCALE = HEAD_DIM ** -0.5
N_MEM = 256
MEM_HEADS = 4
MEM_WIDTH = MEM_HEADS * HEAD_DIM
DIL_PAIRS = ((128, 1), (512, 4), (2048, 16))
N_DIL_GROUPS = len(DIL_PAIRS)
HEADS_PER_GROUP = 4
DIL_WIDTH = HEADS_PER_GROUP * HEAD_DIM
QBLK = 128
ROPE_THETA = 500000.0
ROPE_DIMS = HEAD_DIM // 4
MLSTM_HEADS = 12
MLSTM_WIDTH = MLSTM_HEADS * HEAD_DIM
CONV_W = 4
MLSTM_CHUNK = 64
N_GROUPS = 4
EXPERTS_PER_GROUP = 8
N_EXPERTS = N_GROUPS * EXPERTS_PER_GROUP
TOP_K = 2
D_EXPERT = 256
ALPHA = (2 * DEPTH) ** 0.25
BETA = (8 * DEPTH) ** -0.25
N_A = (DEPTH + 1) // 2
N_B = DEPTH // 2
IN_A = 3 * N_DIL_GROUPS * DIL_WIDTH + MEM_WIDTH
IN_B = 4 * MLSTM_WIDTH + 2 * MLSTM_HEADS + MEM_WIDTH
OUT_A = DIL_WIDTH + MEM_WIDTH
OUT_B = MLSTM_WIDTH + MEM_WIDTH
LN_EPS = 1e-5

kernel_name = "hybrid_dilated_mlstm_hmoe_step"


def layer_norm(x, g, b):
    xf = x.astype(jnp.float32)
    mu = jnp.mean(xf, axis=-1, keepdims=True)
    var = jnp.mean(jnp.square(xf - mu), axis=-1, keepdims=True)
    return ((xf - mu) * lax.rsqrt(var + LN_EPS) * g + b).astype(x.dtype)


def rope(x, pos):
    half = ROPE_DIMS // 2
    inv = jnp.power(ROPE_THETA, -jnp.arange(half, dtype=jnp.float32) * (2.0 / ROPE_DIMS))
    ang = pos.astype(jnp.float32)[:, None] * inv[None, :]
    cos = jnp.cos(ang)[:, None, :]
    sin = jnp.sin(ang)[:, None, :]
    xr = x[..., :ROPE_DIMS].astype(jnp.float32)
    x1, x2 = xr[..., :half], xr[..., half:]
    rot = jnp.concatenate([x1 * cos - x2 * sin, x1 * sin + x2 * cos], axis=-1).astype(x.dtype)
    return jnp.concatenate([rot, x[..., ROPE_DIMS:]], axis=-1)


def dilated_prompt(q, k, v, dil, span):
    n, S, H, E = q.shape
    L = -(-S // (dil * QBLK)) * QBLK
    Sp, nb = L * dil, L // QBLK

    def split(t):
        t = jnp.pad(t, ((0, 0), (0, Sp - S), (0, 0), (0, 0)))
        return t.reshape(n, L, dil, H, E).transpose(0, 2, 1, 3, 4).reshape(n, dil, nb, QBLK, H, E)

    def with_prev(t):
        prev = jnp.pad(t, ((0, 0), (0, 0), (1, 0), (0, 0), (0, 0), (0, 0)))[:, :, :nb]
        return jnp.concatenate([prev, t], axis=3)

    qb = split(q)
    kw = with_prev(split(k))
    vw = with_prev(split(v))
    s = jnp.einsum("brnqhe,brnkhe->brnhqk", qb, kw, preferred_element_type=jnp.float32) * ATTN_SCALE
    qi = jnp.arange(QBLK)[:, None] + QBLK
    ki = jnp.arange(2 * QBLK)[None, :]
    band = (qi >= ki) & (qi - ki <= span)
    mask = band[None] & ((jnp.arange(nb) > 0)[:, None, None] | (ki >= QBLK)[None])
    s = jnp.where(mask[:, None], s, -jnp.inf)
    m = jnp.max(s, axis=-1, keepdims=True)
    p = jnp.exp(s - m)
    den = jnp.sum(p, axis=-1, keepdims=True)
    o = jnp.einsum("brnhqk,brnkhe->brnqhe", p / den, vw.astype(jnp.float32))
    lse = jnp.swapaxes((m + jnp.log(den))[..., 0], 3, 4)
    o = o.reshape(n, dil, L, H, E).transpose(0, 2, 1, 3, 4).reshape(n, Sp, H, E)[:, :S]
    lse = lse.reshape(n, dil, L, H).transpose(0, 2, 1, 3).reshape(n, Sp, H)[:, :S]
    return o, lse


def dilated_sample(q, kv_new, buf, dil, span):
    T = q.shape[1]
    Wb = buf.shape[1]
    kv = jnp.concatenate([buf.astype(kv_new.dtype), kv_new], axis=1)
    idx = Wb + jnp.arange(T)[:, None] - dil * jnp.arange(span + 1)[None, :]
    valid = idx >= 0
    g = kv[:, jnp.maximum(idx, 0)]
    s = jnp.einsum("bqhe,bqjhe->bqhj", q, g[:, :, :, 0], preferred_element_type=jnp.float32) * ATTN_SCALE
    s = jnp.where(valid[None, :, None, :], s, -jnp.inf)
    m = jnp.max(s, axis=-1, keepdims=True)
    p = jnp.exp(s - m)
    den = jnp.sum(p, axis=-1, keepdims=True)
    o = jnp.einsum("bqhj,bqjhe->bqhe", p / den, g[:, :, :, 1].astype(jnp.float32))
    return o, (m + jnp.log(den))[..., 0]


def mem_attend(q, mem_kv):
    s = jnp.einsum("bthe,bmhe->bhtm", q, mem_kv[:, :, 0], preferred_element_type=jnp.float32) * ATTN_SCALE
    p = jax.nn.softmax(s, axis=-1)
    return jnp.einsum("bhtm,bmhe->bthe", p, mem_kv[:, :, 1].astype(jnp.float32))


def mixer_dilated(h, pos, w_in, w_out, mem_kv, win_bufs):
    n, T, _ = h.shape
    z = h @ w_in
    qkv = z[..., :3 * N_DIL_GROUPS * DIL_WIDTH].reshape(n, T, 3, N_DIL_GROUPS, HEADS_PER_GROUP, HEAD_DIM)
    mem_q = z[..., 3 * N_DIL_GROUPS * DIL_WIDTH:].reshape(n, T, MEM_HEADS, HEAD_DIM)
    outs, lses, new_kv = [], [], []
    for g, (win, dil) in enumerate(DIL_PAIRS):
        span = win // dil
        q = rope(qkv[:, :, 0, g], pos)
        k = rope(qkv[:, :, 1, g], pos)
        kv = jnp.stack([k, qkv[:, :, 2, g]], axis=2)
        if win_bufs is None:
            o, lse = dilated_prompt(q, k, qkv[:, :, 2, g], dil, span)
            new_kv.append(kv[:, T - min(win, T):])
        else:
            o, lse = dilated_sample(q, kv, win_bufs[g], dil, span)
            new_kv.append(kv)
        outs.append(o)
        lses.append(lse)
    wts = jax.nn.softmax(jnp.stack(lses, axis=0), axis=0)
    mix = jnp.sum(wts[..., None] * jnp.stack(outs, axis=0), axis=0).reshape(n, T, DIL_WIDTH)
    mo = mem_attend(mem_q, mem_kv).reshape(n, T, MEM_WIDTH)
    y = jnp.concatenate([mix, mo], axis=-1).astype(h.dtype) @ w_out
    return y, new_kv


def causal_conv(u, buf, w):
    T = u.shape[1]
    uc = jnp.concatenate([buf.astype(u.dtype), u], axis=1)
    y = uc[:, 0:T] * w[0]
    for i in range(1, CONV_W):
        y = y + uc[:, i:i + T] * w[i]
    return y, uc[:, T:]


def mlstm_chunked(q, k, v, ig, lf, C0, n0, m0):
    n, H, T, E = q.shape
    L = MLSTM_CHUNK if T % MLSTM_CHUNK == 0 else T
    nc = T // L
    chunks = lambda t: jnp.moveaxis(t.reshape(n, H, nc, L, *t.shape[3:]), 2, 0)
    causal = jnp.tril(jnp.ones((L, L), dtype=bool))

    def step(carry, xs):
        C, nv, m = carry
        qc, kc, vc, ic, fc = xs
        b = jnp.cumsum(fc, axis=-1)
        log_intra = jnp.where(causal, b[..., :, None] - b[..., None, :] + ic[..., None, :], -jnp.inf)
        log_inter = b + m[..., None]
        mt = jnp.maximum(log_inter, jnp.max(log_intra, axis=-1))
        w_intra = jnp.exp(log_intra - mt[..., None])
        w_inter = jnp.exp(log_inter - mt)
        qk = jnp.einsum("nhte,nhse->nhts", qc, kc) * w_intra
        num = jnp.einsum("nhts,nhsf->nhtf", qk, vc) + w_inter[..., None] * jnp.einsum("nhte,nhef->nhtf", qc, C)
        den = jnp.sum(qk, axis=-1) + w_inter * jnp.einsum("nhte,nhe->nht", qc, nv)
        h = num / jnp.maximum(jnp.abs(den), jnp.exp(-mt))[..., None]
        b_last = b[..., -1]
        log_w = b_last[..., None] - b + ic
        m_new = jnp.maximum(b_last + m, jnp.max(log_w, axis=-1))
        wk = jnp.exp(log_w - m_new[..., None])[..., None] * kc
        decay = jnp.exp(b_last + m - m_new)
        C_new = decay[..., None, None] * C + jnp.einsum("nhse,nhsf->nhef", wk, vc)
        n_new = decay[..., None] * nv + jnp.sum(wk, axis=2)
        return (C_new, n_new, m_new), h

    (C, nv, m), hs = lax.scan(step, (C0, n0, m0), (chunks(q), chunks(k), chunks(v), chunks(ig), chunks(lf)))
    return jnp.moveaxis(hs, 0, 2).reshape(n, H, T, E), C, nv, m


def mixer_mlstm(h, w_in, conv_w, b_if, norm_g, w_out, mem_kv, conv_buf, C0, n0, m0):
    n, T, _ = h.shape
    W, H, E = MLSTM_WIDTH, MLSTM_HEADS, HEAD_DIM
    z = h @ w_in
    qk, new_buf = causal_conv(z[..., :2 * W], conv_buf, conv_w)
    qk = jax.nn.silu(qk.astype(jnp.float32))
    heads = lambda t: t.reshape(n, T, H, E).transpose(0, 2, 1, 3)
    q = heads(qk[..., :W])
    k = heads(qk[..., W:]) * ATTN_SCALE
    v = heads(z[..., 2 * W:3 * W].astype(jnp.float32))
    o_gate = jax.nn.sigmoid(z[..., 3 * W:4 * W].astype(jnp.float32))
    gates = z[..., 4 * W:4 * W + 2 * H].astype(jnp.float32).reshape(n, T, 2, H) + b_if.astype(jnp.float32)
    ig = gates[:, :, 0].transpose(0, 2, 1)
    lf = jax.nn.log_sigmoid(gates[:, :, 1]).transpose(0, 2, 1)
    mem_q = z[..., 4 * W + 2 * H:].reshape(n, T, MEM_HEADS, E)
    h_til, C, nv, m = mlstm_chunked(q, k, v, ig, lf, C0.astype(jnp.float32), n0.astype(jnp.float32),
                                    m0.astype(jnp.float32))
    mu = jnp.mean(h_til, axis=-1, keepdims=True)
    var = jnp.mean(jnp.square(h_til - mu), axis=-1, keepdims=True)
    hn = ((h_til - mu) * lax.rsqrt(var + LN_EPS)).transpose(0, 2, 1, 3).reshape(n, T, W) * norm_g
    cell = o_gate * hn
    mo = mem_attend(mem_q, mem_kv).reshape(n, T, MEM_WIDTH)
    y = jnp.concatenate([cell, mo], axis=-1).astype(h.dtype) @ w_out
    return y, new_buf, C, nv, m


def hier_moe(x, w_grp, b_grp, w_exp, b_exp, w_gate, w_up, w_down):
    n, T, D = x.shape
    xf = x.reshape(n * T, D)
    N = n * T
    rows = jnp.arange(N)
    gl = (xf @ w_grp).astype(jnp.float32) + b_grp
    gp = jax.nn.softmax(gl, axis=-1)
    gsel = jnp.argmax(gl, axis=-1)
    el = ((xf @ w_exp).astype(jnp.float32) + b_exp).reshape(N, N_GROUPS, EXPERTS_PER_GROUP)
    tv, ti = lax.top_k(el[rows, gsel], TOP_K)
    tw = jax.nn.softmax(tv, axis=-1) * gp[rows, gsel][:, None]
    within = jnp.sum(tw[..., None] * jax.nn.one_hot(ti, EXPERTS_PER_GROUP), axis=1)
    gates = (jax.nn.one_hot(gsel, N_GROUPS)[:, :, None] * within[:, None, :]).reshape(N, N_EXPERTS)
    out = jnp.zeros((N, D), jnp.float32)
    for e in range(N_EXPERTS):
        he = jax.nn.silu(xf @ w_gate[e]) * (xf @ w_up[e])
        out = out + gates[:, e:e + 1] * (he @ w_down[e])
    return out.astype(x.dtype).reshape(n, T, D)


def setup_inputs(seed: int = 0) -> dict:
    key = jax.random.key(seed)
    ks = iter(jax.random.split(key, 40))
    nrm = lambda shape, scale: jax.random.normal(next(ks), shape, jnp.float32) * scale
    D = D_MODEL
    inp = {}
    inp["x_prompt"] = nrm((BATCH, SEQ, D), 1.0)
    inp["x_sample"] = nrm((DEC_BATCH, DEC_SEQ, D), 1.0)
    inp["mem_prompt"] = nrm((BATCH, N_MEM, D), 1.0)
    for name, (win, dil) in zip(("cache_win1_kv", "cache_win2_kv", "cache_win3_kv"), DIL_PAIRS):
        inp[name] = nrm((N_A, DEC_BATCH, min(win, PAST_LEN), 2, HEADS_PER_GROUP, HEAD_DIM), 1.0)
    inp["cache_mem_kv"] = nrm((DEPTH, DEC_BATCH, N_MEM, 2, MEM_HEADS, HEAD_DIM), 1.0)
    inp["state_mlstm_conv"] = nrm((N_B, DEC_BATCH, CONV_W - 1, 2 * MLSTM_WIDTH), 1.0)
    inp["state_mlstm_C"] = nrm((N_B, DEC_BATCH, MLSTM_HEADS, HEAD_DIM, HEAD_DIM), 0.5)
    inp["state_mlstm_n"] = nrm((N_B, DEC_BATCH, MLSTM_HEADS, HEAD_DIM), 0.5)
    inp["state_mlstm_m"] = nrm((N_B, DEC_BATCH, MLSTM_HEADS), 1.0)
    inp["w_in_A"] = nrm((N_A, D, IN_A), D ** -0.5)
    inp["w_out_A"] = nrm((N_A, OUT_A, D), BETA * OUT_A ** -0.5)
    inp["w_in_B"] = nrm((N_B, D, IN_B), D ** -0.5)
    inp["mlstm_conv_w"] = nrm((N_B, CONV_W, 2 * MLSTM_WIDTH), CONV_W ** -0.5)
    b_i = nrm((N_B, 1, MLSTM_HEADS), 0.1)
    b_f = jnp.linspace(3.0, 6.0, MLSTM_HEADS, dtype=jnp.float32)[None, None, :] + nrm((N_B, 1, MLSTM_HEADS), 0.1)
    inp["mlstm_b_if"] = jnp.concatenate([b_i, b_f], axis=1)
    inp["mlstm_norm_g"] = 1.0 + nrm((N_B, MLSTM_WIDTH), 0.02)
    inp["w_out_B"] = nrm((N_B, OUT_B, D), BETA * OUT_B ** -0.5)
    inp["w_mem_kv"] = nrm((DEPTH, D, 2 * MEM_WIDTH), D ** -0.5)
    inp["ln_g"] = 1.0 + nrm((DEPTH, 2, D), 0.02)
    inp["ln_b"] = nrm((DEPTH, 2, D), 0.02)
    inp["w_grp"] = nrm((DEPTH, D, N_GROUPS), D ** -0.5)
    inp["b_grp"] = nrm((DEPTH, N_GROUPS), 0.01)
    inp["w_exp"] = nrm((DEPTH, D, N_EXPERTS), D ** -0.5)
    inp["b_exp"] = nrm((DEPTH, N_EXPERTS), 0.01)
    inp["w_gate"] = nrm((DEPTH, N_EXPERTS, D, D_EXPERT), D ** -0.5)
    inp["w_up"] = nrm((DEPTH, N_EXPERTS, D, D_EXPERT), D ** -0.5)
    inp["w_down"] = nrm((DEPTH, N_EXPERTS, D_EXPERT, D), BETA * D_EXPERT ** -0.5)
    return inp


def reference(x_prompt, x_sample, mem_prompt, cache_win1_kv, cache_win2_kv, cache_win3_kv, cache_mem_kv,
              state_mlstm_conv, state_mlstm_C, state_mlstm_n, state_mlstm_m,
              w_in_A, w_out_A, w_in_B, mlstm_conv_w, mlstm_b_if, mlstm_norm_g, w_out_B,
              w_mem_kv, ln_g, ln_b, w_grp, b_grp, w_exp, b_exp, w_gate, w_up, w_down):
    nP, S, _ = x_prompt.shape
    T = x_sample.shape[1]
    pos_p = jnp.arange(S, dtype=jnp.int32)
    pos_s = PAST_LEN + jnp.arange(T, dtype=jnp.int32)
    hp, hs = x_prompt, x_sample
    win_p, win_s = ([], [], []), ([], [], [])
    conv_p, conv_s, C_p, C_s, n_p, n_s, m_p, m_s = [], [], [], [], [], [], [], []
    mem_kv_p = []
    for i in range(DEPTH):
        mkv_p = (mem_prompt @ w_mem_kv[i]).reshape(nP, N_MEM, 2, MEM_HEADS, HEAD_DIM)
        mkv_s = cache_mem_kv[i]
        mem_kv_p.append(mkv_p)
        if i % 2 == 0:
            a = i // 2
            bufs = (cache_win1_kv[a], cache_win2_kv[a], cache_win3_kv[a])
            yp, kvp = mixer_dilated(hp, pos_p, w_in_A[a], w_out_A[a], mkv_p, None)
            ys, kvs = mixer_dilated(hs, pos_s, w_in_A[a], w_out_A[a], mkv_s, bufs)
            for g in range(N_DIL_GROUPS):
                win_p[g].append(kvp[g])
                win_s[g].append(kvs[g])
        else:
            b = i // 2
            wb = (w_in_B[b], mlstm_conv_w[b], mlstm_b_if[b], mlstm_norm_g[b], w_out_B[b])
            yp, cb, Cn, nn_, mm = mixer_mlstm(
                hp, *wb, mkv_p,
                jnp.zeros((nP, CONV_W - 1, 2 * MLSTM_WIDTH), hp.dtype),
                jnp.zeros((nP, MLSTM_HEADS, HEAD_DIM, HEAD_DIM), jnp.float32),
                jnp.zeros((nP, MLSTM_HEADS, HEAD_DIM), jnp.float32),
                jnp.zeros((nP, MLSTM_HEADS), jnp.float32))
            conv_p.append(cb); C_p.append(Cn); n_p.append(nn_); m_p.append(mm)
            ys, cb, Cn, nn_, mm = mixer_mlstm(
                hs, *wb, mkv_s, state_mlstm_conv[b], state_mlstm_C[b], state_mlstm_n[b], state_mlstm_m[b])
            conv_s.append(cb); C_s.append(Cn); n_s.append(nn_); m_s.append(mm)
        hp = layer_norm(ALPHA * hp + yp, ln_g[i, 0], ln_b[i, 0])
        hs = layer_norm(ALPHA * hs + ys, ln_g[i, 0], ln_b[i, 0])
        moe_w = (w_grp[i], b_grp[i], w_exp[i], b_exp[i], w_gate[i], w_up[i], w_down[i])
        hp = layer_norm(ALPHA * hp + hier_moe(hp, *moe_w), ln_g[i, 1], ln_b[i, 1])
        hs = layer_norm(ALPHA * hs + hier_moe(hs, *moe_w), ln_g[i, 1], ln_b[i, 1])
    return (hp, hs,
            jnp.stack(win_p[0]), jnp.stack(win_s[0]),
            jnp.stack(win_p[1]), jnp.stack(win_s[1]),
            jnp.stack(win_p[2]), jnp.stack(win_s[2]),
            jnp.stack(conv_p), jnp.stack(conv_s),
            jnp.stack(C_p), jnp.stack(C_s),
            jnp.stack(n_p), jnp.stack(n_s),
            jnp.stack(m_p), jnp.stack(m_s),
            jnp.stack(mem_kv_p))
```

```python
import functools

import jax
import jax.numpy as jnp
import numpy as np
from jax import lax
from jax.experimental import pallas as pl
from jax.experimental.pallas import tpu as pltpu

D_MODEL = 1024
HEAD_DIM = 64
ATTN_SCALE = HEAD_DIM ** -0.5
PAST_LEN = 16384
N_MEM = 256
MEM_HEADS = 4
MEM_WIDTH = MEM_HEADS * HEAD_DIM
DIL_PAIRS = ((128, 1), (512, 4), (2048, 16))
N_DIL_GROUPS = len(DIL_PAIRS)
HEADS_PER_GROUP = 4
DIL_WIDTH = HEADS_PER_GROUP * HEAD_DIM
QBLK = 128
ROPE_THETA = 500000.0
ROPE_DIMS = HEAD_DIM // 4
MLSTM_HEADS = 12
MLSTM_WIDTH = MLSTM_HEADS * HEAD_DIM
CONV_W = 4
N_GROUPS = 4
EXPERTS_PER_GROUP = 8
N_EXPERTS = N_GROUPS * EXPERTS_PER_GROUP
D_EXPERT = 256
DEPTH = 2
ALPHA = (2 * DEPTH) ** 0.25
LN_EPS = 1e-5

V7X_LANES = 128
V7X_SUBLANES = 8
V7X_VMEM_BYTES = 64 * 1024 * 1024
VMEM_LIMIT = 48 * 1024 * 1024

ROW_TILE = 512
SMALL_TILE = 128
MLSTM_CHUNK = 128
EXPERT_ROWS = 256

BF16 = jnp.bfloat16
F32 = jnp.float32
NEG_INF = float("-inf")


def _cparams(n_axes, vmem=None):
    return pltpu.CompilerParams(dimension_semantics=("arbitrary",) * n_axes,
                                vmem_limit_bytes=vmem)


def _nt_dot(a, b):
    return lax.dot_general(a, b, (((1,), (1,)), ((), ())), preferred_element_type=F32)


def _dot(a, b):
    return jnp.dot(a, b, preferred_element_type=F32)


def _rope_tables(pos):
    half = ROPE_DIMS // 2
    inv = jnp.power(ROPE_THETA, -jnp.arange(half, dtype=F32) * (2.0 / ROPE_DIMS))
    ang = pos.astype(F32)[:, None] * inv[None, :]
    cos, sin = jnp.cos(ang), jnp.sin(ang)
    n = pos.shape[0]
    one = jnp.ones((n, HEAD_DIM - ROPE_DIMS), F32)
    zero8 = jnp.zeros((n, half), F32)
    zrest = jnp.zeros((n, HEAD_DIM - ROPE_DIMS), F32)
    a = jnp.concatenate([cos, cos, one], axis=1)
    b = jnp.concatenate([zero8, sin, zrest], axis=1)
    c = jnp.concatenate([-sin, zero8, zrest], axis=1)
    rep = V7X_LANES // HEAD_DIM
    return jnp.tile(a, (1, rep)), jnp.tile(b, (1, rep)), jnp.tile(c, (1, rep))


def _rope_apply(x, ra, rb, rc):
    parts = []
    for s in range(x.shape[1] // V7X_LANES):
        v = x[:, s * V7X_LANES:(s + 1) * V7X_LANES]
        parts.append(v * ra + pltpu.roll(v, ROPE_DIMS // 2, 1) * rb
                     + pltpu.roll(v, V7X_LANES - ROPE_DIMS // 2, 1) * rc)
    return jnp.concatenate(parts, axis=1)


def _in_proj_a_body(x_ref, w_ref, ra_ref, rb_ref, rc_ref, *outs, kv_rows):
    q_refs, k_refs, v_refs = outs[0:3], outs[3:6], outs[6:9]
    mq_ref = outs[9]
    kv_refs = outs[10:13]
    xb = x_ref[...].astype(BF16)
    ra, rb, rc = ra_ref[...], rb_ref[...], rc_ref[...]
    tm = xb.shape[0]
    gw = DIL_WIDTH
    for g in range(N_DIL_GROUPS):
        q = _rope_apply(_dot(xb, w_ref[:, g * gw:(g + 1) * gw]), ra, rb, rc)
        k = _rope_apply(_dot(xb, w_ref[:, (3 + g) * gw:(4 + g) * gw]), ra, rb, rc)
        v = _dot(xb, w_ref[:, (6 + g) * gw:(7 + g) * gw])
        q_refs[g][...] = (q * ATTN_SCALE).astype(BF16)
        k_refs[g][...] = k.astype(BF16)
        v_refs[g][...] = v.astype(BF16)
        r = kv_rows[g]
        kv_refs[g][:, 0:gw] = k[tm - r:, :]
        kv_refs[g][:, gw:2 * gw] = v[tm - r:, :]
    mq = _dot(xb, w_ref[:, 9 * gw:9 * gw + MEM_WIDTH])
    mq_ref[...] = (mq * ATTN_SCALE).astype(BF16)


def _in_proj_a(x2d, w_bf, tabs, seq_len, n_seq, tile, full_kv):
    m = x2d.shape[0]
    nt = m // tile
    gw = DIL_WIDTH
    row_spec = lambda w: pl.BlockSpec((tile, w), lambda i: (i, 0))
    if full_kv:
        tab_spec = pl.BlockSpec((tile, V7X_LANES), lambda i: (i, 0))
        kv_rows = (tile,) * 3
        kv_shapes = [jax.ShapeDtypeStruct((m, 2 * gw), F32)] * 3
        kv_specs = [row_spec(2 * gw)] * 3
    else:
        tps = seq_len // tile
        tab_spec = pl.BlockSpec((tile, V7X_LANES), lambda i: (i % tps, 0))
        kv_rows, kv_shapes, kv_specs = [], [], []
        for win, _ in DIL_PAIRS:
            wb = min(win, seq_len)
            r = min(wb, tile)
            nblk = wb // r
            kv_rows.append(r)
            kv_shapes.append(jax.ShapeDtypeStruct((n_seq * wb, 2 * gw), F32))
            kv_specs.append(pl.BlockSpec(
                (r, 2 * gw),
                lambda i, nblk=nblk: ((i // tps) * nblk + jnp.maximum(i % tps - (tps - nblk), 0), 0)))
        kv_rows = tuple(kv_rows)
    out_shapes = [jax.ShapeDtypeStruct((m, gw), BF16)] * 9 + [jax.ShapeDtypeStruct((m, MEM_WIDTH), BF16)] + kv_shapes
    out_specs = [row_spec(gw)] * 9 + [row_spec(MEM_WIDTH)] + kv_specs
    return pl.pallas_call(
        functools.partial(_in_proj_a_body, kv_rows=kv_rows),
        grid=(nt,),
        in_specs=[row_spec(D_MODEL), pl.BlockSpec(w_bf.shape, lambda i: (0, 0)), tab_spec, tab_spec, tab_spec],
        out_specs=out_specs,
        out_shape=out_shapes,
        compiler_params=_cparams(1, VMEM_LIMIT),
        name="in_proj_a",
    )(x2d, w_bf, *tabs)


def _dil_attn_body(q_ref, k_ref, v_ref, o_ref, lse_ref, kprev, vprev, *, span):
    i = pl.program_id(2)

    @pl.when(i == 0)
    def _():
        kprev[...] = jnp.zeros_like(kprev)
        vprev[...] = jnp.zeros_like(vprev)

    q, k, v = q_ref[...], k_ref[...], v_ref[...]
    kp, vp = kprev[...], vprev[...]
    qi = lax.broadcasted_iota(jnp.int32, (QBLK, 2 * QBLK), 0) + QBLK
    ki = lax.broadcasted_iota(jnp.int32, (QBLK, 2 * QBLK), 1)
    band = (qi >= ki) & (qi - ki <= span) & ((i > 0) | (ki >= QBLK))
    lses = []
    for h in range(HEADS_PER_GROUP):
        hs = slice(h * HEAD_DIM, (h + 1) * HEAD_DIM)
        kc = jnp.concatenate([kp[:, hs], k[:, hs]], axis=0)
        vc = jnp.concatenate([vp[:, hs], v[:, hs]], axis=0)
        s = jnp.where(band, _nt_dot(q[:, hs], kc), NEG_INF)
        m = jnp.max(s, axis=1, keepdims=True)
        p = jnp.exp(s - m)
        den = jnp.sum(p, axis=1, keepdims=True)
        o_ref[:, hs] = _dot((p / den).astype(BF16), vc).astype(o_ref.dtype)
        lses.append(m + jnp.log(den))
    lse_ref[...] = jnp.concatenate(lses, axis=1)
    kprev[...] = k
    vprev[...] = v


def _dil_attn(q, k, v, n_seq, seq_len, dil, span):
    gw = DIL_WIDTH
    L = seq_len // dil
    nb = L // QBLK
    view = lambda t: t.reshape(n_seq, L, dil * gw)
    blk = pl.BlockSpec((None, QBLK, gw), lambda b, r, i: (b, i, r))
    o, lse = pl.pallas_call(
        functools.partial(_dil_attn_body, span=span),
        grid=(n_seq, dil, nb),
        in_specs=[blk, blk, blk],
        out_specs=[blk, pl.BlockSpec((None, None, QBLK, HEADS_PER_GROUP), lambda b, r, i: (b, r, i, 0))],
        out_shape=[jax.ShapeDtypeStruct((n_seq, L, dil * gw), BF16),
                   jax.ShapeDtypeStruct((n_seq, dil, L, HEADS_PER_GROUP), F32)],
        scratch_shapes=[pltpu.VMEM((QBLK, gw), BF16), pltpu.VMEM((QBLK, gw), BF16)],
        compiler_params=_cparams(3),
        name=f"dil_attn_d{dil}",
    )(view(q), view(k), view(v))
    o = o.reshape(n_seq * seq_len, gw)
    lse = jnp.transpose(lse, (0, 2, 1, 3)).reshape(n_seq * seq_len, HEADS_PER_GROUP)
    return o, lse


def _group_mix_body(o1, o2, o3, l1, l2, l3, mix_ref):
    os_ = (o1, o2, o3)
    ls = (l1[...], l2[...], l3[...])
    for h in range(HEADS_PER_GROUP):
        hs = slice(h * HEAD_DIM, (h + 1) * HEAD_DIM)
        lh = [l[:, h:h + 1] for l in ls]
        mx = jnp.maximum(jnp.maximum(lh[0], lh[1]), lh[2])
        e = [jnp.exp(x - mx) for x in lh]
        tot = e[0] + e[1] + e[2]
        acc = (e[0] / tot) * os_[0][:, hs].astype(F32)
        acc = acc + (e[1] / tot) * os_[1][:, hs].astype(F32)
        acc = acc + (e[2] / tot) * os_[2][:, hs].astype(F32)
        mix_ref[:, hs] = acc.astype(mix_ref.dtype)


def _group_mix(os_, lses, tile):
    m = os_[0].shape[0]
    gw = DIL_WIDTH
    ospec = pl.BlockSpec((tile, gw), lambda i: (i, 0))
    lspec = pl.BlockSpec((tile, HEADS_PER_GROUP), lambda i: (i, 0))
    return pl.pallas_call(
        _group_mix_body,
        grid=(m // tile,),
        in_specs=[ospec] * 3 + [lspec] * 3,
        out_specs=ospec,
        out_shape=jax.ShapeDtypeStruct((m, gw), BF16),
        compiler_params=_cparams(1),
        name="group_mix",
    )(*os_, *lses)


def _mem_attn_body(q_ref, kv_ref, o_ref):
    q = q_ref[...]
    kv = kv_ref[...].astype(BF16)
    for h in range(MEM_HEADS):
        hs = slice(h * HEAD_DIM, (h + 1) * HEAD_DIM)
        vs = slice(MEM_WIDTH + h * HEAD_DIM, MEM_WIDTH + (h + 1) * HEAD_DIM)
        s = _nt_dot(q[:, hs], kv[:, hs])
        m = jnp.max(s, axis=1, keepdims=True)
        p = jnp.exp(s - m)
        den = jnp.sum(p, axis=1, keepdims=True)
        o_ref[:, hs] = _dot((p / den).astype(BF16), kv[:, vs]).astype(o_ref.dtype)


def _mem_attn(q, mem_kv, seq_len, tile):
    m = q.shape[0]
    tps = seq_len // tile
    return pl.pallas_call(
        _mem_attn_body,
        grid=(m // tile,),
        in_specs=[pl.BlockSpec((tile, MEM_WIDTH), lambda i: (i, 0)),
                  pl.BlockSpec((None, N_MEM, 2 * MEM_WIDTH), lambda i: (i // tps, 0, 0))],
        out_specs=pl.BlockSpec((tile, MEM_WIDTH), lambda i: (i, 0)),
        out_shape=jax.ShapeDtypeStruct((m, MEM_WIDTH), BF16),
        compiler_params=_cparams(1),
        name="mem_attn",
    )(q, mem_kv)


def _layer_norm_rows(x, g, b):
    mu = jnp.mean(x, axis=1, keepdims=True)
    xc = x - mu
    var = jnp.mean(xc * xc, axis=1, keepdims=True)
    return xc * lax.rsqrt(var + LN_EPS) * g + b


def _out_proj_body(a_ref, mo_ref, h_ref, w_ref, g_ref, b_ref, *rest, n_tiles):
    o_ref = rest[-1]
    ka = a_ref.shape[1]

    def rows():
        y = _dot(a_ref[...].astype(BF16), w_ref[0:ka, :]) + _dot(mo_ref[...].astype(BF16), w_ref[ka:, :])
        o_ref[...] = _layer_norm_rows(ALPHA * h_ref[...] + y, g_ref[...], b_ref[...])

    if len(rest) == 1:
        rows()
    else:
        tail_ref = rest[0]
        pl.when(pl.program_id(0) < n_tiles)(rows)

        @pl.when(pl.program_id(0) == n_tiles)
        def _():
            o_ref[0:tail_ref.shape[0], :] = tail_ref[...]


def _out_proj(a, mo, h, h_off, w_bf, g, b, tile, tail=None):
    m, ka = a.shape
    nt = m // tile
    last = nt - 1
    row = lambda w, off=0: pl.BlockSpec((tile, w), lambda i: (jnp.minimum(i, last) + off // tile, 0))
    const = lambda s: pl.BlockSpec(s, lambda i: (0, 0))
    in_specs = [row(ka), row(MEM_WIDTH), row(D_MODEL, h_off), const(w_bf.shape), const((1, D_MODEL)),
                const((1, D_MODEL))]
    args = [a, mo, h, w_bf, g.reshape(1, D_MODEL), b.reshape(1, D_MODEL)]
    out_rows, steps = m, nt
    if tail is not None:
        assert tail.shape[0] <= tile
        in_specs.append(const(tail.shape))
        args.append(tail)
        out_rows, steps = m + tail.shape[0], nt + 1
    return pl.pallas_call(
        functools.partial(_out_proj_body, n_tiles=nt),
        grid=(steps,),
        in_specs=in_specs,
        out_specs=pl.BlockSpec((tile, D_MODEL), lambda i: (i, 0)),
        out_shape=jax.ShapeDtypeStruct((out_rows, D_MODEL), F32),
        compiler_params=_cparams(1, VMEM_LIMIT),
        name="out_proj_ln",
    )(*args)


def _matmul_body(x_ref, w_ref, o_ref):
    o_ref[...] = _dot(x_ref[...].astype(BF16), w_ref[...].astype(BF16))


def _matmul(x, w, tile):
    m, k = x.shape
    n = w.shape[1]
    return pl.pallas_call(
        _matmul_body,
        grid=(m // tile,),
        in_specs=[pl.BlockSpec((tile, k), lambda i: (i, 0)), pl.BlockSpec((k, n), lambda i: (0, 0))],
        out_specs=pl.BlockSpec((tile, n), lambda i: (i, 0)),
        out_shape=jax.ShapeDtypeStruct((m, n), F32),
        compiler_params=_cparams(1),
        name="mem_kv_proj",
    )(x, w)


def _col_attend(q_row, kmat, vmat, kmask, knew, vnew, nmask):
    pk = kmat * q_row
    pn = None if knew is None else knew * q_row
    outs, lses = [], []
    for h in range(HEADS_PER_GROUP):
        hs = slice(h * HEAD_DIM, (h + 1) * HEAD_DIM)
        s = jnp.sum(pk[:, hs], axis=1, keepdims=True)
        if kmask is not None:
            s = jnp.where(kmask, s, NEG_INF)
        m = jnp.max(s, axis=0, keepdims=True)
        if pn is not None:
            sn = jnp.where(nmask, jnp.sum(pn[:, hs], axis=1, keepdims=True), NEG_INF)
            m = jnp.maximum(m, jnp.max(sn, axis=0, keepdims=True))
        p = jnp.exp(s - m)
        den = jnp.sum(p, axis=0, keepdims=True)
        acc = jnp.sum(p * vmat[:, hs], axis=0, keepdims=True)
        if pn is not None:
            pnw = jnp.exp(sn - m)
            den = den + jnp.sum(pnw, axis=0, keepdims=True)
            acc = acc + jnp.sum(pnw * vnew[:, hs], axis=0, keepdims=True)
        outs.append(acc / den)
        lses.append(m + jnp.log(den))
    return outs, lses


def _sample_attn_body(q1, q2, q3, kn1, kn2, kn3, vn1, vn2, vn3, c1, c2, c3, mix_ref, *, n_new):
    qs = (q1[...], q2[...], q3[...])
    kns = (kn1[...], kn2[...], kn3[...])
    vns = (vn1[...], vn2[...], vn3[...])
    gw = DIL_WIDTH
    t_idx = lax.broadcasted_iota(jnp.int32, (n_new, 1), 0)
    rows = []
    for t in range(n_new):
        per_group = []
        for g, (win, dil) in enumerate(DIL_PAIRS):
            q_row = qs[g][t:t + 1, :]
            cache = (c1, c2, c3)[g]
            if dil == 1:
                kmat, vmat = cache[:, 0:gw], cache[:, gw:2 * gw]
                r_idx = lax.broadcasted_iota(jnp.int32, (kmat.shape[0], 1), 0)
                per_group.append(_col_attend(q_row, kmat, vmat, r_idx >= t, kns[g], vns[g], t_idx <= t))
            else:
                base = t * 2 * gw
                kmat, vmat = cache[:, base:base + gw], cache[:, base + gw:base + 2 * gw]
                per_group.append(_col_attend(q_row, kmat, vmat, None, kns[g], vns[g], t_idx == t))
        heads = []
        for h in range(HEADS_PER_GROUP):
            lh = [per_group[g][1][h] for g in range(N_DIL_GROUPS)]
            mx = jnp.maximum(jnp.maximum(lh[0], lh[1]), lh[2])
            e = [jnp.exp(x - mx) for x in lh]
            tot = e[0] + e[1] + e[2]
            acc = (e[0] / tot) * per_group[0][0][h]
            acc = acc + (e[1] / tot) * per_group[1][0][h]
            acc = acc + (e[2] / tot) * per_group[2][0][h]
            heads.append(acc)
        rows.append(jnp.concatenate(heads, axis=1))
    mix_ref[...] = jnp.concatenate(rows, axis=0)


def _sample_attn(qs, kns, vns, caches, n_b, n_new):
    gw = DIL_WIDTH
    small = pl.BlockSpec((None, n_new, gw), lambda b: (b, 0, 0))
    cviews, cspecs = [], []
    for c, (win, dil) in zip(caches, DIL_PAIRS):
        blocks = win // dil
        cviews.append(c.reshape(n_b, blocks, dil * 2 * gw))
        lanes = min(dil, n_new) * 2 * gw
        cspecs.append(pl.BlockSpec((None, blocks, lanes), lambda b: (b, 0, 0)))
    return pl.pallas_call(
        functools.partial(_sample_attn_body, n_new=n_new),
        grid=(n_b,),
        in_specs=[small] * 9 + cspecs,
        out_specs=small,
        out_shape=jax.ShapeDtypeStruct((n_b, n_new, gw), F32),
        compiler_params=_cparams(1, VMEM_LIMIT),
        name="sample_dil_attn",
    )(*qs, *kns, *vns, *cviews)


def _sample_mem_attn_body(q_ref, kv_ref, o_ref, *, n_new):
    q = q_ref[...]
    kmat, vmat = kv_ref[:, 0:MEM_WIDTH], kv_ref[:, MEM_WIDTH:2 * MEM_WIDTH]
    rows = []
    for t in range(n_new):
        outs, _ = _col_attend(q[t:t + 1, :], kmat, vmat, None, None, None, None)
        rows.append(jnp.concatenate(outs, axis=1))
    o_ref[...] = jnp.concatenate(rows, axis=0)


def _sample_mem_attn(q, mem_kv, n_b, n_new):
    small = pl.BlockSpec((None, n_new, MEM_WIDTH), lambda b: (b, 0, 0))
    return pl.pallas_call(
        functools.partial(_sample_mem_attn_body, n_new=n_new),
        grid=(n_b,),
        in_specs=[small, pl.BlockSpec((None, N_MEM, 2 * MEM_WIDTH), lambda b: (b, 0, 0))],
        out_specs=small,
        out_shape=jax.ShapeDtypeStruct((n_b, n_new, MEM_WIDTH), F32),
        compiler_params=_cparams(1),
        name="sample_mem_attn",
    )(q, mem_kv)


ROUTE_LANES = V7X_LANES


def _split_bf16(x):
    hi = x.astype(BF16)
    lo = (x - hi.astype(F32)).astype(BF16)
    return hi, lo


def _router_body(x_ref, whi_ref, wlo_ref, b_ref, o_ref):
    xh, xl = _split_bf16(x_ref[...])
    logits = _dot(xh, whi_ref[...]) + (_dot(xh, wlo_ref[...]) + _dot(xl, whi_ref[...])) + b_ref[...]
    lane = lax.broadcasted_iota(jnp.int32, logits.shape, 1)
    big = jnp.int32(ROUTE_LANES)
    is_grp = (lane >= N_EXPERTS) & (lane < N_EXPERTS + N_GROUPS)
    gl = jnp.where(is_grp, logits, NEG_INF)
    gmax = jnp.max(gl, axis=1, keepdims=True)
    gsel = jnp.min(jnp.where(gl == gmax, lane, big), axis=1, keepdims=True) - N_EXPERTS
    gp = 1.0 / jnp.sum(jnp.exp(gl - gmax), axis=1, keepdims=True)
    in_grp = (lane < N_EXPERTS) & ((lane // EXPERTS_PER_GROUP) == gsel)
    el = jnp.where(in_grp, logits, NEG_INF)
    v1 = jnp.max(el, axis=1, keepdims=True)
    i1 = jnp.min(jnp.where(el == v1, lane, big), axis=1, keepdims=True)
    el2 = jnp.where(lane == i1, NEG_INF, el)
    v2 = jnp.max(el2, axis=1, keepdims=True)
    i2 = jnp.min(jnp.where(el2 == v2, lane, big), axis=1, keepdims=True)
    e2 = jnp.exp(v2 - v1)
    w1 = (1.0 / (1.0 + e2)) * gp
    w2 = (e2 / (1.0 + e2)) * gp
    out = jnp.where(lane == 0, i1.astype(F32), 0.0)
    out = jnp.where(lane == 1, i2.astype(F32), out)
    out = jnp.where(lane == 2, w1, out)
    out = jnp.where(lane == 3, w2, out)
    o_ref[...] = out


def _router(x, w_grp, b_grp, w_exp, b_exp, tile):
    m = x.shape[0]
    pad = ROUTE_LANES - N_EXPERTS - N_GROUPS
    w = jnp.concatenate([w_exp, w_grp, jnp.zeros((D_MODEL, pad), F32)], axis=1)
    b = jnp.concatenate([b_exp, b_grp, jnp.zeros((pad,), F32)]).reshape(1, ROUTE_LANES)
    whi, wlo = _split_bf16(w)
    const = lambda s: pl.BlockSpec(s, lambda i: (0, 0))
    return pl.pallas_call(
        _router_body,
        grid=(pl.cdiv(m, tile),),
        in_specs=[pl.BlockSpec((tile, D_MODEL), lambda i: (i, 0)), const(w.shape), const(w.shape), const(b.shape)],
        out_specs=pl.BlockSpec((tile, ROUTE_LANES), lambda i: (i, 0)),
        out_shape=jax.ShapeDtypeStruct((m, ROUTE_LANES), F32),
        compiler_params=_cparams(1),
        name="moe_router",
    )(x, whi, wlo, b)


def _row_copy(src_hbm, src_row, dst_ref, dst_row, sem):
    return pltpu.make_async_copy(src_hbm.at[pl.ds(src_row, 1)], dst_ref.at[pl.ds(dst_row, 1)], sem)


def _expert_body(te_ref, nv_ref, tok_ref, tokn_ref, dst_ref, dstp_ref, x_hbm, wg_ref, wu_ref, wd_ref, out_hbm,
                 xbuf, ybuf, gsem, ssem):
    t = pl.program_id(0)
    nv = nv_ref[0]
    slot = t % 2
    rows = EXPERT_ROWS

    def gather(idx_ref, s, start):
        def body(j, c):
            cp = _row_copy(x_hbm, idx_ref[0, 0, j], xbuf.at[s], j, gsem.at[s])
            cp.start() if start else cp.wait()
            return c
        lax.fori_loop(0, rows, body, 0)

    def scatter(idx_ref, s, start):
        def body(j, c):
            d = idx_ref[0, 0, j]

            @pl.when(d >= 0)
            def _():
                cp = pltpu.make_async_copy(ybuf.at[s, pl.ds(j, 1)], out_hbm.at[pl.ds(d, 1)], ssem.at[s])
                cp.start() if start else cp.wait()
            return c
        lax.fori_loop(0, rows, body, 0)

    @pl.when(t < nv)
    def _():
        @pl.when(t == 0)
        def _():
            gather(tok_ref, 0, True)

        @pl.when(t + 1 < nv)
        def _():
            gather(tokn_ref, 1 - slot, True)

        gather(tok_ref, slot, False)
        xb = xbuf[slot].astype(BF16)
        hg = _dot(xb, wg_ref[...].astype(BF16))
        hu = _dot(xb, wu_ref[...].astype(BF16))
        he = (hg * jax.nn.sigmoid(hg)) * hu
        ybuf[slot] = _dot(he.astype(BF16), wd_ref[...].astype(BF16))
        scatter(dst_ref, slot, True)

        @pl.when(t >= 1)
        def _():
            scatter(dstp_ref, 1 - slot, False)

        @pl.when(t == nv - 1)
        def _():
            scatter(dst_ref, slot, False)


def _experts(x, row_tok, row_dst, tile_expert, n_valid, w_gate, w_up, w_down, n_out_rows):
    nt = tile_expert.shape[0]
    rows = EXPERT_ROWS
    tok3 = row_tok.reshape(nt, 1, rows)
    dst3 = row_dst.reshape(nt, 1, rows)
    smem = lambda f: pl.BlockSpec((1, 1, rows), f, memory_space=pltpu.SMEM)
    wspec = lambda shp: pl.BlockSpec((None,) + shp, lambda t, te, nv: (te[t], 0, 0))
    grid_spec = pltpu.PrefetchScalarGridSpec(
        num_scalar_prefetch=2,
        grid=(nt,),
        in_specs=[smem(lambda t, te, nv: (t, 0, 0)),
                  smem(lambda t, te, nv: (jnp.minimum(t + 1, nt - 1), 0, 0)),
                  smem(lambda t, te, nv: (t, 0, 0)),
                  smem(lambda t, te, nv: (jnp.maximum(t - 1, 0), 0, 0)),
                  pl.BlockSpec(memory_space=pl.ANY),
                  wspec((D_MODEL, D_EXPERT)), wspec((D_MODEL, D_EXPERT)), wspec((D_EXPERT, D_MODEL))],
        out_specs=pl.BlockSpec(memory_space=pl.ANY),
        scratch_shapes=[pltpu.VMEM((2, rows, D_MODEL), F32), pltpu.VMEM((2, rows, D_MODEL), F32),
                        pltpu.SemaphoreType.DMA((2,)), pltpu.SemaphoreType.DMA((2,))])
    return pl.pallas_call(
        _expert_body,
        grid_spec=grid_spec,
        out_shape=jax.ShapeDtypeStruct((n_out_rows, D_MODEL), F32),
        compiler_params=_cparams(1, VMEM_LIMIT),
        name="moe_experts",
    )(tile_expert, n_valid, tok3, tok3, dst3, dst3, x, w_gate, w_up, w_down)


def _moe_combine_body(h_ref, y1_ref, y2_ref, r_ref, g_ref, b_ref, o_ref):
    r = r_ref[...]
    moe = r[:, 2:3] * y1_ref[...] + r[:, 3:4] * y2_ref[...]
    o_ref[...] = _layer_norm_rows(ALPHA * h_ref[...] + moe, g_ref[...], b_ref[...])


def _moe_combine(h, y2, route, g, b, tile, row_off, n_rows):
    m = h.shape[0]
    y3 = y2.reshape(2, m, D_MODEL)
    ob = row_off // tile
    row = lambda w: pl.BlockSpec((tile, w), lambda i: (i + ob, 0))
    const = lambda s: pl.BlockSpec(s, lambda i: (0, 0))
    return pl.pallas_call(
        _moe_combine_body,
        grid=(n_rows // tile,),
        in_specs=[row(D_MODEL),
                  pl.BlockSpec((None, tile, D_MODEL), lambda i: (0, i + ob, 0)),
                  pl.BlockSpec((None, tile, D_MODEL), lambda i: (1, i + ob, 0)),
                  row(ROUTE_LANES), const((1, D_MODEL)), const((1, D_MODEL))],
        out_specs=pl.BlockSpec((tile, D_MODEL), lambda i: (i, 0)),
        out_shape=jax.ShapeDtypeStruct((n_rows, D_MODEL), F32),
        compiler_params=_cparams(1, VMEM_LIMIT),
        name="moe_combine_ln",
    )(h, y3, y3, route, g.reshape(1, D_MODEL), b.reshape(1, D_MODEL))


def _dispatch_plan(route, m):
    rows = EXPERT_ROWS
    nt = (2 * m + N_EXPERTS * (rows - 1)) // rows + 1
    flat_e = route[:, 0:2].astype(jnp.int32).reshape(-1)
    order = jnp.argsort(flat_e, stable=True).astype(jnp.int32)
    counts = jnp.sum(flat_e[:, None] == jnp.arange(N_EXPERTS, dtype=jnp.int32)[None, :], axis=0).astype(jnp.int32)
    tiles_per = (counts + rows - 1) // rows
    tile_end = jnp.cumsum(tiles_per)
    tile_start = tile_end - tiles_per
    cnt_start = jnp.cumsum(counts) - counts
    n_valid = tile_end[-1]
    ti = jnp.arange(nt, dtype=jnp.int32)
    te = jnp.minimum(jnp.searchsorted(tile_end, ti, side="right"), N_EXPERTS - 1).astype(jnp.int32)
    last_e = te[jnp.maximum(n_valid - 1, 0)]
    tile_expert = jnp.where(ti < n_valid, te, last_e)
    r = jnp.arange(nt * rows, dtype=jnp.int32)
    e_r = jnp.repeat(te, rows)
    rank = r - jnp.repeat(tile_start[te] * rows, rows)
    valid = (rank < counts[e_r]) & (jnp.repeat(ti, rows) < n_valid)
    flat = order[jnp.clip(cnt_start[e_r] + rank, 0, 2 * m - 1)]
    row_tok = jnp.where(valid, flat // 2, 0)
    row_dst = jnp.where(valid, (flat % 2) * m + flat // 2, -1)
    return row_tok, row_dst, tile_expert, n_valid.reshape(1).astype(jnp.int32)


def _moe_layer(h, w_grp, b_grp, w_exp, b_exp, w_gate, w_up, w_down, g, b, parts):
    m = h.shape[0]
    route = _router(h, w_grp, b_grp, w_exp, b_exp, SMALL_TILE)
    row_tok, row_dst, tile_expert, n_valid = _dispatch_plan(route, m)
    y2 = _experts(h, row_tok, row_dst, tile_expert, n_valid, w_gate, w_up, w_down, 2 * m)
    return [_moe_combine(h, y2, route, g, b, tile, off, n) for off, n, tile in parts]


GATE_LANES = V7X_LANES
W_B_COLS = 4 * MLSTM_WIDTH + GATE_LANES + MEM_WIDTH


def _log_sigmoid(x):
    return jnp.minimum(x, 0.0) - jnp.log1p(jnp.exp(-jnp.abs(x)))


def _silu(x):
    return x * jax.nn.sigmoid(x)


def _in_proj_b_body(x_ref, w_ref, cw_ref, bias_ref, p1_ref, p2_ref, p3_ref,
                    q_ref, k_ref, v_ref, og_ref, gt_ref, mq_ref, u_ref, carry, *, seq_rows):
    w = MLSTM_WIDTH
    xb = x_ref[...].astype(BF16)
    tm = xb.shape[0]
    u = _dot(xb, w_ref[:, 0:2 * w])
    row = lax.broadcasted_iota(jnp.int32, (tm, 1), 0)
    if seq_rows is None:
        @pl.when(pl.program_id(1) == 0)
        def _():
            carry[...] = p1_ref[...]
        uc = jnp.concatenate([carry[...], u], axis=0)
        shifted = [uc[V7X_SUBLANES - k:V7X_SUBLANES - k + tm, :] for k in (1, 2, 3)]
        carry[...] = u[tm - V7X_SUBLANES:, :]
        u_ref[...] = u[tm - (CONV_W - 1):, :]
    else:
        t = row % seq_rows
        prevs = (p1_ref[...], p2_ref[...], p3_ref[...])
        shifted = [jnp.where(t >= k, pltpu.roll(u, k, 0), prevs[k - 1]) for k in (1, 2, 3)]
        u_ref[...] = u
    cw = cw_ref[...]
    y = shifted[2] * cw[0:1, :]
    y = y + shifted[1] * cw[1:2, :]
    y = y + shifted[0] * cw[2:3, :]
    y = y + u * cw[3:4, :]
    qk = _silu(y)
    q_ref[...] = qk[:, 0:w].astype(BF16)
    k_ref[...] = (qk[:, w:2 * w] * ATTN_SCALE).astype(BF16)
    v_ref[...] = _dot(xb, w_ref[:, 2 * w:3 * w]).astype(BF16)
    og_ref[...] = jax.nn.sigmoid(_dot(xb, w_ref[:, 3 * w:4 * w])).astype(BF16)
    gates = _dot(xb, w_ref[:, 4 * w:4 * w + GATE_LANES]) + bias_ref[...]
    lane = lax.broadcasted_iota(jnp.int32, gates.shape, 1)
    gt_ref[...] = jnp.where(lane < MLSTM_HEADS, gates, _log_sigmoid(gates))
    mq_ref[...] = (_dot(xb, w_ref[:, 4 * w + GATE_LANES:]) * ATTN_SCALE).astype(BF16)


def _in_proj_b(x, x_off, m, w_bf, conv_w, bias, prevs, n_seq, seq_len, tile, per_tile_seqs):
    w = MLSTM_WIDTH
    tps = max(seq_len // tile, 1)
    ob = x_off // tile
    row = lambda width: pl.BlockSpec((tile, width), lambda b, i: (b * tps + i, 0))
    const = lambda s: pl.BlockSpec(s, lambda b, i: (0, 0))
    if per_tile_seqs:
        pspecs = [row(2 * w)] * 3
        u_rows, u_spec = m, row(2 * w)
    else:
        pspecs = [pl.BlockSpec((V7X_SUBLANES, 2 * w), lambda b, i: (b, 0))] * 3
        u_rows = n_seq * (CONV_W - 1)
        u_spec = pl.BlockSpec((None, CONV_W - 1, 2 * w), lambda b, i: (b, 0, 0))
    u_shape = (jax.ShapeDtypeStruct((m, 2 * w), F32) if per_tile_seqs
               else jax.ShapeDtypeStruct((n_seq, CONV_W - 1, 2 * w), F32))
    return pl.pallas_call(
        functools.partial(_in_proj_b_body, seq_rows=seq_len if per_tile_seqs else None),
        grid=(m // (tps * tile), tps),
        in_specs=[pl.BlockSpec((tile, D_MODEL), lambda b, i: (b * tps + i + ob, 0)),
                  const(w_bf.shape), const(conv_w.shape), const(bias.shape)] + pspecs,
        out_specs=[row(w), row(w), row(w), row(w), row(GATE_LANES), row(MEM_WIDTH), u_spec],
        out_shape=[jax.ShapeDtypeStruct((m, w), BF16)] * 4
                  + [jax.ShapeDtypeStruct((m, GATE_LANES), F32), jax.ShapeDtypeStruct((m, MEM_WIDTH), BF16), u_shape],
        scratch_shapes=[pltpu.VMEM((V7X_SUBLANES, 2 * w), F32)],
        compiler_params=_cparams(2, VMEM_LIMIT),
        name="in_proj_b",
    )(x, w_bf, conv_w, bias, *prevs)


def _mlstm_body(q_ref, k_ref, v_ref, og_ref, gt_ref, ng_ref, c0_ref, n0_ref, m0_ref,
                cell_ref, c_out, n_out, m_out, state, m_state):
    c = pl.program_id(1)
    L = q_ref.shape[0]
    E = HEAD_DIM

    @pl.when(c == 0)
    def _():
        for h in range(MLSTM_HEADS):
            state[h, :, 0:E] = c0_ref[h]
            state[h, :, E:2 * E] = jnp.where(lax.broadcasted_iota(jnp.int32, (E, E), 1) == 0, n0_ref[h], 0.0)
            m_state[h:h + 1, :] = jnp.broadcast_to(m0_ref[0:1, h:h + 1], (1, V7X_LANES))

    gt = gt_ref[...]
    row = lax.broadcasted_iota(jnp.int32, (L, 1), 0)
    b_all = gt
    sh = 1
    while sh < L:
        b_all = b_all + jnp.where(row >= sh, pltpu.roll(b_all, sh, 0), 0.0)
        sh *= 2
    ti = lax.broadcasted_iota(jnp.int32, (L, L), 0)
    si = lax.broadcasted_iota(jnp.int32, (L, L), 1)
    causal = ti >= si
    eye = ti == si
    ones_aug = jnp.ones((L, E), BF16)
    m_new_all = []
    for h in range(MLSTM_HEADS):
        hs = slice(h * E, (h + 1) * E)
        qh, kh, vh = q_ref[:, hs], k_ref[:, hs], v_ref[:, hs]
        ig_col = gt[:, h:h + 1]
        b_col = b_all[:, MLSTM_HEADS + h:MLSTM_HEADS + h + 1]
        m_prev = m_state[h:h + 1, 0:1]
        s_aug = state[h]
        r_col = ig_col - b_col
        r_row = jnp.sum(jnp.where(eye, r_col, 0.0), axis=0, keepdims=True)
        d = jnp.where(causal, b_col + r_row, NEG_INF)
        log_inter = b_col + m_prev
        mt = jnp.maximum(log_inter, jnp.max(d, axis=1, keepdims=True))
        w_intra = jnp.exp(d - mt)
        w_inter = jnp.exp(log_inter - mt)
        qk = _nt_dot(qh, kh) * w_intra
        inter = _dot(qh, s_aug.astype(BF16))
        num = _dot(qk.astype(BF16), vh) + w_inter * inter[:, 0:E]
        den = jnp.sum(qk, axis=1, keepdims=True) + w_inter * inter[:, E:E + 1]
        hv = num / jnp.maximum(jnp.abs(den), jnp.exp(-mt))
        mu = jnp.mean(hv, axis=1, keepdims=True)
        hc = hv - mu
        var = jnp.mean(hc * hc, axis=1, keepdims=True)
        hn = hc * lax.rsqrt(var + LN_EPS) * ng_ref[:, hs]
        cell_ref[:, hs] = (og_ref[:, hs].astype(F32) * hn).astype(cell_ref.dtype)
        b_last = b_col[L - 1:L, :]
        lw = b_last + r_col
        m_new = jnp.maximum(b_last + m_prev, jnp.max(lw, axis=0, keepdims=True))
        wk = (jnp.exp(lw - m_new) * kh.astype(F32)).astype(BF16)
        v_aug = jnp.concatenate([vh, ones_aug], axis=1)
        upd = lax.dot_general(wk, v_aug, (((0,), (0,)), ((), ())), preferred_element_type=F32)
        state[h] = jnp.exp(b_last + m_prev - m_new) * s_aug + upd
        m_state[h:h + 1, :] = jnp.broadcast_to(m_new, (1, V7X_LANES))
        m_new_all.append(m_new)

    for h in range(MLSTM_HEADS):
        c_out[h] = state[h, :, 0:E]
        n_out[h] = state[h, :, E:E + 1]
    m_out[...] = jnp.concatenate(m_new_all, axis=1)


def _mlstm(q, k, v, og, gt, norm_g, c0, n0, m0, n_seq, seq_len, chunk):
    w, hh, e = MLSTM_WIDTH, MLSTM_HEADS, HEAD_DIM
    nc = seq_len // chunk
    row = lambda width: pl.BlockSpec((chunk, width), lambda b, c: (b * nc + c, 0))
    st = lambda shp: pl.BlockSpec((None,) + shp, lambda b, c: (b,) + (0,) * len(shp))
    return pl.pallas_call(
        _mlstm_body,
        grid=(n_seq, nc),
        in_specs=[row(w), row(w), row(w), row(w), row(GATE_LANES), pl.BlockSpec((1, w), lambda b, c: (0, 0)),
                  st((hh, e, e)), st((hh, e, 1)), st((1, hh))],
        out_specs=[row(w), st((hh, e, e)), st((hh, e, 1)), st((1, hh))],
        out_shape=[jax.ShapeDtypeStruct((n_seq * seq_len, w), BF16),
                   jax.ShapeDtypeStruct((n_seq, hh, e, e), F32),
                   jax.ShapeDtypeStruct((n_seq, hh, e, 1), F32),
                   jax.ShapeDtypeStruct((n_seq, 1, hh), F32)],
        scratch_shapes=[pltpu.VMEM((hh, e, 2 * e), F32), pltpu.VMEM((2 * V7X_SUBLANES, V7X_LANES), F32)],
        compiler_params=_cparams(2, VMEM_LIMIT),
        name="mlstm_chunks",
    )(q, k, v, og, gt, norm_g.reshape(1, w), c0, n0.reshape(n_seq, hh, e, 1), m0.reshape(n_seq, 1, hh))


def _w_in_b_regroup(w_in, b_if):
    w4 = 4 * MLSTM_WIDTH
    ng = 2 * MLSTM_HEADS
    zc = jnp.zeros((D_MODEL, GATE_LANES - ng), w_in.dtype)
    w = jnp.concatenate([w_in[:, :w4 + ng], zc, w_in[:, w4 + ng:]], axis=1)
    bias = jnp.concatenate([b_if.reshape(-1).astype(F32), jnp.zeros((GATE_LANES - ng,), F32)]).reshape(1, GATE_LANES)
    return w.astype(BF16), bias


def _layer_b_prompt(h_all, w_bf, bias, conv_w, norm_g, w_out, mem_prompt, w_mem, ln_g, ln_b, n_seq, seq_len,
                    tail):
    m = n_seq * seq_len
    tile = min(ROW_TILE, seq_len)
    chunk = min(MLSTM_CHUNK, seq_len)
    zstate = jnp.zeros((n_seq * V7X_SUBLANES, 2 * MLSTM_WIDTH), F32)
    q, k, v, og, gt, mq, conv = _in_proj_b(h_all, 0, m, w_bf, conv_w, bias, [zstate] * 3, n_seq, seq_len, tile, False)
    hh, e = MLSTM_HEADS, HEAD_DIM
    cell, c_out, n_out, m_out = _mlstm(q, k, v, og, gt, norm_g, jnp.zeros((n_seq, hh, e, e), F32),
                                       jnp.zeros((n_seq, hh, e), F32), jnp.zeros((n_seq, hh), F32),
                                       n_seq, seq_len, chunk)
    mem_kv = _matmul(mem_prompt.reshape(n_seq * N_MEM, D_MODEL), w_mem, N_MEM)
    mo = _mem_attn(mq, mem_kv.reshape(n_seq, N_MEM, 2 * MEM_WIDTH), seq_len, tile)
    h1 = _out_proj(cell, mo, h_all, 0, w_out.astype(BF16), ln_g, ln_b, tile, tail)
    return h1, conv, c_out, n_out.reshape(n_seq, hh, e), m_out.reshape(n_seq, hh), mem_kv


def _layer_b_sample(h_all, h_off, w_bf, bias, conv_w, norm_g, w_out, mem_kv, conv_state, c0, n0, m0, ln_g, ln_b,
                    n_b, n_new):
    m = n_b * n_new
    w2 = 2 * MLSTM_WIDTH
    hh, e = MLSTM_HEADS, HEAD_DIM
    chunk = MLSTM_CHUNK
    t = jnp.arange(n_new)
    prevs = []
    for kk in (1, 2, 3):
        idx = jnp.clip(CONV_W - 1 - kk + t, 0, CONV_W - 2)
        prevs.append(conv_state[:, idx, :].reshape(m, w2))
    q, k, v, og, gt, mq, u = _in_proj_b(h_all, h_off, m, w_bf, conv_w, bias, prevs, n_b, n_new, m, True)
    conv_new = jnp.concatenate([conv_state, u.reshape(n_b, n_new, w2)], axis=1)[:, n_new:]
    pad3 = lambda a: jnp.pad(a.reshape(n_b, n_new, a.shape[1]), ((0, 0), (0, chunk - n_new), (0, 0)))
    lane = jnp.arange(GATE_LANES)
    gpad = jnp.broadcast_to(jnp.where(lane < hh, NEG_INF, 0.0).astype(F32), (n_b, chunk - n_new, GATE_LANES))
    gtp = jnp.concatenate([gt.reshape(n_b, n_new, GATE_LANES), gpad], axis=1)
    flat = lambda a: a.reshape(n_b * chunk, a.shape[2])
    cell, c_out, n_out, m_out = _mlstm(flat(pad3(q)), flat(pad3(k)), flat(pad3(v)), flat(pad3(og)), flat(gtp),
                                       norm_g, c0, n0, m0, n_b, chunk, chunk)
    cell = cell.reshape(n_b, chunk, MLSTM_WIDTH)[:, :n_new].reshape(m, MLSTM_WIDTH)
    mo = _sample_mem_attn(mq.astype(F32).reshape(n_b, n_new, MEM_WIDTH),
                          mem_kv.reshape(n_b, N_MEM, 2 * MEM_WIDTH), n_b, n_new)
    h1 = _out_proj(cell, mo.reshape(m, MEM_WIDTH), h_all, h_off, w_out.astype(BF16), ln_g, ln_b, m)
    return h1, conv_new, c_out, n_out.reshape(n_b, hh, e), m_out.reshape(n_b, hh)


def _layer_a_sample(h2d, w_in, w_out, mem_kv, caches, ln_g, ln_b, n_b, n_new):
    m = n_b * n_new
    pos = PAST_LEN + (jnp.arange(m, dtype=jnp.int32) % n_new)
    outs = _in_proj_a(h2d, w_in.astype(BF16), _rope_tables(pos), n_new, n_b, m, full_kv=True)
    qs, ks, vs, mq, kvs = outs[0:3], outs[3:6], outs[6:9], outs[9], outs[10:13]
    f3 = lambda t: t.astype(F32).reshape(n_b, n_new, t.shape[1])
    caches = [c.reshape(n_b, c.shape[1], 2 * DIL_WIDTH) for c in caches]
    mix = _sample_attn([f3(t) for t in qs], [f3(t) for t in ks], [f3(t) for t in vs], caches, n_b, n_new)
    mo = _sample_mem_attn(f3(mq), mem_kv.reshape(n_b, N_MEM, 2 * MEM_WIDTH), n_b, n_new)
    h1 = _out_proj(mix.reshape(m, DIL_WIDTH), mo.reshape(m, MEM_WIDTH), h2d, 0, w_out.astype(BF16), ln_g, ln_b, m)
    return h1, kvs


def _layer_a_prompt(h2d, mem_prompt, w_in, w_out, w_mem, ln_g, ln_b, n_seq, seq_len, tail):
    tile = min(ROW_TILE, seq_len)
    tabs = _rope_tables(jnp.arange(seq_len, dtype=jnp.int32))
    outs = _in_proj_a(h2d, w_in.astype(BF16), tabs, seq_len, n_seq, tile, full_kv=False)
    qs, ks, vs, mq, kvs = outs[0:3], outs[3:6], outs[6:9], outs[9], outs[10:13]
    mem_kv = _matmul(mem_prompt.reshape(n_seq * N_MEM, D_MODEL), w_mem, N_MEM)
    os_, lses = [], []
    for g, (win, dil) in enumerate(DIL_PAIRS):
        o, lse = _dil_attn(qs[g], ks[g], vs[g], n_seq, seq_len, dil, win // dil)
        os_.append(o)
        lses.append(lse)
    mix = _group_mix(os_, lses, tile)
    mo = _mem_attn(mq, mem_kv.reshape(n_seq, N_MEM, 2 * MEM_WIDTH), seq_len, tile)
    h1 = _out_proj(mix, mo, h2d, 0, w_out.astype(BF16), ln_g, ln_b, tile, tail)
    return h1, kvs, mem_kv


def kernel(x_prompt, x_sample, mem_prompt, cache_win1_kv, cache_win2_kv, cache_win3_kv, cache_mem_kv,
           state_mlstm_conv, state_mlstm_C, state_mlstm_n, state_mlstm_m,
           w_in_A, w_out_A, w_in_B, mlstm_conv_w, mlstm_b_if, mlstm_norm_g, w_out_B,
           w_mem_kv, ln_g, ln_b, w_grp, b_grp, w_exp, b_exp, w_gate, w_up, w_down):
    n_p, seq, d = x_prompt.shape
    n_s, t_new, _ = x_sample.shape
    assert d == D_MODEL and w_in_A.shape[0] == 1 and w_in_B.shape[0] == 1
    mp, ms = n_p * seq, n_s * t_new
    m_all = mp + ms
    assert mp % ROW_TILE == 0 and ms % SMALL_TILE == 0 and mp % ms == 0
    xp = x_prompt.reshape(mp, d)
    xs = x_sample.reshape(ms, d)

    def moe(h, i, parts):
        return _moe_layer(h, w_grp[i], b_grp[i], w_exp[i], b_exp[i], w_gate[i], w_up[i], w_down[i],
                          ln_g[i, 1], ln_b[i, 1], parts)

    caches = (cache_win1_kv[0], cache_win2_kv[0], cache_win3_kv[0])
    hs, kv_s = _layer_a_sample(xs, w_in_A[0], w_out_A[0], cache_mem_kv[0], caches, ln_g[0, 0], ln_b[0, 0],
                               n_s, t_new)
    h, kv_p, mem_kv0 = _layer_a_prompt(xp, mem_prompt, w_in_A[0], w_out_A[0], w_mem_kv[0], ln_g[0, 0], ln_b[0, 0],
                                       n_p, seq, hs)
    (h,) = moe(h, 0, [(0, m_all, SMALL_TILE)])

    w_bf, bias = _w_in_b_regroup(w_in_B[0], mlstm_b_if[0])
    hs, conv_s, c_s, nv_s, mm_s = _layer_b_sample(
        h, mp, w_bf, bias, mlstm_conv_w[0], mlstm_norm_g[0], w_out_B[0], cache_mem_kv[1],
        state_mlstm_conv[0], state_mlstm_C[0], state_mlstm_n[0], state_mlstm_m[0],
        ln_g[1, 0], ln_b[1, 0], n_s, t_new)
    h1, conv_p, c_p, nv_p, mm_p, mem_kv1 = _layer_b_prompt(
        h, w_bf, bias, mlstm_conv_w[0], mlstm_norm_g[0], w_out_B[0], mem_prompt, w_mem_kv[1],
        ln_g[1, 0], ln_b[1, 0], n_p, seq, hs)
    yp, ys = moe(h1, 1, [(0, mp, ROW_TILE), (mp, ms, SMALL_TILE)])

    kv_shape = (2, HEADS_PER_GROUP, HEAD_DIM)
    wins = []
    for g in range(N_DIL_GROUPS):
        wins.append(kv_p[g].reshape((1, n_p, -1) + kv_shape))
        wins.append(kv_s[g].reshape((1, n_s, t_new) + kv_shape))
    mem_kv_p = jnp.stack([mem_kv0, mem_kv1]).reshape((2, n_p, N_MEM, 2, MEM_HEADS, HEAD_DIM))
    return (yp.reshape(n_p, seq, d), ys.reshape(n_s, t_new, d), *wins,
            conv_p[None], conv_s[None], c_p[None], c_s[None], nv_p[None], nv_s[None], mm_p[None], mm_s[None],
            mem_kv_p)
```

```python
import functools

import jax
import jax.numpy as jnp
import numpy as np
from jax import lax
from jax.experimental import pallas as pl
from jax.experimental.pallas import tpu as pltpu

D_MODEL = 1024
HEAD_DIM = 64
ATTN_SCALE = HEAD_DIM ** -0.5
PAST_LEN = 16384
N_MEM = 256
MEM_HEADS = 4
MEM_WIDTH = MEM_HEADS * HEAD_DIM
DIL_PAIRS = ((128, 1), (512, 4), (2048, 16))
N_DIL_GROUPS = len(DIL_PAIRS)
HEADS_PER_GROUP = 4
DIL_WIDTH = HEADS_PER_GROUP * HEAD_DIM
QBLK = 128
ROPE_THETA = 500000.0
ROPE_DIMS = HEAD_DIM // 4
MLSTM_HEADS = 12
MLSTM_WIDTH = MLSTM_HEADS * HEAD_DIM
CONV_W = 4
N_GROUPS = 4
EXPERTS_PER_GROUP = 8
N_EXPERTS = N_GROUPS * EXPERTS_PER_GROUP
D_EXPERT = 256
DEPTH = 2
ALPHA = (2 * DEPTH) ** 0.25
LN_EPS = 1e-5

V7X_LANES = 128
V7X_SUBLANES = 8
V7X_VMEM_BYTES = 64 * 1024 * 1024
VMEM_LIMIT = 48 * 1024 * 1024

ROW_TILE = 512
SMALL_TILE = 128
MLSTM_CHUNK = 128
EXPERT_ROWS = 256

BF16 = jnp.bfloat16
F32 = jnp.float32
NEG_INF = float("-inf")


def _cparams(n_axes, vmem=None):
    return pltpu.CompilerParams(dimension_semantics=("arbitrary",) * n_axes,
                                vmem_limit_bytes=vmem)


def _nt_dot(a, b):
    return lax.dot_general(a, b, (((1,), (1,)), ((), ())), preferred_element_type=F32)


def _dot(a, b):
    return jnp.dot(a, b, preferred_element_type=F32)


def _split_bf16(x):
    hi = x.astype(BF16)
    lo = (x - hi.astype(F32)).astype(BF16)
    return hi, lo


def _lhs(x, precise):
    return _split_bf16(x) if precise else (x.astype(BF16),)


def _mm(lhs, w):
    if len(lhs) == 1:
        return _dot(lhs[0], w)
    wh, wl = _split_bf16(w)
    return _dot(lhs[0], wh) + (_dot(lhs[0], wl) + _dot(lhs[1], wh))


def _rope_tables(pos):
    half = ROPE_DIMS // 2
    inv = jnp.power(ROPE_THETA, -jnp.arange(half, dtype=F32) * (2.0 / ROPE_DIMS))
    ang = pos.astype(F32)[:, None] * inv[None, :]
    cos, sin = jnp.cos(ang), jnp.sin(ang)
    n = pos.shape[0]
    one = jnp.ones((n, HEAD_DIM - ROPE_DIMS), F32)
    zero8 = jnp.zeros((n, half), F32)
    zrest = jnp.zeros((n, HEAD_DIM - ROPE_DIMS), F32)
    a = jnp.concatenate([cos, cos, one], axis=1)
    b = jnp.concatenate([zero8, sin, zrest], axis=1)
    c = jnp.concatenate([-sin, zero8, zrest], axis=1)
    rep = V7X_LANES // HEAD_DIM
    return jnp.tile(a, (1, rep)), jnp.tile(b, (1, rep)), jnp.tile(c, (1, rep))


def _rope_apply(x, ra, rb, rc):
    parts = []
    for s in range(x.shape[1] // V7X_LANES):
        v = x[:, s * V7X_LANES:(s + 1) * V7X_LANES]
        parts.append(v * ra + pltpu.roll(v, ROPE_DIMS // 2, 1) * rb
                     + pltpu.roll(v, V7X_LANES - ROPE_DIMS // 2, 1) * rc)
    return jnp.concatenate(parts, axis=1)


def _in_proj_a_body(x_ref, w_ref, ra_ref, rb_ref, rc_ref, *outs, kv_rows, precise):
    q_refs, k_refs, v_refs = outs[0:3], outs[3:6], outs[6:9]
    mq_ref = outs[9]
    kv_refs = outs[10:13]
    lhs = _lhs(x_ref[...], precise)
    ra, rb, rc = ra_ref[...], rb_ref[...], rc_ref[...]
    tm = x_ref.shape[0]
    gw = DIL_WIDTH
    for g in range(N_DIL_GROUPS):
        q = _rope_apply(_mm(lhs, w_ref[:, g * gw:(g + 1) * gw]), ra, rb, rc)
        k = _rope_apply(_mm(lhs, w_ref[:, (3 + g) * gw:(4 + g) * gw]), ra, rb, rc)
        v = _mm(lhs, w_ref[:, (6 + g) * gw:(7 + g) * gw])
        q_refs[g][...] = (q * ATTN_SCALE).astype(q_refs[g].dtype)
        k_refs[g][...] = k.astype(k_refs[g].dtype)
        v_refs[g][...] = v.astype(v_refs[g].dtype)
        r = kv_rows[g]
        kv_refs[g][:, 0:gw] = k[tm - r:, :]
        kv_refs[g][:, gw:2 * gw] = v[tm - r:, :]
    mq = _mm(lhs, w_ref[:, 9 * gw:9 * gw + MEM_WIDTH])
    mq_ref[...] = (mq * ATTN_SCALE).astype(mq_ref.dtype)


def _in_proj_a(x2d, w, tabs, seq_len, n_seq, tile, full_kv):
    w_bf = w
    act = F32 if full_kv else BF16
    m = x2d.shape[0]
    nt = m // tile
    gw = DIL_WIDTH
    row_spec = lambda w: pl.BlockSpec((tile, w), lambda i: (i, 0))
    if full_kv:
        tab_spec = pl.BlockSpec((tile, V7X_LANES), lambda i: (i, 0))
        kv_rows = (tile,) * 3
        kv_shapes = [jax.ShapeDtypeStruct((m, 2 * gw), F32)] * 3
        kv_specs = [row_spec(2 * gw)] * 3
    else:
        tps = seq_len // tile
        tab_spec = pl.BlockSpec((tile, V7X_LANES), lambda i: (i % tps, 0))
        kv_rows, kv_shapes, kv_specs = [], [], []
        for win, _ in DIL_PAIRS:
            wb = min(win, seq_len)
            r = min(wb, tile)
            nblk = wb // r
            kv_rows.append(r)
            kv_shapes.append(jax.ShapeDtypeStruct((n_seq * wb, 2 * gw), F32))
            kv_specs.append(pl.BlockSpec(
                (r, 2 * gw),
                lambda i, nblk=nblk: ((i // tps) * nblk + jnp.maximum(i % tps - (tps - nblk), 0), 0)))
        kv_rows = tuple(kv_rows)
    out_shapes = [jax.ShapeDtypeStruct((m, gw), act)] * 9 + [jax.ShapeDtypeStruct((m, MEM_WIDTH), act)] + kv_shapes
    out_specs = [row_spec(gw)] * 9 + [row_spec(MEM_WIDTH)] + kv_specs
    return pl.pallas_call(
        functools.partial(_in_proj_a_body, kv_rows=kv_rows, precise=full_kv),
        grid=(nt,),
        in_specs=[row_spec(D_MODEL), pl.BlockSpec(w_bf.shape, lambda i: (0, 0)), tab_spec, tab_spec, tab_spec],
        out_specs=out_specs,
        out_shape=out_shapes,
        compiler_params=_cparams(1, VMEM_LIMIT),
        name="in_proj_a",
    )(x2d, w_bf, *tabs)


def _dil_attn_body(q_ref, k_ref, v_ref, kp_ref, vp_ref, o_ref, lse_ref, *, span):
    i = pl.program_id(2)
    q, k, v = q_ref[...], k_ref[...], v_ref[...]
    kp, vp = kp_ref[...], vp_ref[...]
    qi = lax.broadcasted_iota(jnp.int32, (QBLK, 2 * QBLK), 0) + QBLK
    ki = lax.broadcasted_iota(jnp.int32, (QBLK, 2 * QBLK), 1)
    band = (qi >= ki) & (qi - ki <= span) & ((i > 0) | (ki >= QBLK))
    lses = []
    for h in range(HEADS_PER_GROUP):
        hs = slice(h * HEAD_DIM, (h + 1) * HEAD_DIM)
        kc = jnp.concatenate([kp[:, hs], k[:, hs]], axis=0)
        vc = jnp.concatenate([vp[:, hs], v[:, hs]], axis=0)
        s = jnp.where(band, _nt_dot(q[:, hs], kc), NEG_INF)
        m = jnp.max(s, axis=1, keepdims=True)
        p = jnp.exp(s - m)
        den = jnp.sum(p, axis=1, keepdims=True)
        o_ref[:, hs] = _dot((p / den).astype(BF16), vc).astype(o_ref.dtype)
        lses.append(m + jnp.log(den))
    lse_ref[...] = jnp.concatenate(lses, axis=1)


def _dil_attn(q, k, v, n_seq, seq_len, dil, span):
    gw = DIL_WIDTH
    L = seq_len // dil
    nb = L // QBLK
    view = lambda t: t.reshape(n_seq, L, dil * gw)
    blk = pl.BlockSpec((None, QBLK, gw), lambda b, r, i: (b, i, r))
    prev = pl.BlockSpec((None, QBLK, gw), lambda b, r, i: (b, jnp.maximum(i - 1, 0), r))
    o, lse = pl.pallas_call(
        functools.partial(_dil_attn_body, span=span),
        grid=(n_seq, dil, nb),
        in_specs=[blk, blk, blk, prev, prev],
        out_specs=[blk, pl.BlockSpec((None, None, QBLK, HEADS_PER_GROUP), lambda b, r, i: (b, r, i, 0))],
        out_shape=[jax.ShapeDtypeStruct((n_seq, L, dil * gw), BF16),
                   jax.ShapeDtypeStruct((n_seq, dil, L, HEADS_PER_GROUP), F32)],
        compiler_params=_cparams(3),
        name=f"dil_attn_d{dil}",
    )(view(q), view(k), view(v), view(k), view(v))
    o = o.reshape(n_seq * seq_len, gw)
    lse = jnp.transpose(lse, (0, 2, 1, 3)).reshape(n_seq * seq_len, HEADS_PER_GROUP)
    return o, lse


def _group_mix_body(o1, o2, o3, l1, l2, l3, mix_ref):
    os_ = (o1, o2, o3)
    ls = (l1[...], l2[...], l3[...])
    for h in range(HEADS_PER_GROUP):
        hs = slice(h * HEAD_DIM, (h + 1) * HEAD_DIM)
        lh = [l[:, h:h + 1] for l in ls]
        mx = jnp.maximum(jnp.maximum(lh[0], lh[1]), lh[2])
        e = [jnp.exp(x - mx) for x in lh]
        tot = e[0] + e[1] + e[2]
        acc = (e[0] / tot) * os_[0][:, hs].astype(F32)
        acc = acc + (e[1] / tot) * os_[1][:, hs].astype(F32)
        acc = acc + (e[2] / tot) * os_[2][:, hs].astype(F32)
        mix_ref[:, hs] = acc.astype(mix_ref.dtype)


def _group_mix(os_, lses, tile):
    m = os_[0].shape[0]
    gw = DIL_WIDTH
    ospec = pl.BlockSpec((tile, gw), lambda i: (i, 0))
    lspec = pl.BlockSpec((tile, HEADS_PER_GROUP), lambda i: (i, 0))
    return pl.pallas_call(
        _group_mix_body,
        grid=(m // tile,),
        in_specs=[ospec] * 3 + [lspec] * 3,
        out_specs=ospec,
        out_shape=jax.ShapeDtypeStruct((m, gw), BF16),
        compiler_params=_cparams(1),
        name="group_mix",
    )(*os_, *lses)


def _mem_attn_body(q_ref, kv_ref, o_ref):
    q = q_ref[...]
    kv = kv_ref[...].astype(BF16)
    for h in range(MEM_HEADS):
        hs = slice(h * HEAD_DIM, (h + 1) * HEAD_DIM)
        vs = slice(MEM_WIDTH + h * HEAD_DIM, MEM_WIDTH + (h + 1) * HEAD_DIM)
        s = _nt_dot(q[:, hs], kv[:, hs])
        m = jnp.max(s, axis=1, keepdims=True)
        p = jnp.exp(s - m)
        den = jnp.sum(p, axis=1, keepdims=True)
        o_ref[:, hs] = _dot((p / den).astype(BF16), kv[:, vs]).astype(o_ref.dtype)


def _mem_attn(q, mem_kv, seq_len, tile):
    m = q.shape[0]
    tps = seq_len // tile
    return pl.pallas_call(
        _mem_attn_body,
        grid=(m // tile,),
        in_specs=[pl.BlockSpec((tile, MEM_WIDTH), lambda i: (i, 0)),
                  pl.BlockSpec((None, N_MEM, 2 * MEM_WIDTH), lambda i: (i // tps, 0, 0))],
        out_specs=pl.BlockSpec((tile, MEM_WIDTH), lambda i: (i, 0)),
        out_shape=jax.ShapeDtypeStruct((m, MEM_WIDTH), BF16),
        compiler_params=_cparams(1),
        name="mem_attn",
    )(q, mem_kv)


def _layer_norm_rows(x, g, b):
    mu = jnp.mean(x, axis=1, keepdims=True)
    xc = x - mu
    var = jnp.mean(xc * xc, axis=1, keepdims=True)
    return xc * lax.rsqrt(var + LN_EPS) * g + b


def _out_proj_body(a_ref, mo_ref, h_ref, w_ref, g_ref, b_ref, *rest, n_tiles, precise):
    o_ref = rest[-1]
    ka = a_ref.shape[1]

    def rows():
        y = _mm(_lhs(a_ref[...], precise), w_ref[0:ka, :]) + _mm(_lhs(mo_ref[...], precise), w_ref[ka:, :])
        o_ref[...] = _layer_norm_rows(ALPHA * h_ref[...] + y, g_ref[...], b_ref[...])

    if len(rest) == 1:
        rows()
    else:
        tail_ref = rest[0]
        pl.when(pl.program_id(0) < n_tiles)(rows)

        @pl.when(pl.program_id(0) == n_tiles)
        def _():
            o_ref[0:tail_ref.shape[0], :] = tail_ref[...]


def _out_proj(a, mo, h, h_off, w, g, b, tile, tail=None):
    w_bf = w
    m, ka = a.shape
    nt = m // tile
    last = nt - 1
    row = lambda w, off=0: pl.BlockSpec((tile, w), lambda i: (jnp.minimum(i, last) + off // tile, 0))
    const = lambda s: pl.BlockSpec(s, lambda i: (0, 0))
    in_specs = [row(ka), row(MEM_WIDTH), row(D_MODEL, h_off), const(w_bf.shape), const((1, D_MODEL)),
                const((1, D_MODEL))]
    args = [a, mo, h, w_bf, g.reshape(1, D_MODEL), b.reshape(1, D_MODEL)]
    out_rows, steps = m, nt
    if tail is not None:
        assert tail.shape[0] <= tile
        in_specs.append(const(tail.shape))
        args.append(tail)
        out_rows, steps = m + tail.shape[0], nt + 1
    return pl.pallas_call(
        functools.partial(_out_proj_body, n_tiles=nt, precise=(w.dtype == F32)),
        grid=(steps,),
        in_specs=in_specs,
        out_specs=pl.BlockSpec((tile, D_MODEL), lambda i: (i, 0)),
        out_shape=jax.ShapeDtypeStruct((out_rows, D_MODEL), F32),
        compiler_params=_cparams(1, VMEM_LIMIT),
        name="out_proj_ln",
    )(*args)


def _matmul_body(x_ref, w_ref, o_ref):
    o_ref[...] = _dot(x_ref[...].astype(BF16), w_ref[...].astype(BF16))


def _matmul(x, w, tile):
    m, k = x.shape
    n = w.shape[1]
    return pl.pallas_call(
        _matmul_body,
        grid=(m // tile,),
        in_specs=[pl.BlockSpec((tile, k), lambda i: (i, 0)), pl.BlockSpec((k, n), lambda i: (0, 0))],
        out_specs=pl.BlockSpec((tile, n), lambda i: (i, 0)),
        out_shape=jax.ShapeDtypeStruct((m, n), F32),
        compiler_params=_cparams(1),
        name="mem_kv_proj",
    )(x, w)


def _col_attend(q_row, kmat, vmat, kmask, knew, vnew, nmask):
    pk = kmat * q_row
    pn = None if knew is None else knew * q_row
    outs, lses = [], []
    for h in range(HEADS_PER_GROUP):
        hs = slice(h * HEAD_DIM, (h + 1) * HEAD_DIM)
        s = jnp.sum(pk[:, hs], axis=1, keepdims=True)
        if kmask is not None:
            s = jnp.where(kmask, s, NEG_INF)
        m = jnp.max(s, axis=0, keepdims=True)
        if pn is not None:
            sn = jnp.where(nmask, jnp.sum(pn[:, hs], axis=1, keepdims=True), NEG_INF)
            m = jnp.maximum(m, jnp.max(sn, axis=0, keepdims=True))
        p = jnp.exp(s - m)
        den = jnp.sum(p, axis=0, keepdims=True)
        acc = jnp.sum(p * vmat[:, hs], axis=0, keepdims=True)
        if pn is not None:
            pnw = jnp.exp(sn - m)
            den = den + jnp.sum(pnw, axis=0, keepdims=True)
            acc = acc + jnp.sum(pnw * vnew[:, hs], axis=0, keepdims=True)
        outs.append(acc / den)
        lses.append(m + jnp.log(den))
    return outs, lses


def _sample_attn_body(q1, q2, q3, kn1, kn2, kn3, vn1, vn2, vn3, c1, c2, c3, mix_ref, *, n_new):
    qs = (q1[...], q2[...], q3[...])
    kns = (kn1[...], kn2[...], kn3[...])
    vns = (vn1[...], vn2[...], vn3[...])
    gw = DIL_WIDTH
    t_idx = lax.broadcasted_iota(jnp.int32, (n_new, 1), 0)
    rows = []
    for t in range(n_new):
        per_group = []
        for g, (win, dil) in enumerate(DIL_PAIRS):
            q_row = qs[g][t:t + 1, :]
            cache = (c1, c2, c3)[g]
            if dil == 1:
                kmat, vmat = cache[:, 0:gw], cache[:, gw:2 * gw]
                r_idx = lax.broadcasted_iota(jnp.int32, (kmat.shape[0], 1), 0)
                per_group.append(_col_attend(q_row, kmat, vmat, r_idx >= t, kns[g], vns[g], t_idx <= t))
            else:
                base = t * 2 * gw
                kmat, vmat = cache[:, base:base + gw], cache[:, base + gw:base + 2 * gw]
                per_group.append(_col_attend(q_row, kmat, vmat, None, kns[g], vns[g], t_idx == t))
        heads = []
        for h in range(HEADS_PER_GROUP):
            lh = [per_group[g][1][h] for g in range(N_DIL_GROUPS)]
            mx = jnp.maximum(jnp.maximum(lh[0], lh[1]), lh[2])
            e = [jnp.exp(x - mx) for x in lh]
            tot = e[0] + e[1] + e[2]
            acc = (e[0] / tot) * per_group[0][0][h]
            acc = acc + (e[1] / tot) * per_group[1][0][h]
            acc = acc + (e[2] / tot) * per_group[2][0][h]
            heads.append(acc)
        rows.append(jnp.concatenate(heads, axis=1))
    mix_ref[...] = jnp.concatenate(rows, axis=0)


def _sample_attn(qs, kns, vns, caches, n_b, n_new):
    gw = DIL_WIDTH
    small = pl.BlockSpec((None, n_new, gw), lambda b: (b, 0, 0))
    cviews, cspecs = [], []
    for c, (win, dil) in zip(caches, DIL_PAIRS):
        blocks = win // dil
        cviews.append(c.reshape(n_b, blocks, dil * 2 * gw))
        lanes = min(dil, n_new) * 2 * gw
        cspecs.append(pl.BlockSpec((None, blocks, lanes), lambda b: (b, 0, 0)))
    return pl.pallas_call(
        functools.partial(_sample_attn_body, n_new=n_new),
        grid=(n_b,),
        in_specs=[small] * 9 + cspecs,
        out_specs=small,
        out_shape=jax.ShapeDtypeStruct((n_b, n_new, gw), F32),
        compiler_params=_cparams(1, VMEM_LIMIT),
        name="sample_dil_attn",
    )(*qs, *kns, *vns, *cviews)


def _sample_mem_attn_body(q_ref, kv_ref, o_ref, *, n_new):
    q = q_ref[...]
    kmat, vmat = kv_ref[:, 0:MEM_WIDTH], kv_ref[:, MEM_WIDTH:2 * MEM_WIDTH]
    rows = []
    for t in range(n_new):
        outs, _ = _col_attend(q[t:t + 1, :], kmat, vmat, None, None, None, None)
        rows.append(jnp.concatenate(outs, axis=1))
    o_ref[...] = jnp.concatenate(rows, axis=0)


def _sample_mem_attn(q, mem_kv, n_b, n_new):
    small = pl.BlockSpec((None, n_new, MEM_WIDTH), lambda b: (b, 0, 0))
    return pl.pallas_call(
        functools.partial(_sample_mem_attn_body, n_new=n_new),
        grid=(n_b,),
        in_specs=[small, pl.BlockSpec((None, N_MEM, 2 * MEM_WIDTH), lambda b: (b, 0, 0))],
        out_specs=small,
        out_shape=jax.ShapeDtypeStruct((n_b, n_new, MEM_WIDTH), F32),
        compiler_params=_cparams(1),
        name="sample_mem_attn",
    )(q, mem_kv)


ROUTE_LANES = V7X_LANES


def _router_body(x_ref, w_ref, b_ref, o_ref, cnt_ref):
    xh, xl = _split_bf16(x_ref[...])
    wh, wl = _split_bf16(w_ref[...])
    logits = _dot(xh, wh) + (_dot(xh, wl) + _dot(xl, wh)) + b_ref[...]
    lane = lax.broadcasted_iota(jnp.int32, logits.shape, 1)
    big = jnp.int32(ROUTE_LANES)
    is_grp = (lane >= N_EXPERTS) & (lane < N_EXPERTS + N_GROUPS)
    gl = jnp.where(is_grp, logits, NEG_INF)
    gmax = jnp.max(gl, axis=1, keepdims=True)
    gsel = jnp.min(jnp.where(gl == gmax, lane, big), axis=1, keepdims=True) - N_EXPERTS
    gp = 1.0 / jnp.sum(jnp.exp(gl - gmax), axis=1, keepdims=True)
    in_grp = (lane < N_EXPERTS) & ((lane // EXPERTS_PER_GROUP) == gsel)
    el = jnp.where(in_grp, logits, NEG_INF)
    v1 = jnp.max(el, axis=1, keepdims=True)
    i1 = jnp.min(jnp.where(el == v1, lane, big), axis=1, keepdims=True)
    el2 = jnp.where(lane == i1, NEG_INF, el)
    v2 = jnp.max(el2, axis=1, keepdims=True)
    i2 = jnp.min(jnp.where(el2 == v2, lane, big), axis=1, keepdims=True)
    e2 = jnp.exp(v2 - v1)
    w1 = (1.0 / (1.0 + e2)) * gp
    w2 = (e2 / (1.0 + e2)) * gp
    @pl.when(pl.program_id(0) == 0)
    def _():
        cnt_ref[...] = jnp.zeros_like(cnt_ref)

    tm = logits.shape[0]
    oh1 = (lane == i1).astype(F32)
    oh2 = (lane == i2).astype(F32)
    tri = (lax.broadcasted_iota(jnp.int32, (tm, tm), 0) > lax.broadcasted_iota(jnp.int32, (tm, tm), 1)).astype(BF16)
    base = cnt_ref[0:1, :]
    c1 = jnp.sum(oh1, axis=0, keepdims=True)
    c2 = jnp.sum(oh2, axis=0, keepdims=True)
    r1 = jnp.sum(oh1 * (_dot(tri, oh1.astype(BF16)) + base), axis=1, keepdims=True)
    r2 = jnp.sum(oh2 * (_dot(tri, oh2.astype(BF16)) + (base + c1)), axis=1, keepdims=True)
    cnt_ref[...] = jnp.broadcast_to(base + c1 + c2, cnt_ref.shape)
    out = jnp.where(lane == 0, i1.astype(F32), 0.0)
    out = jnp.where(lane == 1, i2.astype(F32), out)
    out = jnp.where(lane == 2, w1, out)
    out = jnp.where(lane == 3, w2, out)
    out = jnp.where(lane == 4, r1, out)
    out = jnp.where(lane == 5, r2, out)
    o_ref[...] = out


def _router(x, w_grp, b_grp, w_exp, b_exp, tile):
    m = x.shape[0]
    pad = ROUTE_LANES - N_EXPERTS - N_GROUPS
    w = jnp.concatenate([w_exp, w_grp, jnp.zeros((D_MODEL, pad), F32)], axis=1)
    b = jnp.concatenate([b_exp, b_grp, jnp.zeros((pad,), F32)]).reshape(1, ROUTE_LANES)
    const = lambda s: pl.BlockSpec(s, lambda i: (0, 0))
    return pl.pallas_call(
        _router_body,
        grid=(m // tile,),
        in_specs=[pl.BlockSpec((tile, D_MODEL), lambda i: (i, 0)), const(w.shape), const(b.shape)],
        out_specs=[pl.BlockSpec((tile, ROUTE_LANES), lambda i: (i, 0)), const((V7X_SUBLANES, ROUTE_LANES))],
        out_shape=[jax.ShapeDtypeStruct((m, ROUTE_LANES), F32),
                   jax.ShapeDtypeStruct((V7X_SUBLANES, ROUTE_LANES), F32)],
        compiler_params=_cparams(1),
        name="moe_router",
    )(x, w, b)


DMA_UNROLL = 8


def _dispatch_body(pos_ref, x_ref, xs_hbm, sem):
    tile = x_ref.shape[0]

    def body(j, c):
        src = x_ref.at[pl.ds(j, 1)]
        pltpu.make_async_copy(src, xs_hbm.at[pl.ds(pos_ref[0, 0, j], 1)], sem.at[0]).start()
        pltpu.make_async_copy(src, xs_hbm.at[pl.ds(pos_ref[0, 0, tile + j], 1)], sem.at[0]).start()
        return c
    lax.fori_loop(0, tile, body, 0, unroll=DMA_UNROLL)
    for _ in range(2):
        pltpu.make_async_copy(x_ref, xs_hbm.at[pl.ds(0, tile)], sem.at[0]).wait()


def _dispatch(x, pos, tile):
    m = x.shape[0]
    return pl.pallas_call(
        _dispatch_body,
        grid=(m // tile,),
        in_specs=[pl.BlockSpec((1, 1, 2 * tile), lambda i: (i, 0, 0), memory_space=pltpu.SMEM),
                  pl.BlockSpec((tile, D_MODEL), lambda i: (i, 0))],
        out_specs=pl.BlockSpec(memory_space=pl.ANY),
        out_shape=jax.ShapeDtypeStruct((2 * m, D_MODEL), F32),
        scratch_shapes=[pltpu.SemaphoreType.DMA((1,))],
        compiler_params=_cparams(1),
        name="moe_dispatch",
    )(pos, x)


def _expert_body(it_ref, ie_ref, lo_ref, hi_ref, x_ref, wg_ref, wu_ref, wd_ref, y_ref):
    w = pl.program_id(0)
    tile = it_ref[w]
    lo, hi = lo_ref[w], hi_ref[w]
    first = (w == 0) | (it_ref[jnp.maximum(w - 1, 0)] != tile)

    @pl.when(first)
    def _():
        y_ref[...] = jnp.zeros_like(y_ref)

    @pl.when(lo < hi)
    def _():
        xb = x_ref[...].astype(BF16)
        hg = _dot(xb, wg_ref[...].astype(BF16))
        hu = _dot(xb, wu_ref[...].astype(BF16))
        he = (hg * jax.nn.sigmoid(hg)) * hu
        y = _dot(he.astype(BF16), wd_ref[...].astype(BF16))
        row = tile * EXPERT_ROWS + lax.broadcasted_iota(jnp.int32, (EXPERT_ROWS, 1), 0)
        y_ref[...] += jnp.where((row >= lo) & (row < hi), y, 0.0)


def _experts(xs, items, w_gate, w_up, w_down, layer):
    n_items = items[0].shape[0]
    rows = EXPERT_ROWS
    tspec = pl.BlockSpec((rows, D_MODEL), lambda w, it, ie, lo, hi: (it[w], 0))
    wspec = lambda shp: pl.BlockSpec((None, None) + shp, lambda w, it, ie, lo, hi: (layer, ie[w], 0, 0))
    grid_spec = pltpu.PrefetchScalarGridSpec(
        num_scalar_prefetch=4,
        grid=(n_items,),
        in_specs=[tspec, wspec((D_MODEL, D_EXPERT)), wspec((D_MODEL, D_EXPERT)), wspec((D_EXPERT, D_MODEL))],
        out_specs=tspec)
    return pl.pallas_call(
        _expert_body,
        grid_spec=grid_spec,
        out_shape=jax.ShapeDtypeStruct(xs.shape, F32),
        compiler_params=_cparams(1, VMEM_LIMIT),
        name="moe_experts",
    )(*items, xs, w_gate, w_up, w_down)


def _moe_combine_body(pos_ref, posn_ref, h_ref, r_ref, g_ref, b_ref, ys_hbm, o_ref, ybuf, sem, *, nt):
    t = pl.program_id(0)
    slot = t % 2
    tile = h_ref.shape[0]

    def gather(idx_ref, s):
        def body(j, c):
            pltpu.make_async_copy(ys_hbm.at[pl.ds(idx_ref[0, 0, j], 1)], ybuf.at[s, pl.ds(j, 1)], sem.at[s]).start()
            return c
        lax.fori_loop(0, 2 * tile, body, 0, unroll=DMA_UNROLL)

    @pl.when(t == 0)
    def _():
        gather(pos_ref, 0)

    @pl.when(t + 1 < nt)
    def _():
        gather(posn_ref, 1 - slot)

    pltpu.make_async_copy(ys_hbm.at[pl.ds(0, 2 * tile)], ybuf.at[slot], sem.at[slot]).wait()
    r = r_ref[...]
    moe = r[:, 2:3] * ybuf[slot, 0:tile, :] + r[:, 3:4] * ybuf[slot, tile:2 * tile, :]
    o_ref[...] = _layer_norm_rows(ALPHA * h_ref[...] + moe, g_ref[...], b_ref[...])


def _moe_combine(h, ys, pos, route, g, b, tile, row_off, n_rows):
    ob = row_off // tile
    nt = n_rows // tile
    row = lambda w: pl.BlockSpec((tile, w), lambda i: (i + ob, 0))
    const = lambda s: pl.BlockSpec(s, lambda i: (0, 0))
    smem = lambda f: pl.BlockSpec((1, 1, 2 * tile), f, memory_space=pltpu.SMEM)
    return pl.pallas_call(
        functools.partial(_moe_combine_body, nt=nt),
        grid=(nt,),
        in_specs=[smem(lambda i: (i + ob, 0, 0)), smem(lambda i: (jnp.minimum(i + 1, nt - 1) + ob, 0, 0)),
                  row(D_MODEL), row(ROUTE_LANES), const((1, D_MODEL)), const((1, D_MODEL)),
                  pl.BlockSpec(memory_space=pl.ANY)],
        out_specs=pl.BlockSpec((tile, D_MODEL), lambda i: (i, 0)),
        out_shape=jax.ShapeDtypeStruct((n_rows, D_MODEL), F32),
        scratch_shapes=[pltpu.VMEM((2, 2 * tile, D_MODEL), F32), pltpu.SemaphoreType.DMA((2,))],
        compiler_params=_cparams(1, VMEM_LIMIT),
        name="moe_combine_ln",
    )(pos, pos, h, route, g.reshape(1, D_MODEL), b.reshape(1, D_MODEL), ys)


def _dispatch_plan(route, counts, m, tile):
    rows = EXPERT_ROWS
    nt = (2 * m) // rows
    cnt = counts[0, 0:N_EXPERTS].astype(jnp.int32)
    end = jnp.cumsum(cnt)
    start = end - cnt
    e = route[:, 0:2].astype(jnp.int32)
    rank = route[:, 4:6].astype(jnp.int32)
    onehot = (e[:, :, None] == jnp.arange(N_EXPERTS, dtype=jnp.int32)[None, None, :])
    pos = jnp.sum(jnp.where(onehot, start[None, None, :], 0), axis=2) + rank
    pos = jnp.transpose(pos.reshape(m // tile, tile, 2), (0, 2, 1)).reshape(m // tile, 1, 2 * tile)
    first_t = start // rows
    n_it = jnp.where(cnt > 0, (end - 1) // rows - first_t + 1, 0)
    it_end = jnp.cumsum(n_it)
    n_items = nt + N_EXPERTS
    w = jnp.arange(n_items, dtype=jnp.int32)
    ie = jnp.minimum(jnp.sum((it_end[None, :] <= w[:, None]).astype(jnp.int32), axis=1), N_EXPERTS - 1)
    pick = lambda a: jnp.sum(jnp.where(ie[:, None] == jnp.arange(N_EXPERTS)[None, :], a[None, :], 0), axis=1)
    valid = w < it_end[-1]
    it = pick(first_t) + (w - (pick(it_end) - pick(n_it)))
    it = jnp.where(valid, it, nt - 1).astype(jnp.int32)
    last_e = jnp.sum(jnp.where(w == it_end[-1] - 1, ie, 0))
    ie = jnp.where(valid, ie, last_e).astype(jnp.int32)
    lo = jnp.where(valid, pick(start), 0).astype(jnp.int32)
    hi = jnp.where(valid, pick(end), 0).astype(jnp.int32)
    return pos.astype(jnp.int32), (it, ie, lo, hi)


def _moe_layer(h, w_grp, b_grp, w_exp, b_exp, w_gate, w_up, w_down, layer, g, b, parts):
    m = h.shape[0]
    tile = SMALL_TILE
    assert m % tile == 0 and (2 * m) % EXPERT_ROWS == 0
    route, counts = _router(h, w_grp, b_grp, w_exp, b_exp, tile)
    pos, items = _dispatch_plan(route, counts, m, tile)
    xs = _dispatch(h, pos, tile)
    ys = _experts(xs, items, w_gate, w_up, w_down, layer)
    return [_moe_combine(h, ys, pos, route, g, b, tile, off, n) for off, n in parts]


GATE_LANES = V7X_LANES
GATE_F0 = 2 * V7X_SUBLANES


def _log_sigmoid(x):
    return jnp.minimum(x, 0.0) - jnp.log1p(jnp.exp(-jnp.abs(x)))


def _silu(x):
    return x * jax.nn.sigmoid(x)


def _in_proj_b_body(x_ref, w_ref, cw_ref, bias_ref, p1_ref, p2_ref, p3_ref,
                    q_ref, kt_ref, v_ref, og_ref, gt_ref, gtt_ref, mq_ref, u_ref, carry, *, seq_rows):
    w = MLSTM_WIDTH
    lhs = _lhs(x_ref[...], w_ref.dtype == F32)
    tm = x_ref.shape[0]
    u = _mm(lhs, w_ref[:, 0:2 * w])
    row = lax.broadcasted_iota(jnp.int32, (tm, 1), 0)
    if seq_rows is None:
        @pl.when(pl.program_id(1) == 0)
        def _():
            carry[...] = p1_ref[...]
        uc = jnp.concatenate([carry[...], u], axis=0)
        shifted = [uc[V7X_SUBLANES - k:V7X_SUBLANES - k + tm, :] for k in (1, 2, 3)]
        carry[...] = u[tm - V7X_SUBLANES:, :]
        u_ref[...] = u[tm - (CONV_W - 1):, :]
    else:
        t = row % seq_rows
        prevs = (p1_ref[...], p2_ref[...], p3_ref[...])
        shifted = [jnp.where(t >= k, pltpu.roll(u, k, 0), prevs[k - 1]) for k in (1, 2, 3)]
        u_ref[...] = u
    cw = cw_ref[...]
    y = shifted[2] * cw[0:1, :]
    y = y + shifted[1] * cw[1:2, :]
    y = y + shifted[0] * cw[2:3, :]
    y = y + u * cw[3:4, :]
    qk = _silu(y)
    q_ref[...] = qk[:, 0:w].astype(BF16)
    kt_ref[...] = (qk[:, w:2 * w] * ATTN_SCALE).T.astype(BF16)
    v_ref[...] = _mm(lhs, w_ref[:, 2 * w:3 * w]).astype(BF16)
    og_ref[...] = jax.nn.sigmoid(_mm(lhs, w_ref[:, 3 * w:4 * w])).astype(BF16)
    gates = _mm(lhs, w_ref[:, 4 * w:4 * w + GATE_LANES]) + bias_ref[...]
    lane = lax.broadcasted_iota(jnp.int32, gates.shape, 1)
    is_f = (lane >= GATE_F0) & (lane < GATE_F0 + MLSTM_HEADS)
    gt = jnp.where(is_f, _log_sigmoid(gates), gates)
    gt_ref[...] = gt
    gtt_ref[...] = gt.T
    mq_ref[...] = (_mm(lhs, w_ref[:, 4 * w + GATE_LANES:]) * ATTN_SCALE).astype(mq_ref.dtype)


def _in_proj_b(x, x_off, m, w_bf, conv_w, bias, prevs, n_seq, seq_len, tile, per_tile_seqs):
    w = MLSTM_WIDTH
    tps = max(seq_len // tile, 1)
    ob = x_off // tile
    row = lambda width: pl.BlockSpec((tile, width), lambda b, i: (b * tps + i, 0))
    col = lambda height: pl.BlockSpec((height, tile), lambda b, i: (0, b * tps + i))
    const = lambda s: pl.BlockSpec(s, lambda b, i: (0, 0))
    if per_tile_seqs:
        pspecs = [row(2 * w)] * 3
        u_rows, u_spec = m, row(2 * w)
    else:
        pspecs = [pl.BlockSpec((V7X_SUBLANES, 2 * w), lambda b, i: (b, 0))] * 3
        u_rows = n_seq * (CONV_W - 1)
        u_spec = pl.BlockSpec((None, CONV_W - 1, 2 * w), lambda b, i: (b, 0, 0))
    u_shape = (jax.ShapeDtypeStruct((m, 2 * w), F32) if per_tile_seqs
               else jax.ShapeDtypeStruct((n_seq, CONV_W - 1, 2 * w), F32))
    return pl.pallas_call(
        functools.partial(_in_proj_b_body, seq_rows=seq_len if per_tile_seqs else None),
        grid=(m // (tps * tile), tps),
        in_specs=[pl.BlockSpec((tile, D_MODEL), lambda b, i: (b * tps + i + ob, 0)),
                  const(w_bf.shape), const(conv_w.shape), const(bias.shape)] + pspecs,
        out_specs=[row(w), col(w), row(w), row(w), row(GATE_LANES), col(GATE_LANES), row(MEM_WIDTH), u_spec],
        out_shape=[jax.ShapeDtypeStruct((m, w), BF16), jax.ShapeDtypeStruct((w, m), BF16),
                   jax.ShapeDtypeStruct((m, w), BF16), jax.ShapeDtypeStruct((m, w), BF16),
                   jax.ShapeDtypeStruct((m, GATE_LANES), F32), jax.ShapeDtypeStruct((GATE_LANES, m), F32),
                   jax.ShapeDtypeStruct((m, MEM_WIDTH), BF16), u_shape],
        scratch_shapes=[pltpu.VMEM((V7X_SUBLANES, 2 * w), F32)],
        compiler_params=_cparams(2, VMEM_LIMIT),
        name="in_proj_b",
    )(x, w_bf, conv_w, bias, *prevs)


def _mlstm_body(q_ref, kt_ref, v_ref, og_ref, gt_ref, gtt_ref, ng_ref, c0_ref, n0_ref, m0_ref,
                cell_ref, c_out, n_out, m_out, state, m_state):
    c = pl.program_id(1)
    L = q_ref.shape[0]
    E = HEAD_DIM

    @pl.when(c == 0)
    def _():
        for h in range(MLSTM_HEADS):
            state[h, :, 0:E] = c0_ref[h]
            state[h, :, E:2 * E] = jnp.where(lax.broadcasted_iota(jnp.int32, (E, E), 1) == 0, n0_ref[h], 0.0)
            m_state[h:h + 1, :] = jnp.broadcast_to(m0_ref[0:1, h:h + 1], (1, V7X_LANES))

    gt = gt_ref[...]
    row = lax.broadcasted_iota(jnp.int32, (L, 1), 0)
    b_cols = gt
    gtt = gtt_ref[0:2 * GATE_F0, :]
    lane = lax.broadcasted_iota(jnp.int32, (1, L), 1)
    b_rows = gtt
    sh = 1
    while sh < L:
        b_cols = b_cols + jnp.where(row >= sh, pltpu.roll(b_cols, sh, 0), 0.0)
        b_rows = b_rows + jnp.where(lane >= sh, pltpu.roll(b_rows, sh, 1), 0.0)
        sh *= 2
    r_rows = gtt[0:GATE_F0, :] - b_rows[GATE_F0:2 * GATE_F0, :]
    causal = lax.broadcasted_iota(jnp.int32, (L, L), 0) >= lax.broadcasted_iota(jnp.int32, (L, L), 1)
    ones_aug = jnp.ones((L, E), BF16)
    m_new_all = []
    for h in range(MLSTM_HEADS):
        hs = slice(h * E, (h + 1) * E)
        qh, vh = q_ref[:, hs], v_ref[:, hs]
        kt = kt_ref[hs, :]
        b_col = b_cols[:, GATE_F0 + h:GATE_F0 + h + 1]
        r_row = r_rows[h:h + 1, :]
        m_prev = m_state[h:h + 1, 0:1]
        s_aug = state[h]
        d = jnp.where(causal, b_col + r_row, NEG_INF)
        log_inter = b_col + m_prev
        mt = jnp.maximum(log_inter, jnp.max(d, axis=1, keepdims=True))
        w_intra = jnp.exp(d - mt)
        w_inter = jnp.exp(log_inter - mt)
        qk = _dot(qh, kt) * w_intra
        inter = _dot(qh, s_aug.astype(BF16))
        num = _dot(qk.astype(BF16), vh) + w_inter * inter[:, 0:E]
        den = jnp.sum(qk, axis=1, keepdims=True) + w_inter * inter[:, E:E + 1]
        hv = num / jnp.maximum(jnp.abs(den), jnp.exp(-mt))
        mu = jnp.mean(hv, axis=1, keepdims=True)
        hc = hv - mu
        var = jnp.mean(hc * hc, axis=1, keepdims=True)
        hn = hc * lax.rsqrt(var + LN_EPS) * ng_ref[:, hs]
        cell_ref[:, hs] = (og_ref[:, hs].astype(F32) * hn).astype(cell_ref.dtype)
        b_last = b_col[L - 1:L, :]
        lw = b_last + r_row
        m_new = jnp.maximum(b_last + m_prev, jnp.max(lw, axis=1, keepdims=True))
        wkt = (jnp.exp(lw - m_new) * kt.astype(F32)).astype(BF16)
        v_aug = jnp.concatenate([vh, ones_aug], axis=1)
        state[h] = jnp.exp(b_last + m_prev - m_new) * s_aug + _dot(wkt, v_aug)
        m_state[h:h + 1, :] = jnp.broadcast_to(m_new, (1, V7X_LANES))
        m_new_all.append(m_new)

    for h in range(MLSTM_HEADS):
        c_out[h] = state[h, :, 0:E]
        n_out[h] = state[h, :, E:E + 1]
    m_out[...] = jnp.concatenate(m_new_all, axis=1)


def _mlstm(q, kt, v, og, gt, gtt, norm_g, c0, n0, m0, n_seq, seq_len, chunk):
    w, hh, e = MLSTM_WIDTH, MLSTM_HEADS, HEAD_DIM
    nc = seq_len // chunk
    row = lambda width: pl.BlockSpec((chunk, width), lambda b, c: (b * nc + c, 0))
    col = lambda height: pl.BlockSpec((height, chunk), lambda b, c: (0, b * nc + c))
    st = lambda shp: pl.BlockSpec((None,) + shp, lambda b, c: (b,) + (0,) * len(shp))
    return pl.pallas_call(
        _mlstm_body,
        grid=(n_seq, nc),
        in_specs=[row(w), col(w), row(w), row(w), row(GATE_LANES), col(GATE_LANES),
                  pl.BlockSpec((1, w), lambda b, c: (0, 0)), st((hh, e, e)), st((hh, e, 1)), st((1, hh))],
        out_specs=[row(w), st((hh, e, e)), st((hh, e, 1)), st((1, hh))],
        out_shape=[jax.ShapeDtypeStruct((n_seq * seq_len, w), BF16),
                   jax.ShapeDtypeStruct((n_seq, hh, e, e), F32),
                   jax.ShapeDtypeStruct((n_seq, hh, e, 1), F32),
                   jax.ShapeDtypeStruct((n_seq, 1, hh), F32)],
        scratch_shapes=[pltpu.VMEM((hh, e, 2 * e), F32), pltpu.VMEM((2 * V7X_SUBLANES, V7X_LANES), F32)],
        compiler_params=_cparams(2, VMEM_LIMIT),
        name="mlstm_chunks",
    )(q, kt, v, og, gt, gtt, norm_g.reshape(1, w), c0, n0.reshape(n_seq, hh, e, 1), m0.reshape(n_seq, 1, hh))


def _w_in_b_regroup(w_in, b_if):
    w4 = 4 * MLSTM_WIDTH
    hh = MLSTM_HEADS
    z = lambda n: jnp.zeros((D_MODEL, n), w_in.dtype)
    w = jnp.concatenate([w_in[:, :w4 + hh], z(GATE_F0 - hh), w_in[:, w4 + hh:w4 + 2 * hh],
                         z(GATE_LANES - GATE_F0 - hh), w_in[:, w4 + 2 * hh:]], axis=1)
    zb = lambda n: jnp.zeros((n,), F32)
    bias = jnp.concatenate([b_if[0].astype(F32), zb(GATE_F0 - hh), b_if[1].astype(F32),
                            zb(GATE_LANES - GATE_F0 - hh)]).reshape(1, GATE_LANES)
    return w, bias


def _layer_b_prompt(h_all, w_bf, bias, conv_w, norm_g, w_out, mem_prompt, w_mem, ln_g, ln_b, n_seq, seq_len,
                    tail):
    m = n_seq * seq_len
    tile = min(ROW_TILE, seq_len)
    chunk = min(MLSTM_CHUNK, seq_len)
    zstate = jnp.zeros((n_seq * V7X_SUBLANES, 2 * MLSTM_WIDTH), F32)
    q, kt, v, og, gt, gtt, mq, conv = _in_proj_b(h_all, 0, m, w_bf, conv_w, bias, [zstate] * 3, n_seq, seq_len,
                                                 tile, False)
    hh, e = MLSTM_HEADS, HEAD_DIM
    cell, c_out, n_out, m_out = _mlstm(q, kt, v, og, gt, gtt, norm_g, jnp.zeros((n_seq, hh, e, e), F32),
                                       jnp.zeros((n_seq, hh, e), F32), jnp.zeros((n_seq, hh), F32),
                                       n_seq, seq_len, chunk)
    mem_kv = _matmul(mem_prompt.reshape(n_seq * N_MEM, D_MODEL), w_mem, N_MEM)
    mo = _mem_attn(mq, mem_kv.reshape(n_seq, N_MEM, 2 * MEM_WIDTH), seq_len, tile)
    h1 = _out_proj(cell, mo, h_all, 0, w_out.astype(BF16), ln_g, ln_b, tile, tail)
    return h1, conv, c_out, n_out.reshape(n_seq, hh, e), m_out.reshape(n_seq, hh), mem_kv


def _layer_b_sample(h_all, h_off, w_bf, bias, conv_w, norm_g, w_out, mem_kv, conv_state, c0, n0, m0, ln_g, ln_b,
                    n_b, n_new):
    m = n_b * n_new
    w2 = 2 * MLSTM_WIDTH
    hh, e = MLSTM_HEADS, HEAD_DIM
    chunk = MLSTM_CHUNK
    t = jnp.arange(n_new)
    prevs = []
    for kk in (1, 2, 3):
        idx = jnp.clip(CONV_W - 1 - kk + t, 0, CONV_W - 2)
        prevs.append(conv_state[:, idx, :].reshape(m, w2))
    q, kt, v, og, gt, gtt, mq, u = _in_proj_b(h_all, h_off, m, w_bf, conv_w, bias, prevs, n_b, n_new, m, True)
    conv_new = jnp.concatenate([conv_state, u.reshape(n_b, n_new, w2)], axis=1)[:, n_new:]
    npad = chunk - n_new
    pad3 = lambda a: jnp.pad(a.reshape(n_b, n_new, a.shape[1]), ((0, 0), (0, npad), (0, 0)))
    gate_pad = jnp.where(jnp.arange(GATE_LANES) < hh, NEG_INF, 0.0).astype(F32)
    gtp = jnp.concatenate([gt.reshape(n_b, n_new, GATE_LANES),
                           jnp.broadcast_to(gate_pad, (n_b, npad, GATE_LANES))], axis=1)
    gttp = jnp.concatenate([gtt.reshape(GATE_LANES, n_b, n_new),
                            jnp.broadcast_to(gate_pad[:, None, None], (GATE_LANES, n_b, npad))], axis=2)
    ktp = jnp.pad(kt.reshape(MLSTM_WIDTH, n_b, n_new), ((0, 0), (0, 0), (0, npad)))
    flat = lambda a: a.reshape(n_b * chunk, a.shape[2])
    flat_t = lambda a: a.reshape(a.shape[0], n_b * chunk)
    cell, c_out, n_out, m_out = _mlstm(flat(pad3(q)), flat_t(ktp), flat(pad3(v)), flat(pad3(og)), flat(gtp),
                                       flat_t(gttp), norm_g, c0, n0, m0, n_b, chunk, chunk)
    cell = cell.reshape(n_b, chunk, MLSTM_WIDTH)[:, :n_new].reshape(m, MLSTM_WIDTH)
    mo = _sample_mem_attn(mq.astype(F32).reshape(n_b, n_new, MEM_WIDTH),
                          mem_kv.reshape(n_b, N_MEM, 2 * MEM_WIDTH), n_b, n_new)
    h1 = _out_proj(cell, mo.reshape(m, MEM_WIDTH), h_all, h_off, w_out, ln_g, ln_b, m)
    return h1, conv_new, c_out, n_out.reshape(n_b, hh, e), m_out.reshape(n_b, hh)


def _layer_a_sample(h2d, w_in, w_out, mem_kv, caches, ln_g, ln_b, n_b, n_new):
    m = n_b * n_new
    pos = PAST_LEN + (jnp.arange(m, dtype=jnp.int32) % n_new)
    outs = _in_proj_a(h2d, w_in, _rope_tables(pos), n_new, n_b, m, full_kv=True)
    qs, ks, vs, mq, kvs = outs[0:3], outs[3:6], outs[6:9], outs[9], outs[10:13]
    f3 = lambda t: t.reshape(n_b, n_new, t.shape[1])
    caches = [c.reshape(n_b, c.shape[1], 2 * DIL_WIDTH) for c in caches]
    mix = _sample_attn([f3(t) for t in qs], [f3(t) for t in ks], [f3(t) for t in vs], caches, n_b, n_new)
    mo = _sample_mem_attn(f3(mq), mem_kv.reshape(n_b, N_MEM, 2 * MEM_WIDTH), n_b, n_new)
    h1 = _out_proj(mix.reshape(m, DIL_WIDTH), mo.reshape(m, MEM_WIDTH), h2d, 0, w_out, ln_g, ln_b, m)
    return h1, kvs


def _layer_a_prompt(h2d, mem_prompt, w_in, w_out, w_mem, ln_g, ln_b, n_seq, seq_len, tail):
    tile = min(ROW_TILE, seq_len)
    tabs = _rope_tables(jnp.arange(seq_len, dtype=jnp.int32))
    outs = _in_proj_a(h2d, w_in.astype(BF16), tabs, seq_len, n_seq, tile, full_kv=False)
    qs, ks, vs, mq, kvs = outs[0:3], outs[3:6], outs[6:9], outs[9], outs[10:13]
    mem_kv = _matmul(mem_prompt.reshape(n_seq * N_MEM, D_MODEL), w_mem, N_MEM)
    os_, lses = [], []
    for g, (win, dil) in enumerate(DIL_PAIRS):
        o, lse = _dil_attn(qs[g], ks[g], vs[g], n_seq, seq_len, dil, win // dil)
        os_.append(o)
        lses.append(lse)
    mix = _group_mix(os_, lses, tile)
    mo = _mem_attn(mq, mem_kv.reshape(n_seq, N_MEM, 2 * MEM_WIDTH), seq_len, tile)
    h1 = _out_proj(mix, mo, h2d, 0, w_out.astype(BF16), ln_g, ln_b, tile, tail)
    return h1, kvs, mem_kv


def kernel(x_prompt, x_sample, mem_prompt, cache_win1_kv, cache_win2_kv, cache_win3_kv, cache_mem_kv,
           state_mlstm_conv, state_mlstm_C, state_mlstm_n, state_mlstm_m,
           w_in_A, w_out_A, w_in_B, mlstm_conv_w, mlstm_b_if, mlstm_norm_g, w_out_B,
           w_mem_kv, ln_g, ln_b, w_grp, b_grp, w_exp, b_exp, w_gate, w_up, w_down):
    n_p, seq, d = x_prompt.shape
    n_s, t_new, _ = x_sample.shape
    assert d == D_MODEL and w_in_A.shape[0] == 1 and w_in_B.shape[0] == 1
    mp, ms = n_p * seq, n_s * t_new
    m_all = mp + ms
    assert mp % ROW_TILE == 0 and ms % SMALL_TILE == 0 and mp % ms == 0
    xp = x_prompt.reshape(mp, d)
    xs = x_sample.reshape(ms, d)

    def moe(h, i, parts):
        return _moe_layer(h, w_grp[i], b_grp[i], w_exp[i], b_exp[i], w_gate, w_up, w_down, i,
                          ln_g[i, 1], ln_b[i, 1], parts)

    caches = (cache_win1_kv[0], cache_win2_kv[0], cache_win3_kv[0])
    hs, kv_s = _layer_a_sample(xs, w_in_A[0], w_out_A[0], cache_mem_kv[0], caches, ln_g[0, 0], ln_b[0, 0],
                               n_s, t_new)
    h, kv_p, mem_kv0 = _layer_a_prompt(xp, mem_prompt, w_in_A[0], w_out_A[0], w_mem_kv[0], ln_g[0, 0], ln_b[0, 0],
                                       n_p, seq, hs)
    (h,) = moe(h, 0, [(0, m_all)])

    w_b, bias = _w_in_b_regroup(w_in_B[0], mlstm_b_if[0])
    w_bf = w_b.astype(BF16)
    hs, conv_s, c_s, nv_s, mm_s = _layer_b_sample(
        h, mp, w_b, bias, mlstm_conv_w[0], mlstm_norm_g[0], w_out_B[0], cache_mem_kv[1],
        state_mlstm_conv[0], state_mlstm_C[0], state_mlstm_n[0], state_mlstm_m[0],
        ln_g[1, 0], ln_b[1, 0], n_s, t_new)
    h1, conv_p, c_p, nv_p, mm_p, mem_kv1 = _layer_b_prompt(
        h, w_bf, bias, mlstm_conv_w[0], mlstm_norm_g[0], w_out_B[0], mem_prompt, w_mem_kv[1],
        ln_g[1, 0], ln_b[1, 0], n_p, seq, hs)
    yp, ys = moe(h1, 1, [(0, mp), (mp, ms)])

    kv_shape = (2, HEADS_PER_GROUP, HEAD_DIM)
    wins = []
    for g in range(N_DIL_GROUPS):
        wins.append(kv_p[g].reshape((1, n_p, -1) + kv_shape))
        wins.append(kv_s[g].reshape((1, n_s, t_new) + kv_shape))
    mem_kv_p = jnp.stack([mem_kv0, mem_kv1]).reshape((2, n_p, N_MEM, 2, MEM_HEADS, HEAD_DIM))
    return (yp.reshape(n_p, seq, d), ys.reshape(n_s, t_new, d), *wins,
            conv_p[None], conv_s[None], c_p[None], c_s[None], nv_p[None], nv_s[None], mm_p[None], mm_s[None],
            mem_kv_p)
```

```python
import functools

import jax
import jax.numpy as jnp
import numpy as np
from jax import lax
from jax.experimental import pallas as pl
from jax.experimental.pallas import tpu as pltpu

D_MODEL = 1024
HEAD_DIM = 64
ATTN_SCALE = HEAD_DIM ** -0.5
PAST_LEN = 16384
N_MEM = 256
MEM_HEADS = 4
MEM_WIDTH = MEM_HEADS * HEAD_DIM
DIL_PAIRS = ((128, 1), (512, 4), (2048, 16))
N_DIL_GROUPS = len(DIL_PAIRS)
HEADS_PER_GROUP = 4
DIL_WIDTH = HEADS_PER_GROUP * HEAD_DIM
QBLK = 128
ROPE_THETA = 500000.0
ROPE_DIMS = HEAD_DIM // 4
MLSTM_HEADS = 12
MLSTM_WIDTH = MLSTM_HEADS * HEAD_DIM
CONV_W = 4
N_GROUPS = 4
EXPERTS_PER_GROUP = 8
N_EXPERTS = N_GROUPS * EXPERTS_PER_GROUP
D_EXPERT = 256
DEPTH = 2
ALPHA = (2 * DEPTH) ** 0.25
LN_EPS = 1e-5

V7X_LANES = 128
V7X_SUBLANES = 8
V7X_VMEM_BYTES = 64 * 1024 * 1024
VMEM_LIMIT = 48 * 1024 * 1024

ROW_TILE = 512
SMALL_TILE = 128
MLSTM_CHUNK = 128
EXPERT_ROWS = 256

BF16 = jnp.bfloat16
F32 = jnp.float32
NEG_INF = float("-inf")


def _cparams(n_axes, vmem=None):
    return pltpu.CompilerParams(dimension_semantics=("arbitrary",) * n_axes,
                                vmem_limit_bytes=vmem)


def _nt_dot(a, b):
    return lax.dot_general(a, b, (((1,), (1,)), ((), ())), preferred_element_type=F32)


def _dot(a, b):
    return jnp.dot(a, b, preferred_element_type=F32)


def _split_bf16(x):
    hi = x.astype(BF16)
    lo = (x - hi.astype(F32)).astype(BF16)
    return hi, lo


def _lhs(x, precise):
    return _split_bf16(x) if precise else (x.astype(BF16),)


def _mm(lhs, w):
    if len(lhs) == 1:
        return _dot(lhs[0], w)
    wh, wl = _split_bf16(w)
    return _dot(lhs[0], wh) + (_dot(lhs[0], wl) + _dot(lhs[1], wh))


def _rope_tables(pos):
    half = ROPE_DIMS // 2
    inv = jnp.power(ROPE_THETA, -jnp.arange(half, dtype=F32) * (2.0 / ROPE_DIMS))
    ang = pos.astype(F32)[:, None] * inv[None, :]
    cos, sin = jnp.cos(ang), jnp.sin(ang)
    n = pos.shape[0]
    one = jnp.ones((n, HEAD_DIM - ROPE_DIMS), F32)
    zero8 = jnp.zeros((n, half), F32)
    zrest = jnp.zeros((n, HEAD_DIM - ROPE_DIMS), F32)
    a = jnp.concatenate([cos, cos, one], axis=1)
    b = jnp.concatenate([zero8, sin, zrest], axis=1)
    c = jnp.concatenate([-sin, zero8, zrest], axis=1)
    rep = V7X_LANES // HEAD_DIM
    return jnp.tile(a, (1, rep)), jnp.tile(b, (1, rep)), jnp.tile(c, (1, rep))


def _rope_apply(x, ra, rb, rc):
    parts = []
    for s in range(x.shape[1] // V7X_LANES):
        v = x[:, s * V7X_LANES:(s + 1) * V7X_LANES]
        parts.append(v * ra + pltpu.roll(v, ROPE_DIMS // 2, 1) * rb
                     + pltpu.roll(v, V7X_LANES - ROPE_DIMS // 2, 1) * rc)
    return jnp.concatenate(parts, axis=1)


def _in_proj_a_body(x_ref, w_ref, ra_ref, rb_ref, rc_ref, *outs, kv_rows, precise):
    q_refs, k_refs, v_refs = outs[0:3], outs[3:6], outs[6:9]
    mq_ref = outs[9]
    kv_refs = outs[10:13]
    lhs = _lhs(x_ref[...], precise)
    ra, rb, rc = ra_ref[...], rb_ref[...], rc_ref[...]
    tm = x_ref.shape[0]
    gw = DIL_WIDTH
    for g in range(N_DIL_GROUPS):
        q = _rope_apply(_mm(lhs, w_ref[:, g * gw:(g + 1) * gw]), ra, rb, rc)
        k = _rope_apply(_mm(lhs, w_ref[:, (3 + g) * gw:(4 + g) * gw]), ra, rb, rc)
        v = _mm(lhs, w_ref[:, (6 + g) * gw:(7 + g) * gw])
        q_refs[g][...] = (q * ATTN_SCALE).astype(q_refs[g].dtype)
        k_refs[g][...] = k.astype(k_refs[g].dtype)
        v_refs[g][...] = v.astype(v_refs[g].dtype)
        r = kv_rows[g]
        kv_refs[g][:, 0:gw] = k[tm - r:, :]
        kv_refs[g][:, gw:2 * gw] = v[tm - r:, :]
    mq = _mm(lhs, w_ref[:, 9 * gw:9 * gw + MEM_WIDTH])
    mq_ref[...] = (mq * ATTN_SCALE).astype(mq_ref.dtype)


def _in_proj_a(x2d, w, tabs, seq_len, n_seq, tile, full_kv):
    w_bf = w
    act = F32 if full_kv else BF16
    m = x2d.shape[0]
    nt = m // tile
    gw = DIL_WIDTH
    row_spec = lambda w: pl.BlockSpec((tile, w), lambda i: (i, 0))
    if full_kv:
        tab_spec = pl.BlockSpec((tile, V7X_LANES), lambda i: (i, 0))
        kv_rows = (tile,) * 3
        kv_shapes = [jax.ShapeDtypeStruct((m, 2 * gw), F32)] * 3
        kv_specs = [row_spec(2 * gw)] * 3
    else:
        tps = seq_len // tile
        tab_spec = pl.BlockSpec((tile, V7X_LANES), lambda i: (i % tps, 0))
        kv_rows, kv_shapes, kv_specs = [], [], []
        for win, _ in DIL_PAIRS:
            wb = min(win, seq_len)
            r = min(wb, tile)
            nblk = wb // r
            kv_rows.append(r)
            kv_shapes.append(jax.ShapeDtypeStruct((n_seq * wb, 2 * gw), F32))
            kv_specs.append(pl.BlockSpec(
                (r, 2 * gw),
                lambda i, nblk=nblk: ((i // tps) * nblk + jnp.maximum(i % tps - (tps - nblk), 0), 0)))
        kv_rows = tuple(kv_rows)
    out_shapes = [jax.ShapeDtypeStruct((m, gw), act)] * 9 + [jax.ShapeDtypeStruct((m, MEM_WIDTH), act)] + kv_shapes
    out_specs = [row_spec(gw)] * 9 + [row_spec(MEM_WIDTH)] + kv_specs
    return pl.pallas_call(
        functools.partial(_in_proj_a_body, kv_rows=kv_rows, precise=full_kv),
        grid=(nt,),
        in_specs=[row_spec(D_MODEL), pl.BlockSpec(w_bf.shape, lambda i: (0, 0)), tab_spec, tab_spec, tab_spec],
        out_specs=out_specs,
        out_shape=out_shapes,
        compiler_params=_cparams(1, VMEM_LIMIT),
        name="in_proj_a",
    )(x2d, w_bf, *tabs)


def _dil_attn_body(q_ref, k_ref, v_ref, kp_ref, vp_ref, o_ref, lse_ref, *, span):
    i = pl.program_id(2)
    q, k, v = q_ref[...], k_ref[...], v_ref[...]
    kp, vp = kp_ref[...], vp_ref[...]
    qi = lax.broadcasted_iota(jnp.int32, (QBLK, 2 * QBLK), 0) + QBLK
    ki = lax.broadcasted_iota(jnp.int32, (QBLK, 2 * QBLK), 1)
    band = (qi >= ki) & (qi - ki <= span) & ((i > 0) | (ki >= QBLK))
    lses = []
    for h in range(HEADS_PER_GROUP):
        hs = slice(h * HEAD_DIM, (h + 1) * HEAD_DIM)
        kc = jnp.concatenate([kp[:, hs], k[:, hs]], axis=0)
        vc = jnp.concatenate([vp[:, hs], v[:, hs]], axis=0)
        s = jnp.where(band, _nt_dot(q[:, hs], kc), NEG_INF)
        m = jnp.max(s, axis=1, keepdims=True)
        p = jnp.exp(s - m)
        den = jnp.sum(p, axis=1, keepdims=True)
        o_ref[:, hs] = _dot((p / den).astype(BF16), vc).astype(o_ref.dtype)
        lses.append(m + jnp.log(den))
    lse_ref[...] = jnp.concatenate(lses, axis=1)


def _dil_attn(q, k, v, n_seq, seq_len, dil, span):
    gw = DIL_WIDTH
    L = seq_len // dil
    nb = L // QBLK
    view = lambda t: t.reshape(n_seq, L, dil * gw)
    blk = pl.BlockSpec((None, QBLK, gw), lambda b, r, i: (b, i, r))
    prev = pl.BlockSpec((None, QBLK, gw), lambda b, r, i: (b, jnp.maximum(i - 1, 0), r))
    o, lse = pl.pallas_call(
        functools.partial(_dil_attn_body, span=span),
        grid=(n_seq, dil, nb),
        in_specs=[blk, blk, blk, prev, prev],
        out_specs=[blk, pl.BlockSpec((None, None, QBLK, HEADS_PER_GROUP), lambda b, r, i: (b, r, i, 0))],
        out_shape=[jax.ShapeDtypeStruct((n_seq, L, dil * gw), BF16),
                   jax.ShapeDtypeStruct((n_seq, dil, L, HEADS_PER_GROUP), F32)],
        compiler_params=_cparams(3),
        name=f"dil_attn_d{dil}",
    )(view(q), view(k), view(v), view(k), view(v))
    o = o.reshape(n_seq * seq_len, gw)
    lse = jnp.transpose(lse, (0, 2, 1, 3)).reshape(n_seq * seq_len, HEADS_PER_GROUP)
    return o, lse


def _group_mix_body(o1, o2, o3, l1, l2, l3, mix_ref):
    os_ = (o1, o2, o3)
    ls = (l1[...], l2[...], l3[...])
    for h in range(HEADS_PER_GROUP):
        hs = slice(h * HEAD_DIM, (h + 1) * HEAD_DIM)
        lh = [l[:, h:h + 1] for l in ls]
        mx = jnp.maximum(jnp.maximum(lh[0], lh[1]), lh[2])
        e = [jnp.exp(x - mx) for x in lh]
        tot = e[0] + e[1] + e[2]
        acc = (e[0] / tot) * os_[0][:, hs].astype(F32)
        acc = acc + (e[1] / tot) * os_[1][:, hs].astype(F32)
        acc = acc + (e[2] / tot) * os_[2][:, hs].astype(F32)
        mix_ref[:, hs] = acc.astype(mix_ref.dtype)


def _group_mix(os_, lses, tile):
    m = os_[0].shape[0]
    gw = DIL_WIDTH
    ospec = pl.BlockSpec((tile, gw), lambda i: (i, 0))
    lspec = pl.BlockSpec((tile, HEADS_PER_GROUP), lambda i: (i, 0))
    return pl.pallas_call(
        _group_mix_body,
        grid=(m // tile,),
        in_specs=[ospec] * 3 + [lspec] * 3,
        out_specs=ospec,
        out_shape=jax.ShapeDtypeStruct((m, gw), BF16),
        compiler_params=_cparams(1),
        name="group_mix",
    )(*os_, *lses)


def _mem_attn_body(q_ref, kv_ref, o_ref):
    q = q_ref[...]
    kv = kv_ref[...].astype(BF16)
    for h in range(MEM_HEADS):
        hs = slice(h * HEAD_DIM, (h + 1) * HEAD_DIM)
        vs = slice(MEM_WIDTH + h * HEAD_DIM, MEM_WIDTH + (h + 1) * HEAD_DIM)
        s = _nt_dot(q[:, hs], kv[:, hs])
        m = jnp.max(s, axis=1, keepdims=True)
        p = jnp.exp(s - m)
        den = jnp.sum(p, axis=1, keepdims=True)
        o_ref[:, hs] = _dot((p / den).astype(BF16), kv[:, vs]).astype(o_ref.dtype)


def _mem_attn(q, mem_kv, seq_len, tile):
    m = q.shape[0]
    tps = seq_len // tile
    return pl.pallas_call(
        _mem_attn_body,
        grid=(m // tile,),
        in_specs=[pl.BlockSpec((tile, MEM_WIDTH), lambda i: (i, 0)),
                  pl.BlockSpec((None, N_MEM, 2 * MEM_WIDTH), lambda i: (i // tps, 0, 0))],
        out_specs=pl.BlockSpec((tile, MEM_WIDTH), lambda i: (i, 0)),
        out_shape=jax.ShapeDtypeStruct((m, MEM_WIDTH), BF16),
        compiler_params=_cparams(1),
        name="mem_attn",
    )(q, mem_kv)


def _layer_norm_rows(x, g, b):
    mu = jnp.mean(x, axis=1, keepdims=True)
    xc = x - mu
    var = jnp.mean(xc * xc, axis=1, keepdims=True)
    return xc * lax.rsqrt(var + LN_EPS) * g + b


def _out_proj_body(a_ref, mo_ref, h_ref, w_ref, g_ref, b_ref, *rest, n_tiles, precise):
    o_ref = rest[-1]
    ka = a_ref.shape[1]

    def rows():
        y = _mm(_lhs(a_ref[...], precise), w_ref[0:ka, :]) + _mm(_lhs(mo_ref[...], precise), w_ref[ka:, :])
        o_ref[...] = _layer_norm_rows(ALPHA * h_ref[...] + y, g_ref[...], b_ref[...])

    if len(rest) == 1:
        rows()
    else:
        tail_ref = rest[0]
        pl.when(pl.program_id(0) < n_tiles)(rows)

        @pl.when(pl.program_id(0) == n_tiles)
        def _():
            o_ref[0:tail_ref.shape[0], :] = tail_ref[...]


def _out_proj(a, mo, h, h_off, w, g, b, tile, tail=None):
    w_bf = w
    m, ka = a.shape
    nt = m // tile
    last = nt - 1
    row = lambda w, off=0: pl.BlockSpec((tile, w), lambda i: (jnp.minimum(i, last) + off // tile, 0))
    const = lambda s: pl.BlockSpec(s, lambda i: (0, 0))
    in_specs = [row(ka), row(MEM_WIDTH), row(D_MODEL, h_off), const(w_bf.shape), const((1, D_MODEL)),
                const((1, D_MODEL))]
    args = [a, mo, h, w_bf, g.reshape(1, D_MODEL), b.reshape(1, D_MODEL)]
    out_rows, steps = m, nt
    if tail is not None:
        assert tail.shape[0] <= tile
        in_specs.append(const(tail.shape))
        args.append(tail)
        out_rows, steps = m + tail.shape[0], nt + 1
    return pl.pallas_call(
        functools.partial(_out_proj_body, n_tiles=nt, precise=(w.dtype == F32)),
        grid=(steps,),
        in_specs=in_specs,
        out_specs=pl.BlockSpec((tile, D_MODEL), lambda i: (i, 0)),
        out_shape=jax.ShapeDtypeStruct((out_rows, D_MODEL), F32),
        compiler_params=_cparams(1, VMEM_LIMIT),
        name="out_proj_ln",
    )(*args)


def _matmul_body(x_ref, w_ref, o_ref):
    o_ref[...] = _dot(x_ref[...].astype(BF16), w_ref[...].astype(BF16))


def _matmul(x, w, tile):
    m, k = x.shape
    n = w.shape[1]
    return pl.pallas_call(
        _matmul_body,
        grid=(m // tile,),
        in_specs=[pl.BlockSpec((tile, k), lambda i: (i, 0)), pl.BlockSpec((k, n), lambda i: (0, 0))],
        out_specs=pl.BlockSpec((tile, n), lambda i: (i, 0)),
        out_shape=jax.ShapeDtypeStruct((m, n), F32),
        compiler_params=_cparams(1),
        name="mem_kv_proj",
    )(x, w)


def _col_attend(q_row, kmat, vmat, kmask, knew, vnew, nmask):
    pk = kmat * q_row
    pn = None if knew is None else knew * q_row
    outs, lses = [], []
    for h in range(HEADS_PER_GROUP):
        hs = slice(h * HEAD_DIM, (h + 1) * HEAD_DIM)
        s = jnp.sum(pk[:, hs], axis=1, keepdims=True)
        if kmask is not None:
            s = jnp.where(kmask, s, NEG_INF)
        m = jnp.max(s, axis=0, keepdims=True)
        if pn is not None:
            sn = jnp.where(nmask, jnp.sum(pn[:, hs], axis=1, keepdims=True), NEG_INF)
            m = jnp.maximum(m, jnp.max(sn, axis=0, keepdims=True))
        p = jnp.exp(s - m)
        den = jnp.sum(p, axis=0, keepdims=True)
        acc = jnp.sum(p * vmat[:, hs], axis=0, keepdims=True)
        if pn is not None:
            pnw = jnp.exp(sn - m)
            den = den + jnp.sum(pnw, axis=0, keepdims=True)
            acc = acc + jnp.sum(pnw * vnew[:, hs], axis=0, keepdims=True)
        outs.append(acc / den)
        lses.append(m + jnp.log(den))
    return outs, lses


def _sample_attn_body(q1, q2, q3, kn1, kn2, kn3, vn1, vn2, vn3, c1, c2, c3, mix_ref, *, n_new):
    qs = (q1[...], q2[...], q3[...])
    kns = (kn1[...], kn2[...], kn3[...])
    vns = (vn1[...], vn2[...], vn3[...])
    gw = DIL_WIDTH
    t_idx = lax.broadcasted_iota(jnp.int32, (n_new, 1), 0)
    rows = []
    for t in range(n_new):
        per_group = []
        for g, (win, dil) in enumerate(DIL_PAIRS):
            q_row = qs[g][t:t + 1, :]
            cache = (c1, c2, c3)[g]
            if dil == 1:
                kmat, vmat = cache[:, 0:gw], cache[:, gw:2 * gw]
                r_idx = lax.broadcasted_iota(jnp.int32, (kmat.shape[0], 1), 0)
                per_group.append(_col_attend(q_row, kmat, vmat, r_idx >= t, kns[g], vns[g], t_idx <= t))
            else:
                base = t * 2 * gw
                kmat, vmat = cache[:, base:base + gw], cache[:, base + gw:base + 2 * gw]
                per_group.append(_col_attend(q_row, kmat, vmat, None, kns[g], vns[g], t_idx == t))
        heads = []
        for h in range(HEADS_PER_GROUP):
            lh = [per_group[g][1][h] for g in range(N_DIL_GROUPS)]
            mx = jnp.maximum(jnp.maximum(lh[0], lh[1]), lh[2])
            e = [jnp.exp(x - mx) for x in lh]
            tot = e[0] + e[1] + e[2]
            acc = (e[0] / tot) * per_group[0][0][h]
            acc = acc + (e[1] / tot) * per_group[1][0][h]
            acc = acc + (e[2] / tot) * per_group[2][0][h]
            heads.append(acc)
        rows.append(jnp.concatenate(heads, axis=1))
    mix_ref[...] = jnp.concatenate(rows, axis=0)


def _sample_attn(qs, kns, vns, caches, n_b, n_new):
    gw = DIL_WIDTH
    small = pl.BlockSpec((None, n_new, gw), lambda b: (b, 0, 0))
    cviews, cspecs = [], []
    for c, (win, dil) in zip(caches, DIL_PAIRS):
        blocks = win // dil
        cviews.append(c.reshape(n_b, blocks, dil * 2 * gw))
        lanes = min(dil, n_new) * 2 * gw
        cspecs.append(pl.BlockSpec((None, blocks, lanes), lambda b: (b, 0, 0)))
    return pl.pallas_call(
        functools.partial(_sample_attn_body, n_new=n_new),
        grid=(n_b,),
        in_specs=[small] * 9 + cspecs,
        out_specs=small,
        out_shape=jax.ShapeDtypeStruct((n_b, n_new, gw), F32),
        compiler_params=_cparams(1, VMEM_LIMIT),
        name="sample_dil_attn",
    )(*qs, *kns, *vns, *cviews)


def _sample_mem_attn_body(q_ref, kv_ref, o_ref, *, n_new):
    q = q_ref[...]
    kmat, vmat = kv_ref[:, 0:MEM_WIDTH], kv_ref[:, MEM_WIDTH:2 * MEM_WIDTH]
    rows = []
    for t in range(n_new):
        outs, _ = _col_attend(q[t:t + 1, :], kmat, vmat, None, None, None, None)
        rows.append(jnp.concatenate(outs, axis=1))
    o_ref[...] = jnp.concatenate(rows, axis=0)


def _sample_mem_attn(q, mem_kv, n_b, n_new):
    small = pl.BlockSpec((None, n_new, MEM_WIDTH), lambda b: (b, 0, 0))
    return pl.pallas_call(
        functools.partial(_sample_mem_attn_body, n_new=n_new),
        grid=(n_b,),
        in_specs=[small, pl.BlockSpec((None, N_MEM, 2 * MEM_WIDTH), lambda b: (b, 0, 0))],
        out_specs=small,
        out_shape=jax.ShapeDtypeStruct((n_b, n_new, MEM_WIDTH), F32),
        compiler_params=_cparams(1),
        name="sample_mem_attn",
    )(q, mem_kv)


ROUTE_LANES = V7X_LANES


def _router_body(x_ref, w_ref, b_ref, o_ref, cnt_ref):
    xh, xl = _split_bf16(x_ref[...])
    wh, wl = _split_bf16(w_ref[...])
    logits = _dot(xh, wh) + (_dot(xh, wl) + _dot(xl, wh)) + b_ref[...]
    lane = lax.broadcasted_iota(jnp.int32, logits.shape, 1)
    big = jnp.int32(ROUTE_LANES)
    is_grp = (lane >= N_EXPERTS) & (lane < N_EXPERTS + N_GROUPS)
    gl = jnp.where(is_grp, logits, NEG_INF)
    gmax = jnp.max(gl, axis=1, keepdims=True)
    gsel = jnp.min(jnp.where(gl == gmax, lane, big), axis=1, keepdims=True) - N_EXPERTS
    gp = 1.0 / jnp.sum(jnp.exp(gl - gmax), axis=1, keepdims=True)
    in_grp = (lane < N_EXPERTS) & ((lane // EXPERTS_PER_GROUP) == gsel)
    el = jnp.where(in_grp, logits, NEG_INF)
    v1 = jnp.max(el, axis=1, keepdims=True)
    i1 = jnp.min(jnp.where(el == v1, lane, big), axis=1, keepdims=True)
    el2 = jnp.where(lane == i1, NEG_INF, el)
    v2 = jnp.max(el2, axis=1, keepdims=True)
    i2 = jnp.min(jnp.where(el2 == v2, lane, big), axis=1, keepdims=True)
    e2 = jnp.exp(v2 - v1)
    w1 = (1.0 / (1.0 + e2)) * gp
    w2 = (e2 / (1.0 + e2)) * gp
    @pl.when(pl.program_id(0) == 0)
    def _():
        cnt_ref[...] = jnp.zeros_like(cnt_ref)

    tm = logits.shape[0]
    oh1 = (lane == i1).astype(F32)
    oh2 = (lane == i2).astype(F32)
    tri = (lax.broadcasted_iota(jnp.int32, (tm, tm), 0) > lax.broadcasted_iota(jnp.int32, (tm, tm), 1)).astype(BF16)
    base = cnt_ref[0:1, :]
    c1 = jnp.sum(oh1, axis=0, keepdims=True)
    c2 = jnp.sum(oh2, axis=0, keepdims=True)
    r1 = jnp.sum(oh1 * (_dot(tri, oh1.astype(BF16)) + base), axis=1, keepdims=True)
    r2 = jnp.sum(oh2 * (_dot(tri, oh2.astype(BF16)) + (base + c1)), axis=1, keepdims=True)
    cnt_ref[...] = jnp.broadcast_to(base + c1 + c2, cnt_ref.shape)
    out = jnp.where(lane == 0, i1.astype(F32), 0.0)
    out = jnp.where(lane == 1, i2.astype(F32), out)
    out = jnp.where(lane == 2, w1, out)
    out = jnp.where(lane == 3, w2, out)
    out = jnp.where(lane == 4, r1, out)
    out = jnp.where(lane == 5, r2, out)
    o_ref[...] = out


def _router(x, w_grp, b_grp, w_exp, b_exp, tile):
    m = x.shape[0]
    pad = ROUTE_LANES - N_EXPERTS - N_GROUPS
    w = jnp.concatenate([w_exp, w_grp, jnp.zeros((D_MODEL, pad), F32)], axis=1)
    b = jnp.concatenate([b_exp, b_grp, jnp.zeros((pad,), F32)]).reshape(1, ROUTE_LANES)
    const = lambda s: pl.BlockSpec(s, lambda i: (0, 0))
    return pl.pallas_call(
        _router_body,
        grid=(m // tile,),
        in_specs=[pl.BlockSpec((tile, D_MODEL), lambda i: (i, 0)), const(w.shape), const(b.shape)],
        out_specs=[pl.BlockSpec((tile, ROUTE_LANES), lambda i: (i, 0)), const((V7X_SUBLANES, ROUTE_LANES))],
        out_shape=[jax.ShapeDtypeStruct((m, ROUTE_LANES), F32),
                   jax.ShapeDtypeStruct((V7X_SUBLANES, ROUTE_LANES), F32)],
        compiler_params=_cparams(1),
        name="moe_router",
    )(x, w, b)


DMA_UNROLL = 8


def _dispatch_body(pos_ref, x_ref, xs_hbm, sem):
    tile = x_ref.shape[0]

    def body(j, c):
        src = x_ref.at[pl.ds(j, 1)]
        pltpu.make_async_copy(src, xs_hbm.at[pl.ds(pos_ref[0, 0, j], 1)], sem.at[0]).start(priority=0)
        pltpu.make_async_copy(src, xs_hbm.at[pl.ds(pos_ref[0, 0, tile + j], 1)], sem.at[0]).start(priority=1)
        return c
    lax.fori_loop(0, tile, body, 0, unroll=DMA_UNROLL)
    for _ in range(2):
        pltpu.make_async_copy(x_ref, xs_hbm.at[pl.ds(0, tile)], sem.at[0]).wait()


def _dispatch(x, pos, tile):
    m = x.shape[0]
    return pl.pallas_call(
        _dispatch_body,
        grid=(m // tile,),
        in_specs=[pl.BlockSpec((1, 1, 2 * tile), lambda i: (i, 0, 0), memory_space=pltpu.SMEM),
                  pl.BlockSpec((tile, D_MODEL), lambda i: (i, 0))],
        out_specs=pl.BlockSpec(memory_space=pl.ANY),
        out_shape=jax.ShapeDtypeStruct((2 * m, D_MODEL), F32),
        scratch_shapes=[pltpu.SemaphoreType.DMA((1,))],
        compiler_params=_cparams(1),
        name="moe_dispatch",
    )(pos, x)


def _expert_body(it_ref, ie_ref, lo_ref, hi_ref, x_ref, wg_ref, wu_ref, wd_ref, y_ref):
    w = pl.program_id(0)
    tile = it_ref[w]
    lo, hi = lo_ref[w], hi_ref[w]
    first = (w == 0) | (it_ref[jnp.maximum(w - 1, 0)] != tile)

    @pl.when(first)
    def _():
        y_ref[...] = jnp.zeros_like(y_ref)

    @pl.when(lo < hi)
    def _():
        xb = x_ref[...].astype(BF16)
        hg = _dot(xb, wg_ref[...])
        hu = _dot(xb, wu_ref[...])
        he = (hg * jax.nn.sigmoid(hg)) * hu
        y = _dot(he.astype(BF16), wd_ref[...])
        row = tile * EXPERT_ROWS + lax.broadcasted_iota(jnp.int32, (EXPERT_ROWS, 1), 0)
        y_ref[...] += jnp.where((row >= lo) & (row < hi), y, 0.0)


def _experts(xs, items, w_gate, w_up, w_down, layer):
    n_items = items[0].shape[0]
    rows = EXPERT_ROWS
    tspec = pl.BlockSpec((rows, D_MODEL), lambda w, it, ie, lo, hi: (it[w], 0))
    wspec = lambda shp: pl.BlockSpec((None, None) + shp, lambda w, it, ie, lo, hi: (layer, ie[w], 0, 0))
    grid_spec = pltpu.PrefetchScalarGridSpec(
        num_scalar_prefetch=4,
        grid=(n_items,),
        in_specs=[tspec, wspec((D_MODEL, D_EXPERT)), wspec((D_MODEL, D_EXPERT)), wspec((D_EXPERT, D_MODEL))],
        out_specs=tspec)
    return pl.pallas_call(
        _expert_body,
        grid_spec=grid_spec,
        out_shape=jax.ShapeDtypeStruct(xs.shape, F32),
        compiler_params=_cparams(1, VMEM_LIMIT),
        name="moe_experts",
    )(*items, xs, w_gate, w_up, w_down)


def _moe_combine_body(pos_ref, posn_ref, h_ref, r_ref, g_ref, b_ref, ys_hbm, o_ref, ybuf, sem, *, nt):
    t = pl.program_id(0)
    slot = t % 2
    tile = h_ref.shape[0]

    def gather(idx_ref, s):
        def body(j, c):
            for k in range(2):
                jj = j + k * tile
                pltpu.make_async_copy(ys_hbm.at[pl.ds(idx_ref[0, 0, jj], 1)], ybuf.at[s, pl.ds(jj, 1)],
                                      sem.at[s]).start(priority=k)
            return c
        lax.fori_loop(0, tile, body, 0, unroll=DMA_UNROLL)

    @pl.when(t == 0)
    def _():
        gather(pos_ref, 0)

    @pl.when(t + 1 < nt)
    def _():
        gather(posn_ref, 1 - slot)

    pltpu.make_async_copy(ys_hbm.at[pl.ds(0, 2 * tile)], ybuf.at[slot], sem.at[slot]).wait()
    r = r_ref[...]
    moe = r[:, 2:3] * ybuf[slot, 0:tile, :] + r[:, 3:4] * ybuf[slot, tile:2 * tile, :]
    o_ref[...] = _layer_norm_rows(ALPHA * h_ref[...] + moe, g_ref[...], b_ref[...])


def _moe_combine(h, ys, pos, route, g, b, tile, row_off, n_rows):
    ob = row_off // tile
    nt = n_rows // tile
    row = lambda w: pl.BlockSpec((tile, w), lambda i: (i + ob, 0))
    const = lambda s: pl.BlockSpec(s, lambda i: (0, 0))
    smem = lambda f: pl.BlockSpec((1, 1, 2 * tile), f, memory_space=pltpu.SMEM)
    return pl.pallas_call(
        functools.partial(_moe_combine_body, nt=nt),
        grid=(nt,),
        in_specs=[smem(lambda i: (i + ob, 0, 0)), smem(lambda i: (jnp.minimum(i + 1, nt - 1) + ob, 0, 0)),
                  row(D_MODEL), row(ROUTE_LANES), const((1, D_MODEL)), const((1, D_MODEL)),
                  pl.BlockSpec(memory_space=pl.ANY)],
        out_specs=pl.BlockSpec((tile, D_MODEL), lambda i: (i, 0)),
        out_shape=jax.ShapeDtypeStruct((n_rows, D_MODEL), F32),
        scratch_shapes=[pltpu.VMEM((2, 2 * tile, D_MODEL), F32), pltpu.SemaphoreType.DMA((2,))],
        compiler_params=_cparams(1, VMEM_LIMIT),
        name="moe_combine_ln",
    )(pos, pos, h, route, g.reshape(1, D_MODEL), b.reshape(1, D_MODEL), ys)


def _dispatch_plan(route, counts, m, tile):
    rows = EXPERT_ROWS
    nt = (2 * m) // rows
    cnt = counts[0, 0:N_EXPERTS].astype(jnp.int32)
    end = jnp.cumsum(cnt)
    start = end - cnt
    e = route[:, 0:2].astype(jnp.int32)
    rank = route[:, 4:6].astype(jnp.int32)
    onehot = (e[:, :, None] == jnp.arange(N_EXPERTS, dtype=jnp.int32)[None, None, :])
    pos = jnp.sum(jnp.where(onehot, start[None, None, :], 0), axis=2) + rank
    pos = jnp.transpose(pos.reshape(m // tile, tile, 2), (0, 2, 1)).reshape(m // tile, 1, 2 * tile)
    first_t = start // rows
    n_it = jnp.where(cnt > 0, (end - 1) // rows - first_t + 1, 0)
    it_end = jnp.cumsum(n_it)
    n_items = nt + N_EXPERTS
    w = jnp.arange(n_items, dtype=jnp.int32)
    ie = jnp.minimum(jnp.sum((it_end[None, :] <= w[:, None]).astype(jnp.int32), axis=1), N_EXPERTS - 1)
    pick = lambda a: jnp.sum(jnp.where(ie[:, None] == jnp.arange(N_EXPERTS)[None, :], a[None, :], 0), axis=1)
    valid = w < it_end[-1]
    it = pick(first_t) + (w - (pick(it_end) - pick(n_it)))
    it = jnp.where(valid, it, nt - 1).astype(jnp.int32)
    last_e = jnp.sum(jnp.where(w == it_end[-1] - 1, ie, 0))
    ie = jnp.where(valid, ie, last_e).astype(jnp.int32)
    lo = jnp.where(valid, pick(start), 0).astype(jnp.int32)
    hi = jnp.where(valid, pick(end), 0).astype(jnp.int32)
    return pos.astype(jnp.int32), (it, ie, lo, hi)


def _moe_layer(h, w_grp, b_grp, w_exp, b_exp, w_gate, w_up, w_down, layer, g, b, parts):
    m = h.shape[0]
    tile = SMALL_TILE
    assert m % tile == 0 and (2 * m) % EXPERT_ROWS == 0
    route, counts = _router(h, w_grp, b_grp, w_exp, b_exp, tile)
    pos, items = _dispatch_plan(route, counts, m, tile)
    xs = _dispatch(h, pos, tile)
    ys = _experts(xs, items, w_gate, w_up, w_down, layer)
    return [_moe_combine(h, ys, pos, route, g, b, tile, off, n) for off, n in parts]


GATE_LANES = V7X_LANES
GATE_F0 = 2 * V7X_SUBLANES


def _log_sigmoid(x):
    return jnp.minimum(x, 0.0) - jnp.log1p(jnp.exp(-jnp.abs(x)))


def _silu(x):
    return x * jax.nn.sigmoid(x)


def _in_proj_b_body(x_ref, w_ref, cw_ref, bias_ref, p1_ref, p2_ref, p3_ref,
                    q_ref, kt_ref, v_ref, og_ref, gt_ref, gtt_ref, mq_ref, u_ref, carry, *, seq_rows):
    w = MLSTM_WIDTH
    lhs = _lhs(x_ref[...], w_ref.dtype == F32)
    tm = x_ref.shape[0]
    u = _mm(lhs, w_ref[:, 0:2 * w])
    row = lax.broadcasted_iota(jnp.int32, (tm, 1), 0)
    if seq_rows is None:
        @pl.when(pl.program_id(1) == 0)
        def _():
            carry[...] = p1_ref[...]
        uc = jnp.concatenate([carry[...], u], axis=0)
        shifted = [uc[V7X_SUBLANES - k:V7X_SUBLANES - k + tm, :] for k in (1, 2, 3)]
        carry[...] = u[tm - V7X_SUBLANES:, :]
        u_ref[...] = u[tm - (CONV_W - 1):, :]
    else:
        t = row % seq_rows
        prevs = (p1_ref[...], p2_ref[...], p3_ref[...])
        shifted = [jnp.where(t >= k, pltpu.roll(u, k, 0), prevs[k - 1]) for k in (1, 2, 3)]
        u_ref[...] = u
    cw = cw_ref[...]
    y = shifted[2] * cw[0:1, :]
    y = y + shifted[1] * cw[1:2, :]
    y = y + shifted[0] * cw[2:3, :]
    y = y + u * cw[3:4, :]
    qk = _silu(y)
    q_ref[...] = qk[:, 0:w].astype(BF16)
    kt_ref[...] = (qk[:, w:2 * w] * ATTN_SCALE).T.astype(BF16)
    v_ref[...] = _mm(lhs, w_ref[:, 2 * w:3 * w]).astype(BF16)
    og_ref[...] = jax.nn.sigmoid(_mm(lhs, w_ref[:, 3 * w:4 * w])).astype(BF16)
    gates = _mm(lhs, w_ref[:, 4 * w:4 * w + GATE_LANES]) + bias_ref[...]
    lane = lax.broadcasted_iota(jnp.int32, gates.shape, 1)
    is_f = (lane >= GATE_F0) & (lane < GATE_F0 + MLSTM_HEADS)
    gt = jnp.where(is_f, _log_sigmoid(gates), gates)
    gt_ref[...] = gt
    gtt_ref[...] = gt.T
    mq_ref[...] = (_mm(lhs, w_ref[:, 4 * w + GATE_LANES:]) * ATTN_SCALE).astype(mq_ref.dtype)


def _in_proj_b(x, x_off, m, w_bf, conv_w, bias, prevs, n_seq, seq_len, tile, per_tile_seqs):
    w = MLSTM_WIDTH
    tps = max(seq_len // tile, 1)
    ob = x_off // tile
    row = lambda width: pl.BlockSpec((tile, width), lambda b, i: (b * tps + i, 0))
    col = lambda height: pl.BlockSpec((height, tile), lambda b, i: (0, b * tps + i))
    const = lambda s: pl.BlockSpec(s, lambda b, i: (0, 0))
    if per_tile_seqs:
        pspecs = [row(2 * w)] * 3
        u_rows, u_spec = m, row(2 * w)
    else:
        pspecs = [pl.BlockSpec((V7X_SUBLANES, 2 * w), lambda b, i: (b, 0))] * 3
        u_rows = n_seq * (CONV_W - 1)
        u_spec = pl.BlockSpec((None, CONV_W - 1, 2 * w), lambda b, i: (b, 0, 0))
    u_shape = (jax.ShapeDtypeStruct((m, 2 * w), F32) if per_tile_seqs
               else jax.ShapeDtypeStruct((n_seq, CONV_W - 1, 2 * w), F32))
    return pl.pallas_call(
        functools.partial(_in_proj_b_body, seq_rows=seq_len if per_tile_seqs else None),
        grid=(m // (tps * tile), tps),
        in_specs=[pl.BlockSpec((tile, D_MODEL), lambda b, i: (b * tps + i + ob, 0)),
                  const(w_bf.shape), const(conv_w.shape), const(bias.shape)] + pspecs,
        out_specs=[row(w), col(w), row(w), row(w), row(GATE_LANES), col(GATE_LANES), row(MEM_WIDTH), u_spec],
        out_shape=[jax.ShapeDtypeStruct((m, w), BF16), jax.ShapeDtypeStruct((w, m), BF16),
                   jax.ShapeDtypeStruct((m, w), BF16), jax.ShapeDtypeStruct((m, w), BF16),
                   jax.ShapeDtypeStruct((m, GATE_LANES), F32), jax.ShapeDtypeStruct((GATE_LANES, m), F32),
                   jax.ShapeDtypeStruct((m, MEM_WIDTH), BF16), u_shape],
        scratch_shapes=[pltpu.VMEM((V7X_SUBLANES, 2 * w), F32)],
        compiler_params=_cparams(2, VMEM_LIMIT),
        name="in_proj_b",
    )(x, w_bf, conv_w, bias, *prevs)


def _mlstm_body(q_ref, kt_ref, v_ref, og_ref, gt_ref, gtt_ref, ng_ref, s0_ref, m0_ref, cell_ref, state, m_state):
    c = pl.program_id(1)
    L = q_ref.shape[0]
    E = HEAD_DIM

    @pl.when(c == 0)
    def _():
        state[...] = s0_ref[...]
        m_state[...] = m0_ref[...]

    m_all = m_state[...]

    gt = gt_ref[...]
    row = lax.broadcasted_iota(jnp.int32, (L, 1), 0)
    b_cols = gt
    gtt = gtt_ref[0:2 * GATE_F0, :]
    lane = lax.broadcasted_iota(jnp.int32, (1, L), 1)
    b_rows = gtt
    sh = 1
    while sh < L:
        b_cols = b_cols + jnp.where(row >= sh, pltpu.roll(b_cols, sh, 0), 0.0)
        b_rows = b_rows + jnp.where(lane >= sh, pltpu.roll(b_rows, sh, 1), 0.0)
        sh *= 2
    r_rows = gtt[0:GATE_F0, :] - b_rows[GATE_F0:2 * GATE_F0, :]
    causal = lax.broadcasted_iota(jnp.int32, (L, L), 0) >= lax.broadcasted_iota(jnp.int32, (L, L), 1)
    ones_aug = jnp.ones((L, E), BF16)
    m_new_all = []
    for h in range(MLSTM_HEADS):
        hs = slice(h * E, (h + 1) * E)
        qh, vh = q_ref[:, hs], v_ref[:, hs]
        kt = kt_ref[hs, :]
        b_col = b_cols[:, GATE_F0 + h:GATE_F0 + h + 1]
        r_row = r_rows[h:h + 1, :]
        m_prev = m_all[h:h + 1, 0:1]
        s_aug = state[h]
        d = jnp.where(causal, b_col + r_row, NEG_INF)
        log_inter = b_col + m_prev
        mt = jnp.maximum(log_inter, jnp.max(d, axis=1, keepdims=True))
        w_intra = jnp.exp(d - mt)
        w_inter = jnp.exp(log_inter - mt)
        qk = _dot(qh, kt) * w_intra
        inter = _dot(qh, s_aug.astype(BF16))
        num = _dot(qk.astype(BF16), vh) + w_inter * inter[:, 0:E]
        den = jnp.sum(qk, axis=1, keepdims=True) + w_inter * inter[:, E:E + 1]
        hv = num / jnp.maximum(jnp.abs(den), jnp.exp(-mt))
        mu = jnp.mean(hv, axis=1, keepdims=True)
        hc = hv - mu
        var = jnp.mean(hc * hc, axis=1, keepdims=True)
        hn = hc * lax.rsqrt(var + LN_EPS) * ng_ref[:, hs]
        cell_ref[:, hs] = (og_ref[:, hs].astype(F32) * hn).astype(cell_ref.dtype)
        b_last = b_col[L - 1:L, :]
        lw = b_last + r_row
        m_new = jnp.maximum(b_last + m_prev, jnp.max(lw, axis=1, keepdims=True))
        wkt = (jnp.exp(lw - m_new) * kt.astype(F32)).astype(BF16)
        v_aug = jnp.concatenate([vh, ones_aug], axis=1)
        state[h] = jnp.exp(b_last + m_prev - m_new) * s_aug + _dot(wkt, v_aug)
        m_new_all.append(jnp.broadcast_to(m_new, (1, V7X_LANES)))

    pad_rows = m_state.shape[0] - MLSTM_HEADS
    m_state[...] = jnp.concatenate(m_new_all + [jnp.zeros((pad_rows, V7X_LANES), F32)], axis=0)


def _mlstm(q, kt, v, og, gt, gtt, norm_g, c0, n0, m0, n_seq, seq_len, chunk):
    w, hh, e = MLSTM_WIDTH, MLSTM_HEADS, HEAD_DIM
    nc = seq_len // chunk
    mrows = 2 * V7X_SUBLANES
    s0 = jnp.concatenate([c0, jnp.broadcast_to(n0[..., None], (n_seq, hh, e, e))], axis=-1)
    m0b = jnp.pad(jnp.broadcast_to(m0[..., None], (n_seq, hh, V7X_LANES)), ((0, 0), (0, mrows - hh), (0, 0)))
    row = lambda width: pl.BlockSpec((chunk, width), lambda b, c: (b * nc + c, 0))
    col = lambda height: pl.BlockSpec((height, chunk), lambda b, c: (0, b * nc + c))
    st = lambda shp: pl.BlockSpec((None,) + shp, lambda b, c: (b,) + (0,) * len(shp))
    cell, s_out, m_out = pl.pallas_call(
        _mlstm_body,
        grid=(n_seq, nc),
        in_specs=[row(w), col(w), row(w), row(w), row(GATE_LANES), col(GATE_LANES),
                  pl.BlockSpec((1, w), lambda b, c: (0, 0)), st((hh, e, 2 * e)), st((mrows, V7X_LANES))],
        out_specs=[row(w), st((hh, e, 2 * e)), st((mrows, V7X_LANES))],
        out_shape=[jax.ShapeDtypeStruct((n_seq * seq_len, w), BF16),
                   jax.ShapeDtypeStruct((n_seq, hh, e, 2 * e), F32),
                   jax.ShapeDtypeStruct((n_seq, mrows, V7X_LANES), F32)],
        compiler_params=_cparams(2, VMEM_LIMIT),
        name="mlstm_chunks",
    )(q, kt, v, og, gt, gtt, norm_g.reshape(1, w), s0, m0b)
    return cell, s_out[..., 0:e], s_out[..., e], m_out[:, 0:hh, 0]


def _w_in_b_regroup(w_in, b_if):
    w4 = 4 * MLSTM_WIDTH
    hh = MLSTM_HEADS
    z = lambda n: jnp.zeros((D_MODEL, n), w_in.dtype)
    w = jnp.concatenate([w_in[:, :w4 + hh], z(GATE_F0 - hh), w_in[:, w4 + hh:w4 + 2 * hh],
                         z(GATE_LANES - GATE_F0 - hh), w_in[:, w4 + 2 * hh:]], axis=1)
    zb = lambda n: jnp.zeros((n,), F32)
    bias = jnp.concatenate([b_if[0].astype(F32), zb(GATE_F0 - hh), b_if[1].astype(F32),
                            zb(GATE_LANES - GATE_F0 - hh)]).reshape(1, GATE_LANES)
    return w, bias


def _layer_b_prompt(h_all, w_bf, bias, conv_w, norm_g, w_out, mem_prompt, w_mem, ln_g, ln_b, n_seq, seq_len,
                    tail):
    m = n_seq * seq_len
    tile = min(ROW_TILE, seq_len)
    chunk = min(MLSTM_CHUNK, seq_len)
    zstate = jnp.zeros((n_seq * V7X_SUBLANES, 2 * MLSTM_WIDTH), F32)
    q, kt, v, og, gt, gtt, mq, conv = _in_proj_b(h_all, 0, m, w_bf, conv_w, bias, [zstate] * 3, n_seq, seq_len,
                                                 tile, False)
    hh, e = MLSTM_HEADS, HEAD_DIM
    cell, c_out, n_out, m_out = _mlstm(q, kt, v, og, gt, gtt, norm_g, jnp.zeros((n_seq, hh, e, e), F32),
                                       jnp.zeros((n_seq, hh, e), F32), jnp.zeros((n_seq, hh), F32),
                                       n_seq, seq_len, chunk)
    mem_kv = _matmul(mem_prompt.reshape(n_seq * N_MEM, D_MODEL), w_mem, N_MEM)
    mo = _mem_attn(mq, mem_kv.reshape(n_seq, N_MEM, 2 * MEM_WIDTH), seq_len, tile)
    h1 = _out_proj(cell, mo, h_all, 0, w_out.astype(BF16), ln_g, ln_b, tile, tail)
    return h1, conv, c_out, n_out.reshape(n_seq, hh, e), m_out.reshape(n_seq, hh), mem_kv


def _layer_b_sample(h_all, h_off, w_bf, bias, conv_w, norm_g, w_out, mem_kv, conv_state, c0, n0, m0, ln_g, ln_b,
                    n_b, n_new):
    m = n_b * n_new
    w2 = 2 * MLSTM_WIDTH
    hh, e = MLSTM_HEADS, HEAD_DIM
    chunk = MLSTM_CHUNK
    t = jnp.arange(n_new)
    prevs = []
    for kk in (1, 2, 3):
        idx = jnp.clip(CONV_W - 1 - kk + t, 0, CONV_W - 2)
        prevs.append(conv_state[:, idx, :].reshape(m, w2))
    q, kt, v, og, gt, gtt, mq, u = _in_proj_b(h_all, h_off, m, w_bf, conv_w, bias, prevs, n_b, n_new, m, True)
    conv_new = jnp.concatenate([conv_state, u.reshape(n_b, n_new, w2)], axis=1)[:, n_new:]
    npad = chunk - n_new
    pad3 = lambda a: jnp.pad(a.reshape(n_b, n_new, a.shape[1]), ((0, 0), (0, npad), (0, 0)))
    gate_pad = jnp.where(jnp.arange(GATE_LANES) < hh, NEG_INF, 0.0).astype(F32)
    gtp = jnp.concatenate([gt.reshape(n_b, n_new, GATE_LANES),
                           jnp.broadcast_to(gate_pad, (n_b, npad, GATE_LANES))], axis=1)
    gttp = jnp.concatenate([gtt.reshape(GATE_LANES, n_b, n_new),
                            jnp.broadcast_to(gate_pad[:, None, None], (GATE_LANES, n_b, npad))], axis=2)
    ktp = jnp.pad(kt.reshape(MLSTM_WIDTH, n_b, n_new), ((0, 0), (0, 0), (0, npad)))
    flat = lambda a: a.reshape(n_b * chunk, a.shape[2])
    flat_t = lambda a: a.reshape(a.shape[0], n_b * chunk)
    cell, c_out, n_out, m_out = _mlstm(flat(pad3(q)), flat_t(ktp), flat(pad3(v)), flat(pad3(og)), flat(gtp),
                                       flat_t(gttp), norm_g, c0, n0, m0, n_b, chunk, chunk)
    cell = cell.reshape(n_b, chunk, MLSTM_WIDTH)[:, :n_new].reshape(m, MLSTM_WIDTH)
    mo = _sample_mem_attn(mq.astype(F32).reshape(n_b, n_new, MEM_WIDTH),
                          mem_kv.reshape(n_b, N_MEM, 2 * MEM_WIDTH), n_b, n_new)
    h1 = _out_proj(cell, mo.reshape(m, MEM_WIDTH), h_all, h_off, w_out, ln_g, ln_b, m)
    return h1, conv_new, c_out, n_out.reshape(n_b, hh, e), m_out.reshape(n_b, hh)


def _layer_a_sample(h2d, w_in, w_out, mem_kv, caches, ln_g, ln_b, n_b, n_new):
    m = n_b * n_new
    pos = PAST_LEN + (jnp.arange(m, dtype=jnp.int32) % n_new)
    outs = _in_proj_a(h2d, w_in, _rope_tables(pos), n_new, n_b, m, full_kv=True)
    qs, ks, vs, mq, kvs = outs[0:3], outs[3:6], outs[6:9], outs[9], outs[10:13]
    f3 = lambda t: t.reshape(n_b, n_new, t.shape[1])
    caches = [c.reshape(n_b, c.shape[1], 2 * DIL_WIDTH) for c in caches]
    mix = _sample_attn([f3(t) for t in qs], [f3(t) for t in ks], [f3(t) for t in vs], caches, n_b, n_new)
    mo = _sample_mem_attn(f3(mq), mem_kv.reshape(n_b, N_MEM, 2 * MEM_WIDTH), n_b, n_new)
    h1 = _out_proj(mix.reshape(m, DIL_WIDTH), mo.reshape(m, MEM_WIDTH), h2d, 0, w_out, ln_g, ln_b, m)
    return h1, kvs


def _layer_a_prompt(h2d, mem_prompt, w_in, w_out, w_mem, ln_g, ln_b, n_seq, seq_len, tail):
    tile = min(ROW_TILE, seq_len)
    tabs = _rope_tables(jnp.arange(seq_len, dtype=jnp.int32))
    outs = _in_proj_a(h2d, w_in.astype(BF16), tabs, seq_len, n_seq, tile, full_kv=False)
    qs, ks, vs, mq, kvs = outs[0:3], outs[3:6], outs[6:9], outs[9], outs[10:13]
    mem_kv = _matmul(mem_prompt.reshape(n_seq * N_MEM, D_MODEL), w_mem, N_MEM)
    os_, lses = [], []
    for g, (win, dil) in enumerate(DIL_PAIRS):
        o, lse = _dil_attn(qs[g], ks[g], vs[g], n_seq, seq_len, dil, win // dil)
        os_.append(o)
        lses.append(lse)
    mix = _group_mix(os_, lses, tile)
    mo = _mem_attn(mq, mem_kv.reshape(n_seq, N_MEM, 2 * MEM_WIDTH), seq_len, tile)
    h1 = _out_proj(mix, mo, h2d, 0, w_out.astype(BF16), ln_g, ln_b, tile, tail)
    return h1, kvs, mem_kv


def kernel(x_prompt, x_sample, mem_prompt, cache_win1_kv, cache_win2_kv, cache_win3_kv, cache_mem_kv,
           state_mlstm_conv, state_mlstm_C, state_mlstm_n, state_mlstm_m,
           w_in_A, w_out_A, w_in_B, mlstm_conv_w, mlstm_b_if, mlstm_norm_g, w_out_B,
           w_mem_kv, ln_g, ln_b, w_grp, b_grp, w_exp, b_exp, w_gate, w_up, w_down):
    n_p, seq, d = x_prompt.shape
    n_s, t_new, _ = x_sample.shape
    assert d == D_MODEL and w_in_A.shape[0] == 1 and w_in_B.shape[0] == 1
    mp, ms = n_p * seq, n_s * t_new
    m_all = mp + ms
    assert mp % ROW_TILE == 0 and ms % SMALL_TILE == 0 and mp % ms == 0
    xp = x_prompt.reshape(mp, d)
    xs = x_sample.reshape(ms, d)

    wg_bf, wu_bf, wd_bf = w_gate.astype(BF16), w_up.astype(BF16), w_down.astype(BF16)

    def moe(h, i, parts):
        return _moe_layer(h, w_grp[i], b_grp[i], w_exp[i], b_exp[i], wg_bf, wu_bf, wd_bf, i,
                          ln_g[i, 1], ln_b[i, 1], parts)

    caches = (cache_win1_kv[0], cache_win2_kv[0], cache_win3_kv[0])
    hs, kv_s = _layer_a_sample(xs, w_in_A[0], w_out_A[0], cache_mem_kv[0], caches, ln_g[0, 0], ln_b[0, 0],
                               n_s, t_new)
    h, kv_p, mem_kv0 = _layer_a_prompt(xp, mem_prompt, w_in_A[0], w_out_A[0], w_mem_kv[0], ln_g[0, 0], ln_b[0, 0],
                                       n_p, seq, hs)
    (h,) = moe(h, 0, [(0, m_all)])

    w_b, bias = _w_in_b_regroup(w_in_B[0], mlstm_b_if[0])
    w_bf = w_b.astype(BF16)
    hs, conv_s, c_s, nv_s, mm_s = _layer_b_sample(
        h, mp, w_b, bias, mlstm_conv_w[0], mlstm_norm_g[0], w_out_B[0], cache_mem_kv[1],
        state_mlstm_conv[0], state_mlstm_C[0], state_mlstm_n[0], state_mlstm_m[0],
        ln_g[1, 0], ln_b[1, 0], n_s, t_new)
    h1, conv_p, c_p, nv_p, mm_p, mem_kv1 = _layer_b_prompt(
        h, w_bf, bias, mlstm_conv_w[0], mlstm_norm_g[0], w_out_B[0], mem_prompt, w_mem_kv[1],
        ln_g[1, 0], ln_b[1, 0], n_p, seq, hs)
    yp, ys = moe(h1, 1, [(0, mp), (mp, ms)])

    kv_shape = (2, HEADS_PER_GROUP, HEAD_DIM)
    wins = []
    for g in range(N_DIL_GROUPS):
        wins.append(kv_p[g].reshape((1, n_p, -1) + kv_shape))
        wins.append(kv_s[g].reshape((1, n_s, t_new) + kv_shape))
    mem_kv_p = jnp.stack([mem_kv0, mem_kv1]).reshape((2, n_p, N_MEM, 2, MEM_HEADS, HEAD_DIM))
    return (yp.reshape(n_p, seq, d), ys.reshape(n_s, t_new, d), *wins,
            conv_p[None], conv_s[None], c_p[None], c_s[None], nv_p[None], nv_s[None], mm_p[None], mm_s[None],
            mem_kv_p)
```

```python
import functools

import jax
import jax.numpy as jnp
import numpy as np
from jax import lax
from jax.experimental import pallas as pl
from jax.experimental.pallas import tpu as pltpu

D_MODEL = 1024
HEAD_DIM = 64
ATTN_SCALE = HEAD_DIM ** -0.5
PAST_LEN = 16384
N_MEM = 256
MEM_HEADS = 4
MEM_WIDTH = MEM_HEADS * HEAD_DIM
DIL_PAIRS = ((128, 1), (512, 4), (2048, 16))
N_DIL_GROUPS = len(DIL_PAIRS)
HEADS_PER_GROUP = 4
DIL_WIDTH = HEADS_PER_GROUP * HEAD_DIM
QBLK = 128
ROPE_THETA = 500000.0
ROPE_DIMS = HEAD_DIM // 4
MLSTM_HEADS = 12
MLSTM_WIDTH = MLSTM_HEADS * HEAD_DIM
CONV_W = 4
N_GROUPS = 4
EXPERTS_PER_GROUP = 8
N_EXPERTS = N_GROUPS * EXPERTS_PER_GROUP
D_EXPERT = 256
DEPTH = 2
ALPHA = (2 * DEPTH) ** 0.25
LN_EPS = 1e-5

V7X_LANES = 128
V7X_SUBLANES = 8
V7X_VMEM_BYTES = 64 * 1024 * 1024
VMEM_LIMIT = 48 * 1024 * 1024

ROW_TILE = 512
SMALL_TILE = 128
MLSTM_CHUNK = 128
EXPERT_ROWS = 256

BF16 = jnp.bfloat16
F32 = jnp.float32
NEG_INF = float("-inf")


def _cparams(n_axes, vmem=None, flags=None):
    return pltpu.CompilerParams(dimension_semantics=("arbitrary",) * n_axes,
                                vmem_limit_bytes=vmem, flags=flags)


def _nt_dot(a, b):
    return lax.dot_general(a, b, (((1,), (1,)), ((), ())), preferred_element_type=F32)


def _dot(a, b):
    return jnp.dot(a, b, preferred_element_type=F32)


def _split_bf16(x):
    hi = x.astype(BF16)
    lo = (x - hi.astype(F32)).astype(BF16)
    return hi, lo


def _lhs(x, precise):
    return _split_bf16(x) if precise else (x.astype(BF16),)


def _mm(lhs, w):
    if len(lhs) == 1:
        return _dot(lhs[0], w)
    wh, wl = _split_bf16(w)
    return _dot(lhs[0], wh) + (_dot(lhs[0], wl) + _dot(lhs[1], wh))


def _rope_tables(pos):
    half = ROPE_DIMS // 2
    inv = jnp.power(ROPE_THETA, -jnp.arange(half, dtype=F32) * (2.0 / ROPE_DIMS))
    ang = pos.astype(F32)[:, None] * inv[None, :]
    cos, sin = jnp.cos(ang), jnp.sin(ang)
    n = pos.shape[0]
    one = jnp.ones((n, HEAD_DIM - ROPE_DIMS), F32)
    zero8 = jnp.zeros((n, half), F32)
    zrest = jnp.zeros((n, HEAD_DIM - ROPE_DIMS), F32)
    a = jnp.concatenate([cos, cos, one], axis=1)
    b = jnp.concatenate([zero8, sin, zrest], axis=1)
    c = jnp.concatenate([-sin, zero8, zrest], axis=1)
    rep = V7X_LANES // HEAD_DIM
    return jnp.tile(a, (1, rep)), jnp.tile(b, (1, rep)), jnp.tile(c, (1, rep))


def _rope_apply(x, ra, rb, rc):
    parts = []
    for s in range(x.shape[1] // V7X_LANES):
        v = x[:, s * V7X_LANES:(s + 1) * V7X_LANES]
        parts.append(v * ra + pltpu.roll(v, ROPE_DIMS // 2, 1) * rb
                     + pltpu.roll(v, V7X_LANES - ROPE_DIMS // 2, 1) * rc)
    return jnp.concatenate(parts, axis=1)


def _in_proj_a_body(x_ref, w_ref, ra_ref, rb_ref, rc_ref, *outs, kv_rows, precise):
    q_refs, k_refs, v_refs = outs[0:3], outs[3:6], outs[6:9]
    mq_ref = outs[9]
    kv_refs = outs[10:13]
    lhs = _lhs(x_ref[...], precise)
    ra, rb, rc = ra_ref[...], rb_ref[...], rc_ref[...]
    tm = x_ref.shape[0]
    gw = DIL_WIDTH
    for g in range(N_DIL_GROUPS):
        q = _rope_apply(_mm(lhs, w_ref[:, g * gw:(g + 1) * gw]), ra, rb, rc)
        k = _rope_apply(_mm(lhs, w_ref[:, (3 + g) * gw:(4 + g) * gw]), ra, rb, rc)
        v = _mm(lhs, w_ref[:, (6 + g) * gw:(7 + g) * gw])
        q_refs[g][...] = (q * ATTN_SCALE).astype(q_refs[g].dtype)
        k_refs[g][...] = k.astype(k_refs[g].dtype)
        v_refs[g][...] = v.astype(v_refs[g].dtype)
        r = kv_rows[g]
        kv_refs[g][:, 0:gw] = k[tm - r:, :]
        kv_refs[g][:, gw:2 * gw] = v[tm - r:, :]
    mq = _mm(lhs, w_ref[:, 9 * gw:9 * gw + MEM_WIDTH])
    mq_ref[...] = (mq * ATTN_SCALE).astype(mq_ref.dtype)


def _in_proj_a(x2d, w, tabs, seq_len, n_seq, tile, full_kv):
    w_bf = w
    act = F32 if full_kv else BF16
    m = x2d.shape[0]
    nt = m // tile
    gw = DIL_WIDTH
    row_spec = lambda w: pl.BlockSpec((tile, w), lambda i: (i, 0))
    if full_kv:
        tab_spec = pl.BlockSpec((tile, V7X_LANES), lambda i: (i, 0))
        kv_rows = (tile,) * 3
        kv_shapes = [jax.ShapeDtypeStruct((m, 2 * gw), F32)] * 3
        kv_specs = [row_spec(2 * gw)] * 3
    else:
        tps = seq_len // tile
        tab_spec = pl.BlockSpec((tile, V7X_LANES), lambda i: (i % tps, 0))
        kv_rows, kv_shapes, kv_specs = [], [], []
        for win, _ in DIL_PAIRS:
            wb = min(win, seq_len)
            r = min(wb, tile)
            nblk = wb // r
            kv_rows.append(r)
            kv_shapes.append(jax.ShapeDtypeStruct((n_seq * wb, 2 * gw), F32))
            kv_specs.append(pl.BlockSpec(
                (r, 2 * gw),
                lambda i, nblk=nblk: ((i // tps) * nblk + jnp.maximum(i % tps - (tps - nblk), 0), 0)))
        kv_rows = tuple(kv_rows)
    out_shapes = [jax.ShapeDtypeStruct((m, gw), act)] * 9 + [jax.ShapeDtypeStruct((m, MEM_WIDTH), act)] + kv_shapes
    out_specs = [row_spec(gw)] * 9 + [row_spec(MEM_WIDTH)] + kv_specs
    return pl.pallas_call(
        functools.partial(_in_proj_a_body, kv_rows=kv_rows, precise=full_kv),
        grid=(nt,),
        in_specs=[row_spec(D_MODEL), pl.BlockSpec(w_bf.shape, lambda i: (0, 0)), tab_spec, tab_spec, tab_spec],
        out_specs=out_specs,
        out_shape=out_shapes,
        compiler_params=_cparams(1, VMEM_LIMIT),
        name="in_proj_a",
    )(x2d, w_bf, *tabs)


def _dil_attn_body(q_ref, k_ref, v_ref, kp_ref, vp_ref, o_ref, lse_ref, *, span):
    i = pl.program_id(2)
    q, k, v = q_ref[...], k_ref[...], v_ref[...]
    kp, vp = kp_ref[...], vp_ref[...]
    qi = lax.broadcasted_iota(jnp.int32, (QBLK, 2 * QBLK), 0) + QBLK
    ki = lax.broadcasted_iota(jnp.int32, (QBLK, 2 * QBLK), 1)
    band = (qi >= ki) & (qi - ki <= span) & ((i > 0) | (ki >= QBLK))
    lses = []
    for h in range(HEADS_PER_GROUP):
        hs = slice(h * HEAD_DIM, (h + 1) * HEAD_DIM)
        kc = jnp.concatenate([kp[:, hs], k[:, hs]], axis=0)
        vc = jnp.concatenate([vp[:, hs], v[:, hs]], axis=0)
        s = jnp.where(band, _nt_dot(q[:, hs], kc), NEG_INF)
        m = jnp.max(s, axis=1, keepdims=True)
        p = jnp.exp(s - m)
        den = jnp.sum(p, axis=1, keepdims=True)
        o_ref[:, hs] = _dot((p / den).astype(BF16), vc).astype(o_ref.dtype)
        lses.append(m + jnp.log(den))
    lse_ref[...] = jnp.concatenate(lses, axis=1)


def _dil_attn(q, k, v, n_seq, seq_len, dil, span):
    gw = DIL_WIDTH
    L = seq_len // dil
    nb = L // QBLK
    view = lambda t: t.reshape(n_seq, L, dil * gw)
    blk = pl.BlockSpec((None, QBLK, gw), lambda b, r, i: (b, i, r))
    prev = pl.BlockSpec((None, QBLK, gw), lambda b, r, i: (b, jnp.maximum(i - 1, 0), r))
    o, lse = pl.pallas_call(
        functools.partial(_dil_attn_body, span=span),
        grid=(n_seq, dil, nb),
        in_specs=[blk, blk, blk, prev, prev],
        out_specs=[blk, pl.BlockSpec((None, None, QBLK, HEADS_PER_GROUP), lambda b, r, i: (b, r, i, 0))],
        out_shape=[jax.ShapeDtypeStruct((n_seq, L, dil * gw), BF16),
                   jax.ShapeDtypeStruct((n_seq, dil, L, HEADS_PER_GROUP), F32)],
        compiler_params=_cparams(3),
        name=f"dil_attn_d{dil}",
    )(view(q), view(k), view(v), view(k), view(v))
    o = o.reshape(n_seq * seq_len, gw)
    lse = jnp.transpose(lse, (0, 2, 1, 3)).reshape(n_seq * seq_len, HEADS_PER_GROUP)
    return o, lse


def _group_mix_body(o1, o2, o3, l1, l2, l3, mix_ref):
    os_ = (o1, o2, o3)
    ls = (l1[...], l2[...], l3[...])
    for h in range(HEADS_PER_GROUP):
        hs = slice(h * HEAD_DIM, (h + 1) * HEAD_DIM)
        lh = [l[:, h:h + 1] for l in ls]
        mx = jnp.maximum(jnp.maximum(lh[0], lh[1]), lh[2])
        e = [jnp.exp(x - mx) for x in lh]
        tot = e[0] + e[1] + e[2]
        acc = (e[0] / tot) * os_[0][:, hs].astype(F32)
        acc = acc + (e[1] / tot) * os_[1][:, hs].astype(F32)
        acc = acc + (e[2] / tot) * os_[2][:, hs].astype(F32)
        mix_ref[:, hs] = acc.astype(mix_ref.dtype)


def _group_mix(os_, lses, tile):
    m = os_[0].shape[0]
    gw = DIL_WIDTH
    ospec = pl.BlockSpec((tile, gw), lambda i: (i, 0))
    lspec = pl.BlockSpec((tile, HEADS_PER_GROUP), lambda i: (i, 0))
    return pl.pallas_call(
        _group_mix_body,
        grid=(m // tile,),
        in_specs=[ospec] * 3 + [lspec] * 3,
        out_specs=ospec,
        out_shape=jax.ShapeDtypeStruct((m, gw), BF16),
        compiler_params=_cparams(1),
        name="group_mix",
    )(*os_, *lses)


def _mem_attn_body(q_ref, kv_ref, o_ref):
    q = q_ref[...]
    kv = kv_ref[...].astype(BF16)
    for h in range(MEM_HEADS):
        hs = slice(h * HEAD_DIM, (h + 1) * HEAD_DIM)
        vs = slice(MEM_WIDTH + h * HEAD_DIM, MEM_WIDTH + (h + 1) * HEAD_DIM)
        s = _nt_dot(q[:, hs], kv[:, hs])
        m = jnp.max(s, axis=1, keepdims=True)
        p = jnp.exp(s - m)
        den = jnp.sum(p, axis=1, keepdims=True)
        o_ref[:, hs] = _dot((p / den).astype(BF16), kv[:, vs]).astype(o_ref.dtype)


def _mem_attn(q, mem_kv, seq_len, tile):
    m = q.shape[0]
    tps = seq_len // tile
    return pl.pallas_call(
        _mem_attn_body,
        grid=(m // tile,),
        in_specs=[pl.BlockSpec((tile, MEM_WIDTH), lambda i: (i, 0)),
                  pl.BlockSpec((None, N_MEM, 2 * MEM_WIDTH), lambda i: (i // tps, 0, 0))],
        out_specs=pl.BlockSpec((tile, MEM_WIDTH), lambda i: (i, 0)),
        out_shape=jax.ShapeDtypeStruct((m, MEM_WIDTH), BF16),
        compiler_params=_cparams(1),
        name="mem_attn",
    )(q, mem_kv)


def _layer_norm_rows(x, g, b):
    mu = jnp.mean(x, axis=1, keepdims=True)
    xc = x - mu
    var = jnp.mean(xc * xc, axis=1, keepdims=True)
    return xc * lax.rsqrt(var + LN_EPS) * g + b


def _out_proj_body(a_ref, mo_ref, h_ref, w_ref, g_ref, b_ref, *rest, n_tiles, precise):
    o_ref = rest[-1]
    ka = a_ref.shape[1]

    def rows():
        y = _mm(_lhs(a_ref[...], precise), w_ref[0:ka, :]) + _mm(_lhs(mo_ref[...], precise), w_ref[ka:, :])
        o_ref[...] = _layer_norm_rows(ALPHA * h_ref[...] + y, g_ref[...], b_ref[...])

    if len(rest) == 1:
        rows()
    else:
        tail_ref = rest[0]
        pl.when(pl.program_id(0) < n_tiles)(rows)

        @pl.when(pl.program_id(0) == n_tiles)
        def _():
            o_ref[0:tail_ref.shape[0], :] = tail_ref[...]


def _out_proj(a, mo, h, h_off, w, g, b, tile, tail=None):
    w_bf = w
    m, ka = a.shape
    nt = m // tile
    last = nt - 1
    row = lambda w, off=0: pl.BlockSpec((tile, w), lambda i: (jnp.minimum(i, last) + off // tile, 0))
    const = lambda s: pl.BlockSpec(s, lambda i: (0, 0))
    in_specs = [row(ka), row(MEM_WIDTH), row(D_MODEL, h_off), const(w_bf.shape), const((1, D_MODEL)),
                const((1, D_MODEL))]
    args = [a, mo, h, w_bf, g.reshape(1, D_MODEL), b.reshape(1, D_MODEL)]
    out_rows, steps = m, nt
    if tail is not None:
        assert tail.shape[0] <= tile
        in_specs.append(const(tail.shape))
        args.append(tail)
        out_rows, steps = m + tail.shape[0], nt + 1
    return pl.pallas_call(
        functools.partial(_out_proj_body, n_tiles=nt, precise=(w.dtype == F32)),
        grid=(steps,),
        in_specs=in_specs,
        out_specs=pl.BlockSpec((tile, D_MODEL), lambda i: (i, 0)),
        out_shape=jax.ShapeDtypeStruct((out_rows, D_MODEL), F32),
        compiler_params=_cparams(1, VMEM_LIMIT),
        name="out_proj_ln",
    )(*args)


def _matmul_body(x_ref, w_ref, o_ref):
    o_ref[...] = _dot(x_ref[...].astype(BF16), w_ref[...].astype(BF16))


def _matmul(x, w, tile):
    m, k = x.shape
    n = w.shape[1]
    return pl.pallas_call(
        _matmul_body,
        grid=(m // tile,),
        in_specs=[pl.BlockSpec((tile, k), lambda i: (i, 0)), pl.BlockSpec((k, n), lambda i: (0, 0))],
        out_specs=pl.BlockSpec((tile, n), lambda i: (i, 0)),
        out_shape=jax.ShapeDtypeStruct((m, n), F32),
        compiler_params=_cparams(1),
        name="mem_kv_proj",
    )(x, w)


def _col_attend(q_row, kmat, vmat, kmask, knew, vnew, nmask):
    pk = kmat * q_row
    pn = None if knew is None else knew * q_row
    outs, lses = [], []
    for h in range(HEADS_PER_GROUP):
        hs = slice(h * HEAD_DIM, (h + 1) * HEAD_DIM)
        s = jnp.sum(pk[:, hs], axis=1, keepdims=True)
        if kmask is not None:
            s = jnp.where(kmask, s, NEG_INF)
        m = jnp.max(s, axis=0, keepdims=True)
        if pn is not None:
            sn = jnp.where(nmask, jnp.sum(pn[:, hs], axis=1, keepdims=True), NEG_INF)
            m = jnp.maximum(m, jnp.max(sn, axis=0, keepdims=True))
        p = jnp.exp(s - m)
        den = jnp.sum(p, axis=0, keepdims=True)
        acc = jnp.sum(p * vmat[:, hs], axis=0, keepdims=True)
        if pn is not None:
            pnw = jnp.exp(sn - m)
            den = den + jnp.sum(pnw, axis=0, keepdims=True)
            acc = acc + jnp.sum(pnw * vnew[:, hs], axis=0, keepdims=True)
        outs.append(acc / den)
        lses.append(m + jnp.log(den))
    return outs, lses


def _sample_attn_body(q1, q2, q3, kn1, kn2, kn3, vn1, vn2, vn3, c1, c2, c3, mix_ref, *, n_new):
    qs = (q1[...], q2[...], q3[...])
    kns = (kn1[...], kn2[...], kn3[...])
    vns = (vn1[...], vn2[...], vn3[...])
    gw = DIL_WIDTH
    t_idx = lax.broadcasted_iota(jnp.int32, (n_new, 1), 0)
    rows = []
    for t in range(n_new):
        per_group = []
        for g, (win, dil) in enumerate(DIL_PAIRS):
            q_row = qs[g][t:t + 1, :]
            cache = (c1, c2, c3)[g]
            if dil == 1:
                kmat, vmat = cache[:, 0:gw], cache[:, gw:2 * gw]
                r_idx = lax.broadcasted_iota(jnp.int32, (kmat.shape[0], 1), 0)
                per_group.append(_col_attend(q_row, kmat, vmat, r_idx >= t, kns[g], vns[g], t_idx <= t))
            else:
                base = t * 2 * gw
                kmat, vmat = cache[:, base:base + gw], cache[:, base + gw:base + 2 * gw]
                per_group.append(_col_attend(q_row, kmat, vmat, None, kns[g], vns[g], t_idx == t))
        heads = []
        for h in range(HEADS_PER_GROUP):
            lh = [per_group[g][1][h] for g in range(N_DIL_GROUPS)]
            mx = jnp.maximum(jnp.maximum(lh[0], lh[1]), lh[2])
            e = [jnp.exp(x - mx) for x in lh]
            tot = e[0] + e[1] + e[2]
            acc = (e[0] / tot) * per_group[0][0][h]
            acc = acc + (e[1] / tot) * per_group[1][0][h]
            acc = acc + (e[2] / tot) * per_group[2][0][h]
            heads.append(acc)
        rows.append(jnp.concatenate(heads, axis=1))
    mix_ref[...] = jnp.concatenate(rows, axis=0)


def _sample_attn(qs, kns, vns, caches, n_b, n_new):
    gw = DIL_WIDTH
    small = pl.BlockSpec((None, n_new, gw), lambda b: (b, 0, 0))
    cviews, cspecs = [], []
    for c, (win, dil) in zip(caches, DIL_PAIRS):
        blocks = win // dil
        cviews.append(c.reshape(n_b, blocks, dil * 2 * gw))
        lanes = min(dil, n_new) * 2 * gw
        cspecs.append(pl.BlockSpec((None, blocks, lanes), lambda b: (b, 0, 0)))
    return pl.pallas_call(
        functools.partial(_sample_attn_body, n_new=n_new),
        grid=(n_b,),
        in_specs=[small] * 9 + cspecs,
        out_specs=small,
        out_shape=jax.ShapeDtypeStruct((n_b, n_new, gw), F32),
        compiler_params=_cparams(1, VMEM_LIMIT),
        name="sample_dil_attn",
    )(*qs, *kns, *vns, *cviews)


def _sample_mem_attn_body(q_ref, kv_ref, o_ref, *, n_new):
    q = q_ref[...]
    kmat, vmat = kv_ref[:, 0:MEM_WIDTH], kv_ref[:, MEM_WIDTH:2 * MEM_WIDTH]
    rows = []
    for t in range(n_new):
        outs, _ = _col_attend(q[t:t + 1, :], kmat, vmat, None, None, None, None)
        rows.append(jnp.concatenate(outs, axis=1))
    o_ref[...] = jnp.concatenate(rows, axis=0)


def _sample_mem_attn(q, mem_kv, n_b, n_new):
    small = pl.BlockSpec((None, n_new, MEM_WIDTH), lambda b: (b, 0, 0))
    return pl.pallas_call(
        functools.partial(_sample_mem_attn_body, n_new=n_new),
        grid=(n_b,),
        in_specs=[small, pl.BlockSpec((None, N_MEM, 2 * MEM_WIDTH), lambda b: (b, 0, 0))],
        out_specs=small,
        out_shape=jax.ShapeDtypeStruct((n_b, n_new, MEM_WIDTH), F32),
        compiler_params=_cparams(1),
        name="sample_mem_attn",
    )(q, mem_kv)


ROUTE_LANES = V7X_LANES


def _router_body(x_ref, w_ref, b_ref, cnt0_ref, o_ref, cnt_ref):
    xh, xl = _split_bf16(x_ref[...])
    wh, wl = _split_bf16(w_ref[...])
    logits = _dot(xh, wh) + (_dot(xh, wl) + _dot(xl, wh)) + b_ref[...]
    lane = lax.broadcasted_iota(jnp.int32, logits.shape, 1)
    big = jnp.int32(ROUTE_LANES)
    is_grp = (lane >= N_EXPERTS) & (lane < N_EXPERTS + N_GROUPS)
    gl = jnp.where(is_grp, logits, NEG_INF)
    gmax = jnp.max(gl, axis=1, keepdims=True)
    gsel = jnp.min(jnp.where(gl == gmax, lane, big), axis=1, keepdims=True) - N_EXPERTS
    gp = 1.0 / jnp.sum(jnp.exp(gl - gmax), axis=1, keepdims=True)
    in_grp = (lane < N_EXPERTS) & ((lane // EXPERTS_PER_GROUP) == gsel)
    el = jnp.where(in_grp, logits, NEG_INF)
    v1 = jnp.max(el, axis=1, keepdims=True)
    i1 = jnp.min(jnp.where(el == v1, lane, big), axis=1, keepdims=True)
    el2 = jnp.where(lane == i1, NEG_INF, el)
    v2 = jnp.max(el2, axis=1, keepdims=True)
    i2 = jnp.min(jnp.where(el2 == v2, lane, big), axis=1, keepdims=True)
    e2 = jnp.exp(v2 - v1)
    w1 = (1.0 / (1.0 + e2)) * gp
    w2 = (e2 / (1.0 + e2)) * gp
    @pl.when(pl.program_id(0) == 0)
    def _():
        cnt_ref[...] = cnt0_ref[...]

    tm = logits.shape[0]
    oh1 = (lane == i1).astype(F32)
    oh2 = (lane == i2).astype(F32)
    tri = (lax.broadcasted_iota(jnp.int32, (tm, tm), 0) > lax.broadcasted_iota(jnp.int32, (tm, tm), 1)).astype(BF16)
    base = cnt_ref[0:1, :]
    c1 = jnp.sum(oh1, axis=0, keepdims=True)
    c2 = jnp.sum(oh2, axis=0, keepdims=True)
    r1 = jnp.sum(oh1 * (_dot(tri, oh1.astype(BF16)) + base), axis=1, keepdims=True)
    r2 = jnp.sum(oh2 * (_dot(tri, oh2.astype(BF16)) + (base + c1)), axis=1, keepdims=True)
    cnt_ref[...] = jnp.broadcast_to(base + c1 + c2, cnt_ref.shape)
    out = jnp.where(lane == 0, i1.astype(F32), 0.0)
    out = jnp.where(lane == 1, i2.astype(F32), out)
    out = jnp.where(lane == 2, w1, out)
    out = jnp.where(lane == 3, w2, out)
    out = jnp.where(lane == 4, r1, out)
    out = jnp.where(lane == 5, r2, out)
    o_ref[...] = out


def _router(x, row_off, n_rows, w_grp, b_grp, w_exp, b_exp, tile, counts0):
    pad = ROUTE_LANES - N_EXPERTS - N_GROUPS
    w = jnp.concatenate([w_exp, w_grp, jnp.zeros((D_MODEL, pad), F32)], axis=1)
    b = jnp.concatenate([b_exp, b_grp, jnp.zeros((pad,), F32)]).reshape(1, ROUTE_LANES)
    ob = row_off // tile
    const = lambda s: pl.BlockSpec(s, lambda i: (0, 0))
    cshape = (V7X_SUBLANES, ROUTE_LANES)
    return pl.pallas_call(
        _router_body,
        grid=(n_rows // tile,),
        in_specs=[pl.BlockSpec((tile, D_MODEL), lambda i: (i + ob, 0)), const(w.shape), const(b.shape), const(cshape)],
        out_specs=[pl.BlockSpec((tile, ROUTE_LANES), lambda i: (i, 0)), const(cshape)],
        out_shape=[jax.ShapeDtypeStruct((n_rows, ROUTE_LANES), F32), jax.ShapeDtypeStruct(cshape, F32)],
        compiler_params=_cparams(1, VMEM_LIMIT),
        name="moe_router",
    )(x, w, b, counts0)


DMA_UNROLL = 8


def _dispatch_body(pos_ref, x_ref, xs_hbm, sem):
    tile = x_ref.shape[0]

    def body(j, c):
        src = x_ref.at[pl.ds(j, 1)]
        pltpu.make_async_copy(src, xs_hbm.at[pl.ds(pos_ref[0, 0, j], 1)], sem.at[0]).start(priority=0)
        pltpu.make_async_copy(src, xs_hbm.at[pl.ds(pos_ref[0, 0, tile + j], 1)], sem.at[0]).start(priority=1)
        return c
    lax.fori_loop(0, tile, body, 0, unroll=DMA_UNROLL)
    for _ in range(2):
        pltpu.make_async_copy(x_ref, xs_hbm.at[pl.ds(0, tile)], sem.at[0]).wait()


def _dispatch(x, pos, tile):
    m = x.shape[0]
    return pl.pallas_call(
        _dispatch_body,
        grid=(m // tile,),
        in_specs=[pl.BlockSpec((1, 1, 2 * tile), lambda i: (i, 0, 0), memory_space=pltpu.SMEM),
                  pl.BlockSpec((tile, D_MODEL), lambda i: (i, 0))],
        out_specs=pl.BlockSpec(memory_space=pl.ANY),
        out_shape=jax.ShapeDtypeStruct((2 * m, D_MODEL), F32),
        scratch_shapes=[pltpu.SemaphoreType.DMA((1,))],
        compiler_params=_cparams(1),
        name="moe_dispatch",
    )(pos, x)


def _expert_body(it_ref, ie_ref, lo_ref, hi_ref, x_ref, wg_ref, wu_ref, wd_ref, y_ref):
    w = pl.program_id(0)
    tile = it_ref[w]
    lo, hi = lo_ref[w], hi_ref[w]
    first = (w == 0) | (it_ref[jnp.maximum(w - 1, 0)] != tile)

    @pl.when(first)
    def _():
        y_ref[...] = jnp.zeros_like(y_ref)

    @pl.when(lo < hi)
    def _():
        xb = x_ref[...].astype(BF16)
        hg = _dot(xb, wg_ref[...])
        hu = _dot(xb, wu_ref[...])
        he = (hg * jax.nn.sigmoid(hg)) * hu
        y = _dot(he.astype(BF16), wd_ref[...])
        row = tile * EXPERT_ROWS + lax.broadcasted_iota(jnp.int32, (EXPERT_ROWS, 1), 0)
        y_ref[...] += jnp.where((row >= lo) & (row < hi), y, 0.0)


def _experts(xs, items, w_gate, w_up, w_down, layer):
    n_items = items[0].shape[0]
    rows = EXPERT_ROWS
    tspec = pl.BlockSpec((rows, D_MODEL), lambda w, it, ie, lo, hi: (it[w], 0))
    wspec = lambda shp: pl.BlockSpec((None, None) + shp, lambda w, it, ie, lo, hi: (layer, ie[w], 0, 0))
    grid_spec = pltpu.PrefetchScalarGridSpec(
        num_scalar_prefetch=4,
        grid=(n_items,),
        in_specs=[tspec, wspec((D_MODEL, D_EXPERT)), wspec((D_MODEL, D_EXPERT)), wspec((D_EXPERT, D_MODEL))],
        out_specs=tspec)
    return pl.pallas_call(
        _expert_body,
        grid_spec=grid_spec,
        out_shape=jax.ShapeDtypeStruct(xs.shape, F32),
        compiler_params=_cparams(1, VMEM_LIMIT),
        name="moe_experts",
    )(*items, xs, w_gate, w_up, w_down)


def _moe_combine_body(pos_ref, posn_ref, h_ref, r_ref, g_ref, b_ref, ys_hbm, o_ref, ybuf, sem, *, nt):
    t = pl.program_id(0)
    slot = t % 2
    tile = h_ref.shape[0]

    def gather(idx_ref, s):
        def body(j, c):
            for k in range(2):
                jj = j + k * tile
                pltpu.make_async_copy(ys_hbm.at[pl.ds(idx_ref[0, 0, jj], 1)], ybuf.at[s, pl.ds(jj, 1)],
                                      sem.at[s]).start(priority=k)
            return c
        lax.fori_loop(0, tile, body, 0, unroll=DMA_UNROLL)

    @pl.when(t == 0)
    def _():
        gather(pos_ref, 0)

    @pl.when(t + 1 < nt)
    def _():
        gather(posn_ref, 1 - slot)

    pltpu.make_async_copy(ys_hbm.at[pl.ds(0, 2 * tile)], ybuf.at[slot], sem.at[slot]).wait()
    r = r_ref[...]
    moe = r[:, 2:3] * ybuf[slot, 0:tile, :] + r[:, 3:4] * ybuf[slot, tile:2 * tile, :]
    o_ref[...] = _layer_norm_rows(ALPHA * h_ref[...] + moe, g_ref[...], b_ref[...])


def _moe_combine(h, ys, pos, route, g, b, tile, row_off, n_rows):
    ob = row_off // tile
    nt = n_rows // tile
    row = lambda w: pl.BlockSpec((tile, w), lambda i: (i + ob, 0))
    const = lambda s: pl.BlockSpec(s, lambda i: (0, 0))
    smem = lambda f: pl.BlockSpec((1, 1, 2 * tile), f, memory_space=pltpu.SMEM)
    return pl.pallas_call(
        functools.partial(_moe_combine_body, nt=nt),
        grid=(nt,),
        in_specs=[smem(lambda i: (i + ob, 0, 0)), smem(lambda i: (jnp.minimum(i + 1, nt - 1) + ob, 0, 0)),
                  row(D_MODEL), row(ROUTE_LANES), const((1, D_MODEL)), const((1, D_MODEL)),
                  pl.BlockSpec(memory_space=pl.ANY)],
        out_specs=pl.BlockSpec((tile, D_MODEL), lambda i: (i, 0)),
        out_shape=jax.ShapeDtypeStruct((n_rows, D_MODEL), F32),
        scratch_shapes=[pltpu.VMEM((2, 2 * tile, D_MODEL), F32), pltpu.SemaphoreType.DMA((2,))],
        compiler_params=_cparams(1, VMEM_LIMIT),
        name="moe_combine_ln",
    )(pos, pos, h, route, g.reshape(1, D_MODEL), b.reshape(1, D_MODEL), ys)


def _dispatch_plan(route, counts, m, tile):
    rows = EXPERT_ROWS
    nt = (2 * m) // rows
    cnt = counts[0, 0:N_EXPERTS].astype(jnp.int32)
    end = jnp.cumsum(cnt)
    start = end - cnt
    e = route[:, 0:2].astype(jnp.int32)
    rank = route[:, 4:6].astype(jnp.int32)
    onehot = (e[:, :, None] == jnp.arange(N_EXPERTS, dtype=jnp.int32)[None, None, :])
    pos = jnp.sum(jnp.where(onehot, start[None, None, :], 0), axis=2) + rank
    pos = jnp.transpose(pos.reshape(m // tile, tile, 2), (0, 2, 1)).reshape(m // tile, 1, 2 * tile)
    first_t = start // rows
    n_it = jnp.where(cnt > 0, (end - 1) // rows - first_t + 1, 0)
    it_end = jnp.cumsum(n_it)
    n_items = nt + N_EXPERTS
    w = jnp.arange(n_items, dtype=jnp.int32)
    ie = jnp.minimum(jnp.sum((it_end[None, :] <= w[:, None]).astype(jnp.int32), axis=1), N_EXPERTS - 1)
    pick = lambda a: jnp.sum(jnp.where(ie[:, None] == jnp.arange(N_EXPERTS)[None, :], a[None, :], 0), axis=1)
    valid = w < it_end[-1]
    it = pick(first_t) + (w - (pick(it_end) - pick(n_it)))
    it = jnp.where(valid, it, nt - 1).astype(jnp.int32)
    last_e = jnp.sum(jnp.where(w == it_end[-1] - 1, ie, 0))
    ie = jnp.where(valid, ie, last_e).astype(jnp.int32)
    lo = jnp.where(valid, pick(start), 0).astype(jnp.int32)
    hi = jnp.where(valid, pick(end), 0).astype(jnp.int32)
    return pos.astype(jnp.int32), (it, ie, lo, hi)


def _moe_layer(h, n_big, w_grp, b_grp, w_exp, b_exp, w_gate, w_up, w_down, layer, g, b, parts):
    m = h.shape[0]
    tile = SMALL_TILE
    assert m % tile == 0 and (2 * m) % EXPERT_ROWS == 0 and n_big % ROW_TILE == 0
    zero_counts = jnp.zeros((V7X_SUBLANES, ROUTE_LANES), F32)
    route_a, counts = _router(h, 0, n_big, w_grp, b_grp, w_exp, b_exp, ROW_TILE, zero_counts)
    route_b, counts = _router(h, n_big, m - n_big, w_grp, b_grp, w_exp, b_exp, tile, counts)
    route = jnp.concatenate([route_a, route_b], axis=0)
    pos, items = _dispatch_plan(route, counts, m, tile)
    xs = _dispatch(h, pos, tile)
    ys = _experts(xs, items, w_gate, w_up, w_down, layer)
    return [_moe_combine(h, ys, pos, route, g, b, tile, off, n) for off, n in parts]


GATE_LANES = V7X_LANES
GATE_F0 = 2 * V7X_SUBLANES


def _log_sigmoid(x):
    return jnp.minimum(x, 0.0) - jnp.log1p(jnp.exp(-jnp.abs(x)))


def _silu(x):
    return x * jax.nn.sigmoid(x)


def _in_proj_b_body(x_ref, w_ref, cw_ref, bias_ref, p1_ref, p2_ref, p3_ref,
                    q_ref, kt_ref, v_ref, og_ref, gt_ref, gtt_ref, mq_ref, u_ref, carry, *, seq_rows):
    w = MLSTM_WIDTH
    lhs = _lhs(x_ref[...], w_ref.dtype == F32)
    tm = x_ref.shape[0]
    u = _mm(lhs, w_ref[:, 0:2 * w])
    row = lax.broadcasted_iota(jnp.int32, (tm, 1), 0)
    if seq_rows is None:
        @pl.when(pl.program_id(1) == 0)
        def _():
            carry[...] = p1_ref[...]
        uc = jnp.concatenate([carry[...], u], axis=0)
        shifted = [uc[V7X_SUBLANES - k:V7X_SUBLANES - k + tm, :] for k in (1, 2, 3)]
        carry[...] = u[tm - V7X_SUBLANES:, :]
        u_ref[...] = u[tm - (CONV_W - 1):, :]
    else:
        t = row % seq_rows
        prevs = (p1_ref[...], p2_ref[...], p3_ref[...])
        shifted = [jnp.where(t >= k, pltpu.roll(u, k, 0), prevs[k - 1]) for k in (1, 2, 3)]
        u_ref[...] = u
    cw = cw_ref[...]
    y = shifted[2] * cw[0:1, :]
    y = y + shifted[1] * cw[1:2, :]
    y = y + shifted[0] * cw[2:3, :]
    y = y + u * cw[3:4, :]
    qk = _silu(y)
    q_ref[...] = qk[:, 0:w].astype(BF16)
    kt_ref[...] = (qk[:, w:2 * w] * ATTN_SCALE).T.astype(BF16)
    v_ref[...] = _mm(lhs, w_ref[:, 2 * w:3 * w]).astype(BF16)
    og_ref[...] = jax.nn.sigmoid(_mm(lhs, w_ref[:, 3 * w:4 * w])).astype(BF16)
    gates = _mm(lhs, w_ref[:, 4 * w:4 * w + GATE_LANES]) + bias_ref[...]
    lane = lax.broadcasted_iota(jnp.int32, gates.shape, 1)
    is_f = (lane >= GATE_F0) & (lane < GATE_F0 + MLSTM_HEADS)
    gt = jnp.where(is_f, _log_sigmoid(gates), gates)
    gt_ref[...] = gt
    gtt_ref[...] = gt.T
    mq_ref[...] = (_mm(lhs, w_ref[:, 4 * w + GATE_LANES:]) * ATTN_SCALE).astype(mq_ref.dtype)


def _in_proj_b(x, x_off, m, w_bf, conv_w, bias, prevs, n_seq, seq_len, tile, per_tile_seqs):
    w = MLSTM_WIDTH
    tps = max(seq_len // tile, 1)
    ob = x_off // tile
    row = lambda width: pl.BlockSpec((tile, width), lambda b, i: (b * tps + i, 0))
    col = lambda height: pl.BlockSpec((height, tile), lambda b, i: (0, b * tps + i))
    const = lambda s: pl.BlockSpec(s, lambda b, i: (0, 0))
    if per_tile_seqs:
        pspecs = [row(2 * w)] * 3
        u_rows, u_spec = m, row(2 * w)
    else:
        pspecs = [pl.BlockSpec((V7X_SUBLANES, 2 * w), lambda b, i: (b, 0))] * 3
        u_rows = n_seq * (CONV_W - 1)
        u_spec = pl.BlockSpec((None, CONV_W - 1, 2 * w), lambda b, i: (b, 0, 0))
    u_shape = (jax.ShapeDtypeStruct((m, 2 * w), F32) if per_tile_seqs
               else jax.ShapeDtypeStruct((n_seq, CONV_W - 1, 2 * w), F32))
    return pl.pallas_call(
        functools.partial(_in_proj_b_body, seq_rows=seq_len if per_tile_seqs else None),
        grid=(m // (tps * tile), tps),
        in_specs=[pl.BlockSpec((tile, D_MODEL), lambda b, i: (b * tps + i + ob, 0)),
                  const(w_bf.shape), const(conv_w.shape), const(bias.shape)] + pspecs,
        out_specs=[row(w), col(w), row(w), row(w), row(GATE_LANES), col(GATE_LANES), row(MEM_WIDTH), u_spec],
        out_shape=[jax.ShapeDtypeStruct((m, w), BF16), jax.ShapeDtypeStruct((w, m), BF16),
                   jax.ShapeDtypeStruct((m, w), BF16), jax.ShapeDtypeStruct((m, w), BF16),
                   jax.ShapeDtypeStruct((m, GATE_LANES), F32), jax.ShapeDtypeStruct((GATE_LANES, m), F32),
                   jax.ShapeDtypeStruct((m, MEM_WIDTH), BF16), u_shape],
        scratch_shapes=[pltpu.VMEM((V7X_SUBLANES, 2 * w), F32)],
        compiler_params=_cparams(2, VMEM_LIMIT),
        name="in_proj_b",
    )(x, w_bf, conv_w, bias, *prevs)


def _split3_dot(x, sel):
    x1 = x.astype(BF16)
    r1 = x - x1.astype(F32)
    x2 = r1.astype(BF16)
    x3 = (r1 - x2.astype(F32)).astype(BF16)
    return _dot(x1, sel) + (_dot(x2, sel) + _dot(x3, sel))


def _scan_rows(x, op, fill):
    n = x.shape[0]
    row = lax.broadcasted_iota(jnp.int32, (n, 1), 0)
    sh = 1
    while sh < n:
        x = op(x, jnp.where(row >= sh, pltpu.roll(x, sh, 0), fill))
        sh *= 2
    return x


def _mlstm_body(q_ref, kt_ref, v_ref, og_ref, gt_ref, gtt_ref, ng_ref, sel_ref, ln_ref, st0_ref, m0_ref,
                 cell_ref, st_out, m_out, st, m_rows, qk_s):
    c = pl.program_id(1)
    L = q_ref.shape[0]
    E = HEAD_DIM
    H = MLSTM_HEADS
    W = V7X_LANES

    @pl.when(c == 0)
    def _():
        st[...] = st0_ref[...]
        m_rows[...] = m0_ref[...]

    gt = gt_ref[...]
    lane = lax.broadcasted_iota(jnp.int32, (1, W), 1)
    head_lane = (lane >= GATE_F0) & (lane < GATE_F0 + H)
    b_cols = _scan_rows(gt, jnp.add, 0.0)
    r_cols = pltpu.roll(gt, GATE_F0, 1) - b_cols
    cm = _scan_rows(r_cols, jnp.maximum, NEG_INF)
    m_all = m_rows[...]
    sub = lax.broadcasted_iota(jnp.int32, m_all.shape, 0)
    m_lane = jnp.max(jnp.where(lax.broadcasted_iota(jnp.int32, m_all.shape, 1) == sub + GATE_F0, m_all, NEG_INF),
                     axis=0, keepdims=True)
    a_cols = jnp.maximum(m_lane, cm)
    zero = lambda x: jnp.where(head_lane, x, 0.0)
    sel = sel_ref[...]
    a_b = _split3_dot(zero(a_cols), sel)
    wi_b = _split3_dot(zero(jnp.exp(m_lane - a_cols)), sel)
    em_b = _split3_dot(zero(jnp.exp(-(b_cols + a_cols))), sel)

    gtt = gtt_ref[0:2 * GATE_F0, :]
    lane_t = lax.broadcasted_iota(jnp.int32, (1, L), 1)
    b_rows = gtt
    sh = 1
    while sh < L:
        b_rows = b_rows + jnp.where(lane_t >= sh, pltpu.roll(b_rows, sh, 1), 0.0)
        sh *= 2
    r_rows = gtt[0:GATE_F0, :] - b_rows[GATE_F0:2 * GATE_F0, :]
    b_last = b_rows[GATE_F0:2 * GATE_F0, L - 1:L]
    m_prev = m_all[:, 0:1]
    m_new = jnp.maximum(b_last + m_prev, b_last + jnp.max(r_rows, axis=1, keepdims=True))
    decay = jnp.exp(b_last + m_prev - m_new)
    wk_scale = jnp.exp(b_last + r_rows - m_new)
    m_next = jnp.broadcast_to(m_new, m_all.shape)
    m_rows[...] = m_next
    m_out[...] = m_next

    causal = lax.broadcasted_iota(jnp.int32, (L, L), 0) >= lax.broadcasted_iota(jnp.int32, (L, L), 1)
    sub_k = lax.broadcasted_iota(jnp.int32, (W, 1), 0)
    lane_w = lax.broadcasted_iota(jnp.int32, (1, W), 1)
    ones_rhs = jnp.ones((L, W), BF16)

    for h in range(H):
        p, odd = h // 2, h % 2
        own_rows = (sub_k >= E) if odd else (sub_k < E)
        kt_pad = jnp.where(own_rows, kt_ref[p * W:(p + 1) * W, :], jnp.zeros((), BF16))
        s = _dot(q_ref[:, p * W:(p + 1) * W], kt_pad)
        w_intra = jnp.exp(jnp.where(causal, r_rows[h:h + 1, :] - a_b[:, h * W:(h + 1) * W], NEG_INF))
        qk_s[h] = s * w_intra

    ln_sel = ln_ref[...]
    for p in range(H // 2):
        slab = slice(p * W, (p + 1) * W)
        q_pair = q_ref[:, slab]
        v_aug = jnp.concatenate([v_ref[:, slab], ones_rhs], axis=1)
        halves = []
        for odd in range(2):
            h = 2 * p + odd
            qk = qk_s[h].astype(BF16)
            wib = wi_b[:, h * W:(h + 1) * W]
            nd = _dot(qk, v_aug) + jnp.concatenate([wib, wib], axis=1) * _dot(q_pair, st[h].astype(BF16))
            halves.append(nd[:, 0:W] / jnp.maximum(jnp.abs(nd[:, W:2 * W]), em_b[:, h * W:(h + 1) * W]))
        hv = jnp.where(lane_w < E, halves[0], halves[1])
        mu = _split3_dot(hv, ln_sel) * (1.0 / E)
        hc = hv - mu
        var = _split3_dot(hc * hc, ln_sel) * (1.0 / E)
        cell_ref[:, slab] = (og_ref[:, slab].astype(F32) * (hc * lax.rsqrt(var + LN_EPS) * ng_ref[:, slab])
                             ).astype(cell_ref.dtype)

    for h in range(H):
        p, odd = h // 2, h % 2
        rows = slice(odd * E, (odd + 1) * E)
        own_lanes = (lane_w >= E) if odd else (lane_w < E)
        wkt = (kt_ref[h * E:(h + 1) * E, :].astype(F32) * wk_scale[h:h + 1, :]).astype(BF16)
        d = decay[h:h + 1, :]
        v_aug = jnp.concatenate([v_ref[:, p * W:(p + 1) * W], ones_rhs], axis=1)
        own2 = jnp.concatenate([own_lanes, own_lanes], axis=1)
        st[h, rows, :] = d * st[h, rows, :] + jnp.where(own2, _dot(wkt, v_aug), 0.0)
    st_out[...] = st[...]


def _mlstm_selectors():
    h_of_col = jnp.arange(MLSTM_HEADS * V7X_LANES) // V7X_LANES
    sel = (jnp.arange(V7X_LANES)[:, None] == GATE_F0 + h_of_col[None, :]).astype(BF16)
    g = jnp.arange(V7X_LANES) // HEAD_DIM
    ln = (g[:, None] == g[None, :]).astype(BF16)
    return sel, ln


def _mlstm(q, kt, v, og, gt, gtt, norm_g, c0, n0, m0, n_seq, seq_len, chunk):
    w, hh, e, lanes = MLSTM_WIDTH, MLSTM_HEADS, HEAD_DIM, V7X_LANES
    nc = seq_len // chunk
    mrows = 2 * V7X_SUBLANES
    blk = lambda a: jnp.stack([jnp.pad(a[:, h], ((0, 0), ((h % 2) * e, (1 - h % 2) * e), ((h % 2) * e, (1 - h % 2) * e)))
                               for h in range(hh)], axis=1)
    unblk = lambda s: jnp.stack([s[:, h, (h % 2) * e:(h % 2 + 1) * e, (h % 2) * e:(h % 2 + 1) * e]
                                 for h in range(hh)], axis=1)
    st0 = jnp.concatenate([blk(c0), blk(jnp.broadcast_to(n0[..., None], (n_seq, hh, e, e)))], axis=-1)
    m0b = jnp.pad(jnp.broadcast_to(m0[..., None], (n_seq, hh, lanes)), ((0, 0), (0, mrows - hh), (0, 0)))
    sel, ln = _mlstm_selectors()
    row = lambda width: pl.BlockSpec((chunk, width), lambda b, c: (b * nc + c, 0))
    col = lambda height: pl.BlockSpec((height, chunk), lambda b, c: (0, b * nc + c))
    const = lambda a: pl.BlockSpec(a.shape, lambda b, c: (0,) * a.ndim)
    st = lambda shp: pl.BlockSpec((None,) + shp, lambda b, c: (b,) + (0,) * len(shp))
    slab = (hh, lanes, 2 * lanes)
    cell, st_out, m_out = pl.pallas_call(
        _mlstm_body,
        grid=(n_seq, nc),
        in_specs=[row(w), col(w), row(w), row(w), row(GATE_LANES), col(GATE_LANES),
                  pl.BlockSpec((1, w), lambda b, c: (0, 0)), const(sel), const(ln),
                  st(slab), st((mrows, lanes))],
        out_specs=[row(w), st(slab), st((mrows, lanes))],
        out_shape=[jax.ShapeDtypeStruct((n_seq * seq_len, w), BF16),
                   jax.ShapeDtypeStruct((n_seq,) + slab, F32),
                   jax.ShapeDtypeStruct((n_seq, mrows, lanes), F32)],
        scratch_shapes=[pltpu.VMEM(slab, F32), pltpu.VMEM((mrows, lanes), F32),
                        pltpu.VMEM((hh, chunk, chunk), F32)],
        compiler_params=_cparams(2, VMEM_LIMIT),
        name="mlstm_chunks",
    )(q, kt, v, og, gt, gtt, norm_g.reshape(1, w), sel, ln, st0, m0b)
    return cell, unblk(st_out[..., 0:lanes]), unblk(st_out[..., lanes:])[..., 0], m_out[:, 0:hh, 0]


def _w_in_b_regroup(w_in, b_if):
    w4 = 4 * MLSTM_WIDTH
    hh = MLSTM_HEADS
    z = lambda n: jnp.zeros((D_MODEL, n), w_in.dtype)
    w = jnp.concatenate([w_in[:, :w4 + hh], z(GATE_F0 - hh), w_in[:, w4 + hh:w4 + 2 * hh],
                         z(GATE_LANES - GATE_F0 - hh), w_in[:, w4 + 2 * hh:]], axis=1)
    zb = lambda n: jnp.zeros((n,), F32)
    bias = jnp.concatenate([b_if[0].astype(F32), zb(GATE_F0 - hh), b_if[1].astype(F32),
                            zb(GATE_LANES - GATE_F0 - hh)]).reshape(1, GATE_LANES)
    return w, bias


def _layer_b_prompt(h_all, w_bf, bias, conv_w, norm_g, w_out, mem_prompt, w_mem, ln_g, ln_b, n_seq, seq_len,
                    tail):
    m = n_seq * seq_len
    tile = min(ROW_TILE, seq_len)
    chunk = min(MLSTM_CHUNK, seq_len)
    zstate = jnp.zeros((n_seq * V7X_SUBLANES, 2 * MLSTM_WIDTH), F32)
    q, kt, v, og, gt, gtt, mq, conv = _in_proj_b(h_all, 0, m, w_bf, conv_w, bias, [zstate] * 3, n_seq, seq_len,
                                                 tile, False)
    hh, e = MLSTM_HEADS, HEAD_DIM
    cell, c_out, n_out, m_out = _mlstm(q, kt, v, og, gt, gtt, norm_g, jnp.zeros((n_seq, hh, e, e), F32),
                                       jnp.zeros((n_seq, hh, e), F32), jnp.zeros((n_seq, hh), F32),
                                       n_seq, seq_len, chunk)
    mem_kv = _matmul(mem_prompt.reshape(n_seq * N_MEM, D_MODEL), w_mem, N_MEM)
    mo = _mem_attn(mq, mem_kv.reshape(n_seq, N_MEM, 2 * MEM_WIDTH), seq_len, tile)
    h1 = _out_proj(cell, mo, h_all, 0, w_out.astype(BF16), ln_g, ln_b, tile, tail)
    return h1, conv, c_out, n_out.reshape(n_seq, hh, e), m_out.reshape(n_seq, hh), mem_kv


def _layer_b_sample(h_all, h_off, w_bf, bias, conv_w, norm_g, w_out, mem_kv, conv_state, c0, n0, m0, ln_g, ln_b,
                    n_b, n_new):
    m = n_b * n_new
    w2 = 2 * MLSTM_WIDTH
    hh, e = MLSTM_HEADS, HEAD_DIM
    chunk = MLSTM_CHUNK
    t = jnp.arange(n_new)
    prevs = []
    for kk in (1, 2, 3):
        idx = jnp.clip(CONV_W - 1 - kk + t, 0, CONV_W - 2)
        prevs.append(conv_state[:, idx, :].reshape(m, w2))
    q, kt, v, og, gt, gtt, mq, u = _in_proj_b(h_all, h_off, m, w_bf, conv_w, bias, prevs, n_b, n_new, m, True)
    conv_new = jnp.concatenate([conv_state, u.reshape(n_b, n_new, w2)], axis=1)[:, n_new:]
    npad = chunk - n_new
    pad3 = lambda a: jnp.pad(a.reshape(n_b, n_new, a.shape[1]), ((0, 0), (0, npad), (0, 0)))
    gate_pad = jnp.where(jnp.arange(GATE_LANES) < hh, NEG_INF, 0.0).astype(F32)
    gtp = jnp.concatenate([gt.reshape(n_b, n_new, GATE_LANES),
                           jnp.broadcast_to(gate_pad, (n_b, npad, GATE_LANES))], axis=1)
    gttp = jnp.concatenate([gtt.reshape(GATE_LANES, n_b, n_new),
                            jnp.broadcast_to(gate_pad[:, None, None], (GATE_LANES, n_b, npad))], axis=2)
    ktp = jnp.pad(kt.reshape(MLSTM_WIDTH, n_b, n_new), ((0, 0), (0, 0), (0, npad)))
    flat = lambda a: a.reshape(n_b * chunk, a.shape[2])
    flat_t = lambda a: a.reshape(a.shape[0], n_b * chunk)
    cell, c_out, n_out, m_out = _mlstm(flat(pad3(q)), flat_t(ktp), flat(pad3(v)), flat(pad3(og)), flat(gtp),
                                       flat_t(gttp), norm_g, c0, n0, m0, n_b, chunk, chunk)
    cell = cell.reshape(n_b, chunk, MLSTM_WIDTH)[:, :n_new].reshape(m, MLSTM_WIDTH)
    mo = _sample_mem_attn(mq.astype(F32).reshape(n_b, n_new, MEM_WIDTH),
                          mem_kv.reshape(n_b, N_MEM, 2 * MEM_WIDTH), n_b, n_new)
    h1 = _out_proj(cell, mo.reshape(m, MEM_WIDTH), h_all, h_off, w_out, ln_g, ln_b, m)
    return h1, conv_new, c_out, n_out.reshape(n_b, hh, e), m_out.reshape(n_b, hh)


def _layer_a_sample(h2d, w_in, w_out, mem_kv, caches, ln_g, ln_b, n_b, n_new):
    m = n_b * n_new
    pos = PAST_LEN + (jnp.arange(m, dtype=jnp.int32) % n_new)
    outs = _in_proj_a(h2d, w_in, _rope_tables(pos), n_new, n_b, m, full_kv=True)
    qs, ks, vs, mq, kvs = outs[0:3], outs[3:6], outs[6:9], outs[9], outs[10:13]
    f3 = lambda t: t.reshape(n_b, n_new, t.shape[1])
    caches = [c.reshape(n_b, c.shape[1], 2 * DIL_WIDTH) for c in caches]
    mix = _sample_attn([f3(t) for t in qs], [f3(t) for t in ks], [f3(t) for t in vs], caches, n_b, n_new)
    mo = _sample_mem_attn(f3(mq), mem_kv.reshape(n_b, N_MEM, 2 * MEM_WIDTH), n_b, n_new)
    h1 = _out_proj(mix.reshape(m, DIL_WIDTH), mo.reshape(m, MEM_WIDTH), h2d, 0, w_out, ln_g, ln_b, m)
    return h1, kvs


def _layer_a_prompt(h2d, mem_prompt, w_in, w_out, w_mem, ln_g, ln_b, n_seq, seq_len, tail):
    tile = min(ROW_TILE, seq_len)
    tabs = _rope_tables(jnp.arange(seq_len, dtype=jnp.int32))
    outs = _in_proj_a(h2d, w_in.astype(BF16), tabs, seq_len, n_seq, tile, full_kv=False)
    qs, ks, vs, mq, kvs = outs[0:3], outs[3:6], outs[6:9], outs[9], outs[10:13]
    mem_kv = _matmul(mem_prompt.reshape(n_seq * N_MEM, D_MODEL), w_mem, N_MEM)
    os_, lses = [], []
    for g, (win, dil) in enumerate(DIL_PAIRS):
        o, lse = _dil_attn(qs[g], ks[g], vs[g], n_seq, seq_len, dil, win // dil)
        os_.append(o)
        lses.append(lse)
    mix = _group_mix(os_, lses, tile)
    mo = _mem_attn(mq, mem_kv.reshape(n_seq, N_MEM, 2 * MEM_WIDTH), seq_len, tile)
    h1 = _out_proj(mix, mo, h2d, 0, w_out.astype(BF16), ln_g, ln_b, tile, tail)
    return h1, kvs, mem_kv


def kernel(x_prompt, x_sample, mem_prompt, cache_win1_kv, cache_win2_kv, cache_win3_kv, cache_mem_kv,
           state_mlstm_conv, state_mlstm_C, state_mlstm_n, state_mlstm_m,
           w_in_A, w_out_A, w_in_B, mlstm_conv_w, mlstm_b_if, mlstm_norm_g, w_out_B,
           w_mem_kv, ln_g, ln_b, w_grp, b_grp, w_exp, b_exp, w_gate, w_up, w_down):
    n_p, seq, d = x_prompt.shape
    n_s, t_new, _ = x_sample.shape
    assert d == D_MODEL and w_in_A.shape[0] == 1 and w_in_B.shape[0] == 1
    mp, ms = n_p * seq, n_s * t_new
    m_all = mp + ms
    assert mp % ROW_TILE == 0 and ms % SMALL_TILE == 0 and mp % ms == 0
    xp = x_prompt.reshape(mp, d)
    xs = x_sample.reshape(ms, d)

    wg_bf, wu_bf, wd_bf = w_gate.astype(BF16), w_up.astype(BF16), w_down.astype(BF16)

    def moe(h, i, parts):
        return _moe_layer(h, mp, w_grp[i], b_grp[i], w_exp[i], b_exp[i], wg_bf, wu_bf, wd_bf, i,
                          ln_g[i, 1], ln_b[i, 1], parts)

    caches = (cache_win1_kv[0], cache_win2_kv[0], cache_win3_kv[0])
    hs, kv_s = _layer_a_sample(xs, w_in_A[0], w_out_A[0], cache_mem_kv[0], caches, ln_g[0, 0], ln_b[0, 0],
                               n_s, t_new)
    h, kv_p, mem_kv0 = _layer_a_prompt(xp, mem_prompt, w_in_A[0], w_out_A[0], w_mem_kv[0], ln_g[0, 0], ln_b[0, 0],
                                       n_p, seq, hs)
    (h,) = moe(h, 0, [(0, m_all)])

    w_b, bias = _w_in_b_regroup(w_in_B[0], mlstm_b_if[0])
    w_bf = w_b.astype(BF16)
    hs, conv_s, c_s, nv_s, mm_s = _layer_b_sample(
        h, mp, w_b, bias, mlstm_conv_w[0], mlstm_norm_g[0], w_out_B[0], cache_mem_kv[1],
        state_mlstm_conv[0], state_mlstm_C[0], state_mlstm_n[0], state_mlstm_m[0],
        ln_g[1, 0], ln_b[1, 0], n_s, t_new)
    h1, conv_p, c_p, nv_p, mm_p, mem_kv1 = _layer_b_prompt(
        h, w_bf, bias, mlstm_conv_w[0], mlstm_norm_g[0], w_out_B[0], mem_prompt, w_mem_kv[1],
        ln_g[1, 0], ln_b[1, 0], n_p, seq, hs)
    yp, ys = moe(h1, 1, [(0, mp), (mp, ms)])

    kv_shape = (2, HEADS_PER_GROUP, HEAD_DIM)
    wins = []
    for g in range(N_DIL_GROUPS):
        wins.append(kv_p[g].reshape((1, n_p, -1) + kv_shape))
        wins.append(kv_s[g].reshape((1, n_s, t_new) + kv_shape))
    mem_kv_p = jnp.stack([mem_kv0, mem_kv1]).reshape((2, n_p, N_MEM, 2, MEM_HEADS, HEAD_DIM))
    return (yp.reshape(n_p, seq, d), ys.reshape(n_s, t_new, d), *wins,
            conv_p[None], conv_s[None], c_p[None], c_s[None], nv_p[None], nv_s[None], mm_p[None], mm_s[None],
            mem_kv_p)
```

```python
import functools

import jax
import jax.numpy as jnp
import numpy as np
from jax import lax
from jax.experimental import pallas as pl
from jax.experimental.pallas import tpu as pltpu

D_MODEL = 1024
HEAD_DIM = 64
ATTN_SCALE = HEAD_DIM ** -0.5
PAST_LEN = 16384
N_MEM = 256
MEM_HEADS = 4
MEM_WIDTH = MEM_HEADS * HEAD_DIM
DIL_PAIRS = ((128, 1), (512, 4), (2048, 16))
N_DIL_GROUPS = len(DIL_PAIRS)
HEADS_PER_GROUP = 4
DIL_WIDTH = HEADS_PER_GROUP * HEAD_DIM
QBLK = 128
ROPE_THETA = 500000.0
ROPE_DIMS = HEAD_DIM // 4
MLSTM_HEADS = 12
MLSTM_WIDTH = MLSTM_HEADS * HEAD_DIM
CONV_W = 4
N_GROUPS = 4
EXPERTS_PER_GROUP = 8
N_EXPERTS = N_GROUPS * EXPERTS_PER_GROUP
D_EXPERT = 256
DEPTH = 2
ALPHA = (2 * DEPTH) ** 0.25
LN_EPS = 1e-5

V7X_LANES = 128
V7X_SUBLANES = 8
V7X_VMEM_BYTES = 64 * 1024 * 1024
VMEM_LIMIT = 48 * 1024 * 1024

ROW_TILE = 512
SMALL_TILE = 128
MLSTM_CHUNK = 128
EXPERT_ROWS = 256

BF16 = jnp.bfloat16
F32 = jnp.float32
NEG_INF = float("-inf")


def _cparams(n_axes, vmem=None, flags=None):
    return pltpu.CompilerParams(dimension_semantics=("arbitrary",) * n_axes,
                                vmem_limit_bytes=vmem, flags=flags)


def _nt_dot(a, b):
    return lax.dot_general(a, b, (((1,), (1,)), ((), ())), preferred_element_type=F32)


def _dot(a, b):
    return jnp.dot(a, b, preferred_element_type=F32)


def _split_bf16(x):
    hi = x.astype(BF16)
    lo = (x - hi.astype(F32)).astype(BF16)
    return hi, lo


def _lhs(x, precise):
    return _split_bf16(x) if precise else (x.astype(BF16),)


def _mm(lhs, w):
    if len(lhs) == 1:
        return _dot(lhs[0], w)
    wh, wl = _split_bf16(w)
    return _dot(lhs[0], wh) + (_dot(lhs[0], wl) + _dot(lhs[1], wh))


def _rope_tables(pos):
    half = ROPE_DIMS // 2
    inv = jnp.power(ROPE_THETA, -jnp.arange(half, dtype=F32) * (2.0 / ROPE_DIMS))
    ang = pos.astype(F32)[:, None] * inv[None, :]
    cos, sin = jnp.cos(ang), jnp.sin(ang)
    n = pos.shape[0]
    one = jnp.ones((n, HEAD_DIM - ROPE_DIMS), F32)
    zero8 = jnp.zeros((n, half), F32)
    zrest = jnp.zeros((n, HEAD_DIM - ROPE_DIMS), F32)
    a = jnp.concatenate([cos, cos, one], axis=1)
    b = jnp.concatenate([zero8, sin, zrest], axis=1)
    c = jnp.concatenate([-sin, zero8, zrest], axis=1)
    rep = V7X_LANES // HEAD_DIM
    return jnp.tile(a, (1, rep)), jnp.tile(b, (1, rep)), jnp.tile(c, (1, rep))


def _rope_apply(x, ra, rb, rc):
    parts = []
    for s in range(x.shape[1] // V7X_LANES):
        v = x[:, s * V7X_LANES:(s + 1) * V7X_LANES]
        parts.append(v * ra + pltpu.roll(v, ROPE_DIMS // 2, 1) * rb
                     + pltpu.roll(v, V7X_LANES - ROPE_DIMS // 2, 1) * rc)
    return jnp.concatenate(parts, axis=1)


def _in_proj_a_body(x_ref, w_ref, ra_ref, rb_ref, rc_ref, *outs, kv_rows, precise):
    q_refs, k_refs, v_refs = outs[0:3], outs[3:6], outs[6:9]
    mq_ref = outs[9]
    kv_refs = outs[10:13]
    lhs = _lhs(x_ref[...], precise)
    ra, rb, rc = ra_ref[...], rb_ref[...], rc_ref[...]
    tm = x_ref.shape[0]
    gw = DIL_WIDTH
    for g in range(N_DIL_GROUPS):
        q = _rope_apply(_mm(lhs, w_ref[:, g * gw:(g + 1) * gw]), ra, rb, rc)
        k = _rope_apply(_mm(lhs, w_ref[:, (3 + g) * gw:(4 + g) * gw]), ra, rb, rc)
        v = _mm(lhs, w_ref[:, (6 + g) * gw:(7 + g) * gw])
        q_refs[g][...] = (q * ATTN_SCALE).astype(q_refs[g].dtype)
        k_refs[g][...] = k.astype(k_refs[g].dtype)
        v_refs[g][...] = v.astype(v_refs[g].dtype)
        r = kv_rows[g]
        kv_refs[g][:, 0:gw] = k[tm - r:, :]
        kv_refs[g][:, gw:2 * gw] = v[tm - r:, :]
    mq = _mm(lhs, w_ref[:, 9 * gw:9 * gw + MEM_WIDTH])
    mq_ref[...] = (mq * ATTN_SCALE).astype(mq_ref.dtype)


def _in_proj_a(x2d, w, tabs, seq_len, n_seq, tile, full_kv):
    w_bf = w
    act = F32 if full_kv else BF16
    m = x2d.shape[0]
    nt = m // tile
    gw = DIL_WIDTH
    row_spec = lambda w: pl.BlockSpec((tile, w), lambda i: (i, 0))
    if full_kv:
        tab_spec = pl.BlockSpec((tile, V7X_LANES), lambda i: (i, 0))
        kv_rows = (tile,) * 3
        kv_shapes = [jax.ShapeDtypeStruct((m, 2 * gw), F32)] * 3
        kv_specs = [row_spec(2 * gw)] * 3
    else:
        tps = seq_len // tile
        tab_spec = pl.BlockSpec((tile, V7X_LANES), lambda i: (i % tps, 0))
        kv_rows, kv_shapes, kv_specs = [], [], []
        for win, _ in DIL_PAIRS:
            wb = min(win, seq_len)
            r = min(wb, tile)
            nblk = wb // r
            kv_rows.append(r)
            kv_shapes.append(jax.ShapeDtypeStruct((n_seq * wb, 2 * gw), F32))
            kv_specs.append(pl.BlockSpec(
                (r, 2 * gw),
                lambda i, nblk=nblk: ((i // tps) * nblk + jnp.maximum(i % tps - (tps - nblk), 0), 0)))
        kv_rows = tuple(kv_rows)
    out_shapes = [jax.ShapeDtypeStruct((m, gw), act)] * 9 + [jax.ShapeDtypeStruct((m, MEM_WIDTH), act)] + kv_shapes
    out_specs = [row_spec(gw)] * 9 + [row_spec(MEM_WIDTH)] + kv_specs
    return pl.pallas_call(
        functools.partial(_in_proj_a_body, kv_rows=kv_rows, precise=full_kv),
        grid=(nt,),
        in_specs=[row_spec(D_MODEL), pl.BlockSpec(w_bf.shape, lambda i: (0, 0)), tab_spec, tab_spec, tab_spec],
        out_specs=out_specs,
        out_shape=out_shapes,
        compiler_params=_cparams(1, VMEM_LIMIT),
        name="in_proj_a",
    )(x2d, w_bf, *tabs)


def _dil_attn_body(q_ref, k_ref, v_ref, kp_ref, vp_ref, o_ref, lse_ref, *, span):
    i = pl.program_id(2)
    q, k, v = q_ref[...], k_ref[...], v_ref[...]
    kp, vp = kp_ref[...], vp_ref[...]
    qi = lax.broadcasted_iota(jnp.int32, (QBLK, 2 * QBLK), 0) + QBLK
    ki = lax.broadcasted_iota(jnp.int32, (QBLK, 2 * QBLK), 1)
    band = (qi >= ki) & (qi - ki <= span) & ((i > 0) | (ki >= QBLK))
    lses = []
    for h in range(HEADS_PER_GROUP):
        hs = slice(h * HEAD_DIM, (h + 1) * HEAD_DIM)
        kc = jnp.concatenate([kp[:, hs], k[:, hs]], axis=0)
        vc = jnp.concatenate([vp[:, hs], v[:, hs]], axis=0)
        s = jnp.where(band, _nt_dot(q[:, hs], kc), NEG_INF)
        m = jnp.max(s, axis=1, keepdims=True)
        p = jnp.exp(s - m)
        den = jnp.sum(p, axis=1, keepdims=True)
        o_ref[:, hs] = _dot((p / den).astype(BF16), vc).astype(o_ref.dtype)
        lses.append(m + jnp.log(den))
    lse_ref[...] = jnp.concatenate(lses, axis=1)


def _dil_attn(q, k, v, n_seq, seq_len, dil, span):
    gw = DIL_WIDTH
    L = seq_len // dil
    nb = L // QBLK
    view = lambda t: t.reshape(n_seq, L, dil * gw)
    blk = pl.BlockSpec((None, QBLK, gw), lambda b, r, i: (b, i, r))
    prev = pl.BlockSpec((None, QBLK, gw), lambda b, r, i: (b, jnp.maximum(i - 1, 0), r))
    o, lse = pl.pallas_call(
        functools.partial(_dil_attn_body, span=span),
        grid=(n_seq, dil, nb),
        in_specs=[blk, blk, blk, prev, prev],
        out_specs=[blk, pl.BlockSpec((None, None, QBLK, HEADS_PER_GROUP), lambda b, r, i: (b, r, i, 0))],
        out_shape=[jax.ShapeDtypeStruct((n_seq, L, dil * gw), BF16),
                   jax.ShapeDtypeStruct((n_seq, dil, L, HEADS_PER_GROUP), F32)],
        compiler_params=_cparams(3),
        name=f"dil_attn_d{dil}",
    )(view(q), view(k), view(v), view(k), view(v))
    o = o.reshape(n_seq * seq_len, gw)
    lse = jnp.transpose(lse, (0, 2, 1, 3)).reshape(n_seq * seq_len, HEADS_PER_GROUP)
    return o, lse


def _group_mix_body(o1, o2, o3, l1, l2, l3, mix_ref):
    os_ = (o1, o2, o3)
    ls = (l1[...], l2[...], l3[...])
    for h in range(HEADS_PER_GROUP):
        hs = slice(h * HEAD_DIM, (h + 1) * HEAD_DIM)
        lh = [l[:, h:h + 1] for l in ls]
        mx = jnp.maximum(jnp.maximum(lh[0], lh[1]), lh[2])
        e = [jnp.exp(x - mx) for x in lh]
        tot = e[0] + e[1] + e[2]
        acc = (e[0] / tot) * os_[0][:, hs].astype(F32)
        acc = acc + (e[1] / tot) * os_[1][:, hs].astype(F32)
        acc = acc + (e[2] / tot) * os_[2][:, hs].astype(F32)
        mix_ref[:, hs] = acc.astype(mix_ref.dtype)


def _group_mix(os_, lses, tile):
    m = os_[0].shape[0]
    gw = DIL_WIDTH
    ospec = pl.BlockSpec((tile, gw), lambda i: (i, 0))
    lspec = pl.BlockSpec((tile, HEADS_PER_GROUP), lambda i: (i, 0))
    return pl.pallas_call(
        _group_mix_body,
        grid=(m // tile,),
        in_specs=[ospec] * 3 + [lspec] * 3,
        out_specs=ospec,
        out_shape=jax.ShapeDtypeStruct((m, gw), BF16),
        compiler_params=_cparams(1),
        name="group_mix",
    )(*os_, *lses)


def _mem_attn_body(q_ref, kv_ref, o_ref):
    q = q_ref[...]
    kv = kv_ref[...].astype(BF16)
    for h in range(MEM_HEADS):
        hs = slice(h * HEAD_DIM, (h + 1) * HEAD_DIM)
        vs = slice(MEM_WIDTH + h * HEAD_DIM, MEM_WIDTH + (h + 1) * HEAD_DIM)
        s = _nt_dot(q[:, hs], kv[:, hs])
        m = jnp.max(s, axis=1, keepdims=True)
        p = jnp.exp(s - m)
        den = jnp.sum(p, axis=1, keepdims=True)
        o_ref[:, hs] = _dot((p / den).astype(BF16), kv[:, vs]).astype(o_ref.dtype)


def _mem_attn(q, mem_kv, seq_len, tile):
    m = q.shape[0]
    tps = seq_len // tile
    return pl.pallas_call(
        _mem_attn_body,
        grid=(m // tile,),
        in_specs=[pl.BlockSpec((tile, MEM_WIDTH), lambda i: (i, 0)),
                  pl.BlockSpec((None, N_MEM, 2 * MEM_WIDTH), lambda i: (i // tps, 0, 0))],
        out_specs=pl.BlockSpec((tile, MEM_WIDTH), lambda i: (i, 0)),
        out_shape=jax.ShapeDtypeStruct((m, MEM_WIDTH), BF16),
        compiler_params=_cparams(1),
        name="mem_attn",
    )(q, mem_kv)


def _layer_norm_rows(x, g, b):
    mu = jnp.mean(x, axis=1, keepdims=True)
    xc = x - mu
    var = jnp.mean(xc * xc, axis=1, keepdims=True)
    return xc * lax.rsqrt(var + LN_EPS) * g + b


def _out_proj_body(a_ref, mo_ref, h_ref, w_ref, g_ref, b_ref, *rest, n_tiles, precise):
    o_ref = rest[-1]
    ka = a_ref.shape[1]

    def rows():
        y = _mm(_lhs(a_ref[...], precise), w_ref[0:ka, :]) + _mm(_lhs(mo_ref[...], precise), w_ref[ka:, :])
        o_ref[...] = _layer_norm_rows(ALPHA * h_ref[...] + y, g_ref[...], b_ref[...])

    if len(rest) == 1:
        rows()
    else:
        tail_ref = rest[0]
        pl.when(pl.program_id(0) < n_tiles)(rows)

        @pl.when(pl.program_id(0) == n_tiles)
        def _():
            o_ref[0:tail_ref.shape[0], :] = tail_ref[...]


def _out_proj(a, mo, h, h_off, w, g, b, tile, tail=None):
    w_bf = w
    m, ka = a.shape
    nt = m // tile
    last = nt - 1
    row = lambda w, off=0: pl.BlockSpec((tile, w), lambda i: (jnp.minimum(i, last) + off // tile, 0))
    const = lambda s: pl.BlockSpec(s, lambda i: (0, 0))
    in_specs = [row(ka), row(MEM_WIDTH), row(D_MODEL, h_off), const(w_bf.shape), const((1, D_MODEL)),
                const((1, D_MODEL))]
    args = [a, mo, h, w_bf, g.reshape(1, D_MODEL), b.reshape(1, D_MODEL)]
    out_rows, steps = m, nt
    if tail is not None:
        assert tail.shape[0] <= tile
        in_specs.append(const(tail.shape))
        args.append(tail)
        out_rows, steps = m + tail.shape[0], nt + 1
    return pl.pallas_call(
        functools.partial(_out_proj_body, n_tiles=nt, precise=(w.dtype == F32)),
        grid=(steps,),
        in_specs=in_specs,
        out_specs=pl.BlockSpec((tile, D_MODEL), lambda i: (i, 0)),
        out_shape=jax.ShapeDtypeStruct((out_rows, D_MODEL), F32),
        compiler_params=_cparams(1, VMEM_LIMIT),
        name="out_proj_ln",
    )(*args)


def _matmul_body(x_ref, w_ref, o_ref):
    o_ref[...] = _dot(x_ref[...].astype(BF16), w_ref[...].astype(BF16))


def _matmul(x, w, tile):
    m, k = x.shape
    n = w.shape[1]
    return pl.pallas_call(
        _matmul_body,
        grid=(m // tile,),
        in_specs=[pl.BlockSpec((tile, k), lambda i: (i, 0)), pl.BlockSpec((k, n), lambda i: (0, 0))],
        out_specs=pl.BlockSpec((tile, n), lambda i: (i, 0)),
        out_shape=jax.ShapeDtypeStruct((m, n), F32),
        compiler_params=_cparams(1),
        name="mem_kv_proj",
    )(x, w)


def _col_attend(q_row, kmat, vmat, kmask, knew, vnew, nmask):
    pk = kmat * q_row
    pn = None if knew is None else knew * q_row
    outs, lses = [], []
    for h in range(HEADS_PER_GROUP):
        hs = slice(h * HEAD_DIM, (h + 1) * HEAD_DIM)
        s = jnp.sum(pk[:, hs], axis=1, keepdims=True)
        if kmask is not None:
            s = jnp.where(kmask, s, NEG_INF)
        m = jnp.max(s, axis=0, keepdims=True)
        if pn is not None:
            sn = jnp.where(nmask, jnp.sum(pn[:, hs], axis=1, keepdims=True), NEG_INF)
            m = jnp.maximum(m, jnp.max(sn, axis=0, keepdims=True))
        p = jnp.exp(s - m)
        den = jnp.sum(p, axis=0, keepdims=True)
        acc = jnp.sum(p * vmat[:, hs], axis=0, keepdims=True)
        if pn is not None:
            pnw = jnp.exp(sn - m)
            den = den + jnp.sum(pnw, axis=0, keepdims=True)
            acc = acc + jnp.sum(pnw * vnew[:, hs], axis=0, keepdims=True)
        outs.append(acc / den)
        lses.append(m + jnp.log(den))
    return outs, lses


def _sample_attn_body(q1, q2, q3, kn1, kn2, kn3, vn1, vn2, vn3, c1, c2, c3, mix_ref, *, n_new):
    qs = (q1[...], q2[...], q3[...])
    kns = (kn1[...], kn2[...], kn3[...])
    vns = (vn1[...], vn2[...], vn3[...])
    gw = DIL_WIDTH
    t_idx = lax.broadcasted_iota(jnp.int32, (n_new, 1), 0)
    rows = []
    for t in range(n_new):
        per_group = []
        for g, (win, dil) in enumerate(DIL_PAIRS):
            q_row = qs[g][t:t + 1, :]
            cache = (c1, c2, c3)[g]
            if dil == 1:
                kmat, vmat = cache[:, 0:gw], cache[:, gw:2 * gw]
                r_idx = lax.broadcasted_iota(jnp.int32, (kmat.shape[0], 1), 0)
                per_group.append(_col_attend(q_row, kmat, vmat, r_idx >= t, kns[g], vns[g], t_idx <= t))
            else:
                base = t * 2 * gw
                kmat, vmat = cache[:, base:base + gw], cache[:, base + gw:base + 2 * gw]
                per_group.append(_col_attend(q_row, kmat, vmat, None, kns[g], vns[g], t_idx == t))
        heads = []
        for h in range(HEADS_PER_GROUP):
            lh = [per_group[g][1][h] for g in range(N_DIL_GROUPS)]
            mx = jnp.maximum(jnp.maximum(lh[0], lh[1]), lh[2])
            e = [jnp.exp(x - mx) for x in lh]
            tot = e[0] + e[1] + e[2]
            acc = (e[0] / tot) * per_group[0][0][h]
            acc = acc + (e[1] / tot) * per_group[1][0][h]
            acc = acc + (e[2] / tot) * per_group[2][0][h]
            heads.append(acc)
        rows.append(jnp.concatenate(heads, axis=1))
    mix_ref[...] = jnp.concatenate(rows, axis=0)


def _sample_attn(qs, kns, vns, caches, n_b, n_new):
    gw = DIL_WIDTH
    small = pl.BlockSpec((None, n_new, gw), lambda b: (b, 0, 0))
    cviews, cspecs = [], []
    for c, (win, dil) in zip(caches, DIL_PAIRS):
        blocks = win // dil
        cviews.append(c.reshape(n_b, blocks, dil * 2 * gw))
        lanes = min(dil, n_new) * 2 * gw
        cspecs.append(pl.BlockSpec((None, blocks, lanes), lambda b: (b, 0, 0)))
    return pl.pallas_call(
        functools.partial(_sample_attn_body, n_new=n_new),
        grid=(n_b,),
        in_specs=[small] * 9 + cspecs,
        out_specs=small,
        out_shape=jax.ShapeDtypeStruct((n_b, n_new, gw), F32),
        compiler_params=_cparams(1, VMEM_LIMIT),
        name="sample_dil_attn",
    )(*qs, *kns, *vns, *cviews)


def _sample_mem_attn_body(q_ref, kv_ref, o_ref, *, n_new):
    q = q_ref[...]
    kmat, vmat = kv_ref[:, 0:MEM_WIDTH], kv_ref[:, MEM_WIDTH:2 * MEM_WIDTH]
    rows = []
    for t in range(n_new):
        outs, _ = _col_attend(q[t:t + 1, :], kmat, vmat, None, None, None, None)
        rows.append(jnp.concatenate(outs, axis=1))
    o_ref[...] = jnp.concatenate(rows, axis=0)


def _sample_mem_attn(q, mem_kv, n_b, n_new):
    small = pl.BlockSpec((None, n_new, MEM_WIDTH), lambda b: (b, 0, 0))
    return pl.pallas_call(
        functools.partial(_sample_mem_attn_body, n_new=n_new),
        grid=(n_b,),
        in_specs=[small, pl.BlockSpec((None, N_MEM, 2 * MEM_WIDTH), lambda b: (b, 0, 0))],
        out_specs=small,
        out_shape=jax.ShapeDtypeStruct((n_b, n_new, MEM_WIDTH), F32),
        compiler_params=_cparams(1),
        name="sample_mem_attn",
    )(q, mem_kv)


ROUTE_LANES = V7X_LANES


def _router_body(x_ref, w_ref, b_ref, cnt0_ref, o_ref, cnt_ref):
    xh, xl = _split_bf16(x_ref[...])
    wh, wl = _split_bf16(w_ref[...])
    logits = _dot(xh, wh) + (_dot(xh, wl) + _dot(xl, wh)) + b_ref[...]
    lane = lax.broadcasted_iota(jnp.int32, logits.shape, 1)
    big = jnp.int32(ROUTE_LANES)
    is_grp = (lane >= N_EXPERTS) & (lane < N_EXPERTS + N_GROUPS)
    gl = jnp.where(is_grp, logits, NEG_INF)
    gmax = jnp.max(gl, axis=1, keepdims=True)
    gsel = jnp.min(jnp.where(gl == gmax, lane, big), axis=1, keepdims=True) - N_EXPERTS
    gp = 1.0 / jnp.sum(jnp.exp(gl - gmax), axis=1, keepdims=True)
    in_grp = (lane < N_EXPERTS) & ((lane // EXPERTS_PER_GROUP) == gsel)
    el = jnp.where(in_grp, logits, NEG_INF)
    v1 = jnp.max(el, axis=1, keepdims=True)
    i1 = jnp.min(jnp.where(el == v1, lane, big), axis=1, keepdims=True)
    el2 = jnp.where(lane == i1, NEG_INF, el)
    v2 = jnp.max(el2, axis=1, keepdims=True)
    i2 = jnp.min(jnp.where(el2 == v2, lane, big), axis=1, keepdims=True)
    e2 = jnp.exp(v2 - v1)
    w1 = (1.0 / (1.0 + e2)) * gp
    w2 = (e2 / (1.0 + e2)) * gp
    @pl.when(pl.program_id(0) == 0)
    def _():
        cnt_ref[...] = cnt0_ref[...]

    tm = logits.shape[0]
    oh1 = (lane == i1).astype(F32)
    oh2 = (lane == i2).astype(F32)
    tri = (lax.broadcasted_iota(jnp.int32, (tm, tm), 0) > lax.broadcasted_iota(jnp.int32, (tm, tm), 1)).astype(BF16)
    base = cnt_ref[0:1, :]
    c1 = jnp.sum(oh1, axis=0, keepdims=True)
    c2 = jnp.sum(oh2, axis=0, keepdims=True)
    r1 = jnp.sum(oh1 * (_dot(tri, oh1.astype(BF16)) + base), axis=1, keepdims=True)
    r2 = jnp.sum(oh2 * (_dot(tri, oh2.astype(BF16)) + (base + c1)), axis=1, keepdims=True)
    cnt_ref[...] = jnp.broadcast_to(base + c1 + c2, cnt_ref.shape)
    out = jnp.where(lane == 0, i1.astype(F32), 0.0)
    out = jnp.where(lane == 1, i2.astype(F32), out)
    out = jnp.where(lane == 2, w1, out)
    out = jnp.where(lane == 3, w2, out)
    out = jnp.where(lane == 4, r1, out)
    out = jnp.where(lane == 5, r2, out)
    o_ref[...] = out


def _router(x, row_off, n_rows, w_grp, b_grp, w_exp, b_exp, tile, counts0):
    pad = ROUTE_LANES - N_EXPERTS - N_GROUPS
    w = jnp.concatenate([w_exp, w_grp, jnp.zeros((D_MODEL, pad), F32)], axis=1)
    b = jnp.concatenate([b_exp, b_grp, jnp.zeros((pad,), F32)]).reshape(1, ROUTE_LANES)
    ob = row_off // tile
    const = lambda s: pl.BlockSpec(s, lambda i: (0, 0))
    cshape = (V7X_SUBLANES, ROUTE_LANES)
    return pl.pallas_call(
        _router_body,
        grid=(n_rows // tile,),
        in_specs=[pl.BlockSpec((tile, D_MODEL), lambda i: (i + ob, 0)), const(w.shape), const(b.shape), const(cshape)],
        out_specs=[pl.BlockSpec((tile, ROUTE_LANES), lambda i: (i, 0)), const(cshape)],
        out_shape=[jax.ShapeDtypeStruct((n_rows, ROUTE_LANES), F32), jax.ShapeDtypeStruct(cshape, F32)],
        compiler_params=_cparams(1, VMEM_LIMIT),
        name="moe_router",
    )(x, w, b, counts0)


DMA_UNROLL = True


def _dispatch_body(pos_ref, x_ref, xs_hbm, xbuf, sem, *, nt):
    t = pl.program_id(0)
    s = t % 2
    tile = x_ref.shape[0]
    xbuf[s] = x_ref[...]

    def body(j, c):
        src = xbuf.at[s, pl.ds(j, 1)]
        pltpu.make_async_copy(src, xs_hbm.at[pl.ds(pos_ref[0, 0, j], 1)], sem.at[s]).start(priority=0)
        pltpu.make_async_copy(src, xs_hbm.at[pl.ds(pos_ref[0, 0, tile + j], 1)], sem.at[s]).start(priority=1)
        return c
    lax.fori_loop(0, tile, body, 0, unroll=DMA_UNROLL)

    def drain(slot):
        for _ in range(2):
            pltpu.make_async_copy(xbuf.at[slot], xs_hbm.at[pl.ds(0, tile)], sem.at[slot]).wait()

    @pl.when(t >= 1)
    def _():
        drain(1 - s)

    @pl.when(t == nt - 1)
    def _():
        drain(s)


def _dispatch(x, pos, tile):
    m = x.shape[0]
    return pl.pallas_call(
        functools.partial(_dispatch_body, nt=m // tile),
        grid=(m // tile,),
        in_specs=[pl.BlockSpec((1, 1, 2 * tile), lambda i: (i, 0, 0), memory_space=pltpu.SMEM),
                  pl.BlockSpec((tile, D_MODEL), lambda i: (i, 0))],
        out_specs=pl.BlockSpec(memory_space=pl.ANY),
        out_shape=jax.ShapeDtypeStruct((2 * m, D_MODEL), F32),
        scratch_shapes=[pltpu.VMEM((2, tile, D_MODEL), F32), pltpu.SemaphoreType.DMA((2,))],
        compiler_params=_cparams(1),
        name="moe_dispatch",
    )(pos, x)


def _expert_body(it_ref, ie_ref, lo_ref, hi_ref, x_ref, wg_ref, wu_ref, wd_ref, y_ref):
    w = pl.program_id(0)
    tile = it_ref[w]
    lo, hi = lo_ref[w], hi_ref[w]
    first = (w == 0) | (it_ref[jnp.maximum(w - 1, 0)] != tile)

    @pl.when(first)
    def _():
        y_ref[...] = jnp.zeros_like(y_ref)

    @pl.when(lo < hi)
    def _():
        xb = x_ref[...].astype(BF16)
        hg = _dot(xb, wg_ref[...])
        hu = _dot(xb, wu_ref[...])
        he = (hg * jax.nn.sigmoid(hg)) * hu
        y = _dot(he.astype(BF16), wd_ref[...])
        row = tile * EXPERT_ROWS + lax.broadcasted_iota(jnp.int32, (EXPERT_ROWS, 1), 0)
        y_ref[...] += jnp.where((row >= lo) & (row < hi), y, 0.0)


def _experts(xs, items, w_gate, w_up, w_down, layer):
    n_items = items[0].shape[0]
    rows = EXPERT_ROWS
    tspec = pl.BlockSpec((rows, D_MODEL), lambda w, it, ie, lo, hi: (it[w], 0))
    wspec = lambda shp: pl.BlockSpec((None, None) + shp, lambda w, it, ie, lo, hi: (layer, ie[w], 0, 0))
    grid_spec = pltpu.PrefetchScalarGridSpec(
        num_scalar_prefetch=4,
        grid=(n_items,),
        in_specs=[tspec, wspec((D_MODEL, D_EXPERT)), wspec((D_MODEL, D_EXPERT)), wspec((D_EXPERT, D_MODEL))],
        out_specs=tspec)
    return pl.pallas_call(
        _expert_body,
        grid_spec=grid_spec,
        out_shape=jax.ShapeDtypeStruct(xs.shape, F32),
        compiler_params=_cparams(1, VMEM_LIMIT),
        name="moe_experts",
    )(*items, xs, w_gate, w_up, w_down)


def _moe_combine_body(pos_ref, h_ref, r_ref, g_ref, b_ref, ys_hbm, o_ref, ybuf, sem, *, nt):
    t = pl.program_id(0)
    tile = h_ref.shape[0]

    @pl.when(t < nt)
    def _():
        s = t % 2

        def body(j, c):
            for k in range(2):
                jj = j + k * tile
                pltpu.make_async_copy(ys_hbm.at[pl.ds(pos_ref[0, 0, jj], 1)], ybuf.at[s, pl.ds(jj, 1)],
                                      sem.at[s]).start(priority=k)
            return c
        lax.fori_loop(0, tile, body, 0, unroll=DMA_UNROLL)

    @pl.when(t >= 1)
    def _():
        s = (t + 1) % 2
        pltpu.make_async_copy(ys_hbm.at[pl.ds(0, 2 * tile)], ybuf.at[s], sem.at[s]).wait()
        r = r_ref[...]
        moe = r[:, 2:3] * ybuf[s, 0:tile, :] + r[:, 3:4] * ybuf[s, tile:2 * tile, :]
        o_ref[...] = _layer_norm_rows(ALPHA * h_ref[...] + moe, g_ref[...], b_ref[...])


def _moe_combine(h, ys, pos, route, g, b, tile, row_off, n_rows):
    ob = row_off // tile
    nt = n_rows // tile
    done = lambda i: jnp.maximum(i - 1, 0)
    row = lambda w: pl.BlockSpec((tile, w), lambda i: (done(i) + ob, 0))
    const = lambda s: pl.BlockSpec(s, lambda i: (0, 0))
    return pl.pallas_call(
        functools.partial(_moe_combine_body, nt=nt),
        grid=(nt + 1,),
        in_specs=[pl.BlockSpec((1, 1, 2 * tile), lambda i: (jnp.minimum(i, nt - 1) + ob, 0, 0),
                               memory_space=pltpu.SMEM),
                  row(D_MODEL), row(ROUTE_LANES), const((1, D_MODEL)), const((1, D_MODEL)),
                  pl.BlockSpec(memory_space=pl.ANY)],
        out_specs=pl.BlockSpec((tile, D_MODEL), lambda i: (done(i), 0)),
        out_shape=jax.ShapeDtypeStruct((n_rows, D_MODEL), F32),
        scratch_shapes=[pltpu.VMEM((2, 2 * tile, D_MODEL), F32), pltpu.SemaphoreType.DMA((2,))],
        compiler_params=_cparams(1, VMEM_LIMIT),
        name="moe_combine_ln",
    )(pos, h, route, g.reshape(1, D_MODEL), b.reshape(1, D_MODEL), ys)


def _dispatch_plan(route, counts, m, tile):
    rows = EXPERT_ROWS
    nt = (2 * m) // rows
    cnt = counts[0, 0:N_EXPERTS].astype(jnp.int32)
    end = jnp.cumsum(cnt)
    start = end - cnt
    e = route[:, 0:2].astype(jnp.int32)
    rank = route[:, 4:6].astype(jnp.int32)
    onehot = (e[:, :, None] == jnp.arange(N_EXPERTS, dtype=jnp.int32)[None, None, :])
    pos = jnp.sum(jnp.where(onehot, start[None, None, :], 0), axis=2) + rank
    pos = jnp.transpose(pos.reshape(m // tile, tile, 2), (0, 2, 1)).reshape(m // tile, 1, 2 * tile)
    first_t = start // rows
    n_it = jnp.where(cnt > 0, (end - 1) // rows - first_t + 1, 0)
    it_end = jnp.cumsum(n_it)
    n_items = nt + N_EXPERTS
    w = jnp.arange(n_items, dtype=jnp.int32)
    ie = jnp.minimum(jnp.sum((it_end[None, :] <= w[:, None]).astype(jnp.int32), axis=1), N_EXPERTS - 1)
    pick = lambda a: jnp.sum(jnp.where(ie[:, None] == jnp.arange(N_EXPERTS)[None, :], a[None, :], 0), axis=1)
    valid = w < it_end[-1]
    it = pick(first_t) + (w - (pick(it_end) - pick(n_it)))
    it = jnp.where(valid, it, nt - 1).astype(jnp.int32)
    last_e = jnp.sum(jnp.where(w == it_end[-1] - 1, ie, 0))
    ie = jnp.where(valid, ie, last_e).astype(jnp.int32)
    lo = jnp.where(valid, pick(start), 0).astype(jnp.int32)
    hi = jnp.where(valid, pick(end), 0).astype(jnp.int32)
    return pos.astype(jnp.int32), (it, ie, lo, hi)


def _moe_layer(h, n_big, w_grp, b_grp, w_exp, b_exp, w_gate, w_up, w_down, layer, g, b, parts):
    m = h.shape[0]
    tile = SMALL_TILE
    assert m % tile == 0 and (2 * m) % EXPERT_ROWS == 0 and n_big % ROW_TILE == 0
    zero_counts = jnp.zeros((V7X_SUBLANES, ROUTE_LANES), F32)
    route_a, counts = _router(h, 0, n_big, w_grp, b_grp, w_exp, b_exp, ROW_TILE, zero_counts)
    route_b, counts = _router(h, n_big, m - n_big, w_grp, b_grp, w_exp, b_exp, tile, counts)
    route = jnp.concatenate([route_a, route_b], axis=0)
    pos, items = _dispatch_plan(route, counts, m, tile)
    xs = _dispatch(h, pos, tile)
    ys = _experts(xs, items, w_gate, w_up, w_down, layer)
    return [_moe_combine(h, ys, pos, route, g, b, tile, off, n) for off, n in parts]


GATE_LANES = V7X_LANES
GATE_F0 = 2 * V7X_SUBLANES


def _log_sigmoid(x):
    return jnp.minimum(x, 0.0) - jnp.log1p(jnp.exp(-jnp.abs(x)))


def _silu(x):
    return x * jax.nn.sigmoid(x)


def _in_proj_b_body(x_ref, w_ref, cw_ref, bias_ref, p1_ref, p2_ref, p3_ref,
                    q_ref, kt_ref, v_ref, og_ref, gt_ref, gtt_ref, mq_ref, u_ref, carry, *, seq_rows):
    w = MLSTM_WIDTH
    lhs = _lhs(x_ref[...], w_ref.dtype == F32)
    tm = x_ref.shape[0]
    u = _mm(lhs, w_ref[:, 0:2 * w])
    row = lax.broadcasted_iota(jnp.int32, (tm, 1), 0)
    if seq_rows is None:
        @pl.when(pl.program_id(1) == 0)
        def _():
            carry[...] = p1_ref[...]
        uc = jnp.concatenate([carry[...], u], axis=0)
        shifted = [uc[V7X_SUBLANES - k:V7X_SUBLANES - k + tm, :] for k in (1, 2, 3)]
        carry[...] = u[tm - V7X_SUBLANES:, :]
        u_ref[...] = u[tm - (CONV_W - 1):, :]
    else:
        t = row % seq_rows
        prevs = (p1_ref[...], p2_ref[...], p3_ref[...])
        shifted = [jnp.where(t >= k, pltpu.roll(u, k, 0), prevs[k - 1]) for k in (1, 2, 3)]
        u_ref[...] = u
    cw = cw_ref[...]
    y = shifted[2] * cw[0:1, :]
    y = y + shifted[1] * cw[1:2, :]
    y = y + shifted[0] * cw[2:3, :]
    y = y + u * cw[3:4, :]
    qk = _silu(y)
    q_ref[...] = qk[:, 0:w].astype(BF16)
    kt_ref[...] = (qk[:, w:2 * w] * ATTN_SCALE).T.astype(BF16)
    v_ref[...] = _mm(lhs, w_ref[:, 2 * w:3 * w]).astype(BF16)
    og_ref[...] = jax.nn.sigmoid(_mm(lhs, w_ref[:, 3 * w:4 * w])).astype(BF16)
    gates = _mm(lhs, w_ref[:, 4 * w:4 * w + GATE_LANES]) + bias_ref[...]
    lane = lax.broadcasted_iota(jnp.int32, gates.shape, 1)
    is_f = (lane >= GATE_F0) & (lane < GATE_F0 + MLSTM_HEADS)
    gt = jnp.where(is_f, _log_sigmoid(gates), gates)
    gt_ref[...] = gt
    gtt_ref[...] = gt.T
    mq_ref[...] = (_mm(lhs, w_ref[:, 4 * w + GATE_LANES:]) * ATTN_SCALE).astype(mq_ref.dtype)


def _in_proj_b(x, x_off, m, w_bf, conv_w, bias, prevs, n_seq, seq_len, tile, per_tile_seqs):
    w = MLSTM_WIDTH
    tps = max(seq_len // tile, 1)
    ob = x_off // tile
    row = lambda width: pl.BlockSpec((tile, width), lambda b, i: (b * tps + i, 0))
    col = lambda height: pl.BlockSpec((height, tile), lambda b, i: (0, b * tps + i))
    const = lambda s: pl.BlockSpec(s, lambda b, i: (0, 0))
    if per_tile_seqs:
        pspecs = [row(2 * w)] * 3
        u_rows, u_spec = m, row(2 * w)
    else:
        pspecs = [pl.BlockSpec((V7X_SUBLANES, 2 * w), lambda b, i: (b, 0))] * 3
        u_rows = n_seq * (CONV_W - 1)
        u_spec = pl.BlockSpec((None, CONV_W - 1, 2 * w), lambda b, i: (b, 0, 0))
    u_shape = (jax.ShapeDtypeStruct((m, 2 * w), F32) if per_tile_seqs
               else jax.ShapeDtypeStruct((n_seq, CONV_W - 1, 2 * w), F32))
    return pl.pallas_call(
        functools.partial(_in_proj_b_body, seq_rows=seq_len if per_tile_seqs else None),
        grid=(m // (tps * tile), tps),
        in_specs=[pl.BlockSpec((tile, D_MODEL), lambda b, i: (b * tps + i + ob, 0)),
                  const(w_bf.shape), const(conv_w.shape), const(bias.shape)] + pspecs,
        out_specs=[row(w), col(w), row(w), row(w), row(GATE_LANES), col(GATE_LANES), row(MEM_WIDTH), u_spec],
        out_shape=[jax.ShapeDtypeStruct((m, w), BF16), jax.ShapeDtypeStruct((w, m), BF16),
                   jax.ShapeDtypeStruct((m, w), BF16), jax.ShapeDtypeStruct((m, w), BF16),
                   jax.ShapeDtypeStruct((m, GATE_LANES), F32), jax.ShapeDtypeStruct((GATE_LANES, m), F32),
                   jax.ShapeDtypeStruct((m, MEM_WIDTH), BF16), u_shape],
        scratch_shapes=[pltpu.VMEM((V7X_SUBLANES, 2 * w), F32)],
        compiler_params=_cparams(2, VMEM_LIMIT),
        name="in_proj_b",
    )(x, w_bf, conv_w, bias, *prevs)


def _split3_dot(x, sel, terms=3):
    x1 = x.astype(BF16)
    r1 = x - x1.astype(F32)
    x2 = r1.astype(BF16)
    if terms == 2:
        return _dot(x1, sel) + _dot(x2, sel)
    x3 = (r1 - x2.astype(F32)).astype(BF16)
    return _dot(x1, sel) + (_dot(x2, sel) + _dot(x3, sel))


def _scan_rows(x, op, fill):
    n = x.shape[0]
    row = lax.broadcasted_iota(jnp.int32, (n, 1), 0)
    sh = 1
    while sh < n:
        x = op(x, jnp.where(row >= sh, pltpu.roll(x, sh, 0), fill))
        sh *= 2
    return x


def _mlstm_body(q_ref, kt_ref, v_ref, og_ref, gt_ref, gtt_ref, ng_ref, sel_ref, ln_ref, st0_ref, m0_ref,
                 cell_ref, st_out, m_out, st, m_rows, qk_s):
    c = pl.program_id(1)
    L = q_ref.shape[0]
    E = HEAD_DIM
    H = MLSTM_HEADS
    W = V7X_LANES

    @pl.when(c == 0)
    def _():
        st[...] = st0_ref[...]
        m_rows[...] = m0_ref[...]

    gt = gt_ref[...]
    lane = lax.broadcasted_iota(jnp.int32, (1, W), 1)
    head_lane = (lane >= GATE_F0) & (lane < GATE_F0 + H)
    b_cols = _scan_rows(gt, jnp.add, 0.0)
    r_cols = pltpu.roll(gt, GATE_F0, 1) - b_cols
    cm = _scan_rows(r_cols, jnp.maximum, NEG_INF)
    m_all = m_rows[...]
    sub = lax.broadcasted_iota(jnp.int32, m_all.shape, 0)
    m_lane = jnp.max(jnp.where(lax.broadcasted_iota(jnp.int32, m_all.shape, 1) == sub + GATE_F0, m_all, NEG_INF),
                     axis=0, keepdims=True)
    a_cols = jnp.maximum(m_lane, cm)
    zero = lambda x: jnp.where(head_lane, x, 0.0)
    sel = sel_ref[...]
    a_b = _split3_dot(zero(a_cols), sel)
    wi_b = _split3_dot(zero(jnp.exp(m_lane - a_cols)), sel)
    em_b = _split3_dot(zero(jnp.exp(-(b_cols + a_cols))), sel)

    gtt = gtt_ref[0:2 * GATE_F0, :]
    lane_t = lax.broadcasted_iota(jnp.int32, (1, L), 1)
    b_rows = gtt
    sh = 1
    while sh < L:
        b_rows = b_rows + jnp.where(lane_t >= sh, pltpu.roll(b_rows, sh, 1), 0.0)
        sh *= 2
    r_rows = gtt[0:GATE_F0, :] - b_rows[GATE_F0:2 * GATE_F0, :]
    b_last = b_rows[GATE_F0:2 * GATE_F0, L - 1:L]
    m_prev = m_all[:, 0:1]
    m_new = jnp.maximum(b_last + m_prev, b_last + jnp.max(r_rows, axis=1, keepdims=True))
    decay = jnp.exp(b_last + m_prev - m_new)
    wk_scale = jnp.exp(b_last + r_rows - m_new)
    m_next = jnp.broadcast_to(m_new, m_all.shape)
    m_rows[...] = m_next
    m_out[...] = m_next

    causal = lax.broadcasted_iota(jnp.int32, (L, L), 0) >= lax.broadcasted_iota(jnp.int32, (L, L), 1)
    sub_k = lax.broadcasted_iota(jnp.int32, (W, 1), 0)
    lane_w = lax.broadcasted_iota(jnp.int32, (1, W), 1)
    ones_rhs = jnp.ones((L, W), BF16)

    for h in range(H):
        p, odd = h // 2, h % 2
        own_rows = (sub_k >= E) if odd else (sub_k < E)
        kt_pad = jnp.where(own_rows, kt_ref[p * W:(p + 1) * W, :], jnp.zeros((), BF16))
        s = _dot(q_ref[:, p * W:(p + 1) * W], kt_pad)
        w_intra = jnp.exp(jnp.where(causal, r_rows[h:h + 1, :] - a_b[:, h * W:(h + 1) * W], NEG_INF))
        qk_s[h] = s * w_intra

    ln_sel = ln_ref[...]
    for p in range(H // 2):
        slab = slice(p * W, (p + 1) * W)
        q_pair = q_ref[:, slab]
        v_aug = jnp.concatenate([v_ref[:, slab], ones_rhs], axis=1)
        halves = []
        for odd in range(2):
            h = 2 * p + odd
            qk = qk_s[h].astype(BF16)
            wib = wi_b[:, h * W:(h + 1) * W]
            nd = _dot(qk, v_aug) + jnp.concatenate([wib, wib], axis=1) * _dot(q_pair, st[h].astype(BF16))
            halves.append(nd[:, 0:W] / jnp.maximum(jnp.abs(nd[:, W:2 * W]), em_b[:, h * W:(h + 1) * W]))
        hv = jnp.where(lane_w < E, halves[0], halves[1])
        mu = _split3_dot(hv, ln_sel, 2) * (1.0 / E)
        hc = hv - mu
        var = _split3_dot(hc * hc, ln_sel, 2) * (1.0 / E)
        cell_ref[:, slab] = (og_ref[:, slab].astype(F32) * (hc * lax.rsqrt(var + LN_EPS) * ng_ref[:, slab])
                             ).astype(cell_ref.dtype)

    for h in range(H):
        p, odd = h // 2, h % 2
        rows = slice(odd * E, (odd + 1) * E)
        own_lanes = (lane_w >= E) if odd else (lane_w < E)
        wkt = (kt_ref[h * E:(h + 1) * E, :].astype(F32) * wk_scale[h:h + 1, :]).astype(BF16)
        d = decay[h:h + 1, :]
        v_aug = jnp.concatenate([v_ref[:, p * W:(p + 1) * W], ones_rhs], axis=1)
        own2 = jnp.concatenate([own_lanes, own_lanes], axis=1)
        st[h, rows, :] = d * st[h, rows, :] + jnp.where(own2, _dot(wkt, v_aug), 0.0)
    st_out[...] = st[...]


def _mlstm_selectors():
    h_of_col = jnp.arange(MLSTM_HEADS * V7X_LANES) // V7X_LANES
    sel = (jnp.arange(V7X_LANES)[:, None] == GATE_F0 + h_of_col[None, :]).astype(BF16)
    g = jnp.arange(V7X_LANES) // HEAD_DIM
    ln = (g[:, None] == g[None, :]).astype(BF16)
    return sel, ln


def _mlstm(q, kt, v, og, gt, gtt, norm_g, c0, n0, m0, n_seq, seq_len, chunk):
    w, hh, e, lanes = MLSTM_WIDTH, MLSTM_HEADS, HEAD_DIM, V7X_LANES
    nc = seq_len // chunk
    mrows = 2 * V7X_SUBLANES
    blk = lambda a: jnp.stack([jnp.pad(a[:, h], ((0, 0), ((h % 2) * e, (1 - h % 2) * e), ((h % 2) * e, (1 - h % 2) * e)))
                               for h in range(hh)], axis=1)
    unblk = lambda s: jnp.stack([s[:, h, (h % 2) * e:(h % 2 + 1) * e, (h % 2) * e:(h % 2 + 1) * e]
                                 for h in range(hh)], axis=1)
    st0 = jnp.concatenate([blk(c0), blk(jnp.broadcast_to(n0[..., None], (n_seq, hh, e, e)))], axis=-1)
    m0b = jnp.pad(jnp.broadcast_to(m0[..., None], (n_seq, hh, lanes)), ((0, 0), (0, mrows - hh), (0, 0)))
    sel, ln = _mlstm_selectors()
    row = lambda width: pl.BlockSpec((chunk, width), lambda b, c: (b * nc + c, 0))
    col = lambda height: pl.BlockSpec((height, chunk), lambda b, c: (0, b * nc + c))
    const = lambda a: pl.BlockSpec(a.shape, lambda b, c: (0,) * a.ndim)
    st = lambda shp: pl.BlockSpec((None,) + shp, lambda b, c: (b,) + (0,) * len(shp))
    slab = (hh, lanes, 2 * lanes)
    cell, st_out, m_out = pl.pallas_call(
        _mlstm_body,
        grid=(n_seq, nc),
        in_specs=[row(w), col(w), row(w), row(w), row(GATE_LANES), col(GATE_LANES),
                  pl.BlockSpec((1, w), lambda b, c: (0, 0)), const(sel), const(ln),
                  st(slab), st((mrows, lanes))],
        out_specs=[row(w), st(slab), st((mrows, lanes))],
        out_shape=[jax.ShapeDtypeStruct((n_seq * seq_len, w), BF16),
                   jax.ShapeDtypeStruct((n_seq,) + slab, F32),
                   jax.ShapeDtypeStruct((n_seq, mrows, lanes), F32)],
        scratch_shapes=[pltpu.VMEM(slab, F32), pltpu.VMEM((mrows, lanes), F32),
                        pltpu.VMEM((hh, chunk, chunk), F32)],
        compiler_params=_cparams(2, VMEM_LIMIT),
        name="mlstm_chunks",
    )(q, kt, v, og, gt, gtt, norm_g.reshape(1, w), sel, ln, st0, m0b)
    return cell, unblk(st_out[..., 0:lanes]), unblk(st_out[..., lanes:])[..., 0], m_out[:, 0:hh, 0]


def _w_in_b_regroup(w_in, b_if):
    w4 = 4 * MLSTM_WIDTH
    hh = MLSTM_HEADS
    z = lambda n: jnp.zeros((D_MODEL, n), w_in.dtype)
    w = jnp.concatenate([w_in[:, :w4 + hh], z(GATE_F0 - hh), w_in[:, w4 + hh:w4 + 2 * hh],
                         z(GATE_LANES - GATE_F0 - hh), w_in[:, w4 + 2 * hh:]], axis=1)
    zb = lambda n: jnp.zeros((n,), F32)
    bias = jnp.concatenate([b_if[0].astype(F32), zb(GATE_F0 - hh), b_if[1].astype(F32),
                            zb(GATE_LANES - GATE_F0 - hh)]).reshape(1, GATE_LANES)
    return w, bias


def _layer_b_prompt(h_all, w_bf, bias, conv_w, norm_g, w_out, mem_prompt, w_mem, ln_g, ln_b, n_seq, seq_len,
                    tail):
    m = n_seq * seq_len
    tile = min(ROW_TILE, seq_len)
    chunk = min(MLSTM_CHUNK, seq_len)
    zstate = jnp.zeros((n_seq * V7X_SUBLANES, 2 * MLSTM_WIDTH), F32)
    q, kt, v, og, gt, gtt, mq, conv = _in_proj_b(h_all, 0, m, w_bf, conv_w, bias, [zstate] * 3, n_seq, seq_len,
                                                 tile, False)
    hh, e = MLSTM_HEADS, HEAD_DIM
    cell, c_out, n_out, m_out = _mlstm(q, kt, v, og, gt, gtt, norm_g, jnp.zeros((n_seq, hh, e, e), F32),
                                       jnp.zeros((n_seq, hh, e), F32), jnp.zeros((n_seq, hh), F32),
                                       n_seq, seq_len, chunk)
    mem_kv = _matmul(mem_prompt.reshape(n_seq * N_MEM, D_MODEL), w_mem, N_MEM)
    mo = _mem_attn(mq, mem_kv.reshape(n_seq, N_MEM, 2 * MEM_WIDTH), seq_len, tile)
    h1 = _out_proj(cell, mo, h_all, 0, w_out.astype(BF16), ln_g, ln_b, tile, tail)
    return h1, conv, c_out, n_out.reshape(n_seq, hh, e), m_out.reshape(n_seq, hh), mem_kv


def _layer_b_sample(h_all, h_off, w_bf, bias, conv_w, norm_g, w_out, mem_kv, conv_state, c0, n0, m0, ln_g, ln_b,
                    n_b, n_new):
    m = n_b * n_new
    w2 = 2 * MLSTM_WIDTH
    hh, e = MLSTM_HEADS, HEAD_DIM
    chunk = MLSTM_CHUNK
    t = jnp.arange(n_new)
    prevs = []
    for kk in (1, 2, 3):
        idx = jnp.clip(CONV_W - 1 - kk + t, 0, CONV_W - 2)
        prevs.append(conv_state[:, idx, :].reshape(m, w2))
    q, kt, v, og, gt, gtt, mq, u = _in_proj_b(h_all, h_off, m, w_bf, conv_w, bias, prevs, n_b, n_new, m, True)
    conv_new = jnp.concatenate([conv_state, u.reshape(n_b, n_new, w2)], axis=1)[:, n_new:]
    npad = chunk - n_new
    pad3 = lambda a: jnp.pad(a.reshape(n_b, n_new, a.shape[1]), ((0, 0), (0, npad), (0, 0)))
    gate_pad = jnp.where(jnp.arange(GATE_LANES) < hh, NEG_INF, 0.0).astype(F32)
    gtp = jnp.concatenate([gt.reshape(n_b, n_new, GATE_LANES),
                           jnp.broadcast_to(gate_pad, (n_b, npad, GATE_LANES))], axis=1)
    gttp = jnp.concatenate([gtt.reshape(GATE_LANES, n_b, n_new),
                            jnp.broadcast_to(gate_pad[:, None, None], (GATE_LANES, n_b, npad))], axis=2)
    ktp = jnp.pad(kt.reshape(MLSTM_WIDTH, n_b, n_new), ((0, 0), (0, 0), (0, npad)))
    flat = lambda a: a.reshape(n_b * chunk, a.shape[2])
    flat_t = lambda a: a.reshape(a.shape[0], n_b * chunk)
    cell, c_out, n_out, m_out = _mlstm(flat(pad3(q)), flat_t(ktp), flat(pad3(v)), flat(pad3(og)), flat(gtp),
                                       flat_t(gttp), norm_g, c0, n0, m0, n_b, chunk, chunk)
    cell = cell.reshape(n_b, chunk, MLSTM_WIDTH)[:, :n_new].reshape(m, MLSTM_WIDTH)
    mo = _sample_mem_attn(mq.astype(F32).reshape(n_b, n_new, MEM_WIDTH),
                          mem_kv.reshape(n_b, N_MEM, 2 * MEM_WIDTH), n_b, n_new)
    h1 = _out_proj(cell, mo.reshape(m, MEM_WIDTH), h_all, h_off, w_out, ln_g, ln_b, m)
    return h1, conv_new, c_out, n_out.reshape(n_b, hh, e), m_out.reshape(n_b, hh)


def _layer_a_sample(h2d, w_in, w_out, mem_kv, caches, ln_g, ln_b, n_b, n_new):
    m = n_b * n_new
    pos = PAST_LEN + (jnp.arange(m, dtype=jnp.int32) % n_new)
    outs = _in_proj_a(h2d, w_in, _rope_tables(pos), n_new, n_b, m, full_kv=True)
    qs, ks, vs, mq, kvs = outs[0:3], outs[3:6], outs[6:9], outs[9], outs[10:13]
    f3 = lambda t: t.reshape(n_b, n_new, t.shape[1])
    caches = [c.reshape(n_b, c.shape[1], 2 * DIL_WIDTH) for c in caches]
    mix = _sample_attn([f3(t) for t in qs], [f3(t) for t in ks], [f3(t) for t in vs], caches, n_b, n_new)
    mo = _sample_mem_attn(f3(mq), mem_kv.reshape(n_b, N_MEM, 2 * MEM_WIDTH), n_b, n_new)
    h1 = _out_proj(mix.reshape(m, DIL_WIDTH), mo.reshape(m, MEM_WIDTH), h2d, 0, w_out, ln_g, ln_b, m)
    return h1, kvs


def _layer_a_prompt(h2d, mem_prompt, w_in, w_out, w_mem, ln_g, ln_b, n_seq, seq_len, tail):
    tile = min(ROW_TILE, seq_len)
    tabs = _rope_tables(jnp.arange(seq_len, dtype=jnp.int32))
    outs = _in_proj_a(h2d, w_in.astype(BF16), tabs, seq_len, n_seq, tile, full_kv=False)
    qs, ks, vs, mq, kvs = outs[0:3], outs[3:6], outs[6:9], outs[9], outs[10:13]
    mem_kv = _matmul(mem_prompt.reshape(n_seq * N_MEM, D_MODEL), w_mem, N_MEM)
    os_, lses = [], []
    for g, (win, dil) in enumerate(DIL_PAIRS):
        o, lse = _dil_attn(qs[g], ks[g], vs[g], n_seq, seq_len, dil, win // dil)
        os_.append(o)
        lses.append(lse)
    mix = _group_mix(os_, lses, tile)
    mo = _mem_attn(mq, mem_kv.reshape(n_seq, N_MEM, 2 * MEM_WIDTH), seq_len, tile)
    h1 = _out_proj(mix, mo, h2d, 0, w_out.astype(BF16), ln_g, ln_b, tile, tail)
    return h1, kvs, mem_kv


def kernel(x_prompt, x_sample, mem_prompt, cache_win1_kv, cache_win2_kv, cache_win3_kv, cache_mem_kv,
           state_mlstm_conv, state_mlstm_C, state_mlstm_n, state_mlstm_m,
           w_in_A, w_out_A, w_in_B, mlstm_conv_w, mlstm_b_if, mlstm_norm_g, w_out_B,
           w_mem_kv, ln_g, ln_b, w_grp, b_grp, w_exp, b_exp, w_gate, w_up, w_down):
    n_p, seq, d = x_prompt.shape
    n_s, t_new, _ = x_sample.shape
    assert d == D_MODEL and w_in_A.shape[0] == 1 and w_in_B.shape[0] == 1
    mp, ms = n_p * seq, n_s * t_new
    m_all = mp + ms
    assert mp % ROW_TILE == 0 and ms % SMALL_TILE == 0 and mp % ms == 0
    xp = x_prompt.reshape(mp, d)
    xs = x_sample.reshape(ms, d)

    wg_bf, wu_bf, wd_bf = w_gate.astype(BF16), w_up.astype(BF16), w_down.astype(BF16)

    def moe(h, i, parts):
        return _moe_layer(h, mp, w_grp[i], b_grp[i], w_exp[i], b_exp[i], wg_bf, wu_bf, wd_bf, i,
                          ln_g[i, 1], ln_b[i, 1], parts)

    caches = (cache_win1_kv[0], cache_win2_kv[0], cache_win3_kv[0])
    hs, kv_s = _layer_a_sample(xs, w_in_A[0], w_out_A[0], cache_mem_kv[0], caches, ln_g[0, 0], ln_b[0, 0],
                               n_s, t_new)
    h, kv_p, mem_kv0 = _layer_a_prompt(xp, mem_prompt, w_in_A[0], w_out_A[0], w_mem_kv[0], ln_g[0, 0], ln_b[0, 0],
                                       n_p, seq, hs)
    (h,) = moe(h, 0, [(0, m_all)])

    w_b, bias = _w_in_b_regroup(w_in_B[0], mlstm_b_if[0])
    w_bf = w_b.astype(BF16)
    hs, conv_s, c_s, nv_s, mm_s = _layer_b_sample(
        h, mp, w_b, bias, mlstm_conv_w[0], mlstm_norm_g[0], w_out_B[0], cache_mem_kv[1],
        state_mlstm_conv[0], state_mlstm_C[0], state_mlstm_n[0], state_mlstm_m[0],
        ln_g[1, 0], ln_b[1, 0], n_s, t_new)
    h1, conv_p, c_p, nv_p, mm_p, mem_kv1 = _layer_b_prompt(
        h, w_bf, bias, mlstm_conv_w[0], mlstm_norm_g[0], w_out_B[0], mem_prompt, w_mem_kv[1],
        ln_g[1, 0], ln_b[1, 0], n_p, seq, hs)
    yp, ys = moe(h1, 1, [(0, mp), (mp, ms)])

    kv_shape = (2, HEADS_PER_GROUP, HEAD_DIM)
    wins = []
    for g in range(N_DIL_GROUPS):
        wins.append(kv_p[g].reshape((1, n_p, -1) + kv_shape))
        wins.append(kv_s[g].reshape((1, n_s, t_new) + kv_shape))
    mem_kv_p = jnp.stack([mem_kv0, mem_kv1]).reshape((2, n_p, N_MEM, 2, MEM_HEADS, HEAD_DIM))
    return (yp.reshape(n_p, seq, d), ys.reshape(n_s, t_new, d), *wins,
            conv_p[None], conv_s[None], c_p[None], c_s[None], nv_p[None], nv_s[None], mm_p[None], mm_s[None],
            mem_kv_p)
```

```python
import functools

import jax
import jax.numpy as jnp
import numpy as np
from jax import lax
from jax.experimental import pallas as pl
from jax.experimental.pallas import tpu as pltpu

D_MODEL = 1024
HEAD_DIM = 64
ATTN_SCALE = HEAD_DIM ** -0.5
PAST_LEN = 16384
N_MEM = 256
MEM_HEADS = 4
MEM_WIDTH = MEM_HEADS * HEAD_DIM
DIL_PAIRS = ((128, 1), (512, 4), (2048, 16))
N_DIL_GROUPS = len(DIL_PAIRS)
HEADS_PER_GROUP = 4
DIL_WIDTH = HEADS_PER_GROUP * HEAD_DIM
QBLK = 128
ROPE_THETA = 500000.0
ROPE_DIMS = HEAD_DIM // 4
MLSTM_HEADS = 12
MLSTM_WIDTH = MLSTM_HEADS * HEAD_DIM
CONV_W = 4
N_GROUPS = 4
EXPERTS_PER_GROUP = 8
N_EXPERTS = N_GROUPS * EXPERTS_PER_GROUP
D_EXPERT = 256
DEPTH = 2
ALPHA = (2 * DEPTH) ** 0.25
LN_EPS = 1e-5

V7X_LANES = 128
V7X_SUBLANES = 8
V7X_VMEM_BYTES = 64 * 1024 * 1024
VMEM_LIMIT = 48 * 1024 * 1024

ROW_TILE = 512
SMALL_TILE = 128
MLSTM_CHUNK = 128
EXPERT_ROWS = 256

BF16 = jnp.bfloat16
F32 = jnp.float32
NEG_INF = float("-inf")


def _cparams(n_axes, vmem=None, flags=None):
    return pltpu.CompilerParams(dimension_semantics=("arbitrary",) * n_axes,
                                vmem_limit_bytes=vmem, flags=flags)


def _nt_dot(a, b):
    return lax.dot_general(a, b, (((1,), (1,)), ((), ())), preferred_element_type=F32)


def _dot(a, b):
    return jnp.dot(a, b, preferred_element_type=F32)


def _split_bf16(x):
    hi = x.astype(BF16)
    lo = (x - hi.astype(F32)).astype(BF16)
    return hi, lo


def _lhs(x, precise):
    return _split_bf16(x) if precise else (x.astype(BF16),)


def _mm(lhs, w):
    if len(lhs) == 1:
        return _dot(lhs[0], w)
    wh, wl = _split_bf16(w)
    return _dot(lhs[0], wh) + (_dot(lhs[0], wl) + _dot(lhs[1], wh))


def _rope_tables(pos):
    half = ROPE_DIMS // 2
    inv = jnp.power(ROPE_THETA, -jnp.arange(half, dtype=F32) * (2.0 / ROPE_DIMS))
    ang = pos.astype(F32)[:, None] * inv[None, :]
    cos, sin = jnp.cos(ang), jnp.sin(ang)
    n = pos.shape[0]
    one = jnp.ones((n, HEAD_DIM - ROPE_DIMS), F32)
    zero8 = jnp.zeros((n, half), F32)
    zrest = jnp.zeros((n, HEAD_DIM - ROPE_DIMS), F32)
    a = jnp.concatenate([cos, cos, one], axis=1)
    b = jnp.concatenate([zero8, sin, zrest], axis=1)
    c = jnp.concatenate([-sin, zero8, zrest], axis=1)
    rep = V7X_LANES // HEAD_DIM
    return jnp.tile(a, (1, rep)), jnp.tile(b, (1, rep)), jnp.tile(c, (1, rep))


def _rope_apply(x, ra, rb, rc):
    parts = []
    for s in range(x.shape[1] // V7X_LANES):
        v = x[:, s * V7X_LANES:(s + 1) * V7X_LANES]
        parts.append(v * ra + pltpu.roll(v, ROPE_DIMS // 2, 1) * rb
                     + pltpu.roll(v, V7X_LANES - ROPE_DIMS // 2, 1) * rc)
    return jnp.concatenate(parts, axis=1)


def _in_proj_a_body(x_ref, w_ref, ra_ref, rb_ref, rc_ref, *outs, kv_rows, precise):
    q_refs, k_refs, v_refs = outs[0:3], outs[3:6], outs[6:9]
    mq_ref = outs[9]
    kv_refs = outs[10:13]
    lhs = _lhs(x_ref[...], precise)
    ra, rb, rc = ra_ref[...], rb_ref[...], rc_ref[...]
    tm = x_ref.shape[0]
    gw = DIL_WIDTH
    for g in range(N_DIL_GROUPS):
        q = _rope_apply(_mm(lhs, w_ref[:, g * gw:(g + 1) * gw]), ra, rb, rc)
        k = _rope_apply(_mm(lhs, w_ref[:, (3 + g) * gw:(4 + g) * gw]), ra, rb, rc)
        v = _mm(lhs, w_ref[:, (6 + g) * gw:(7 + g) * gw])
        q_refs[g][...] = (q * ATTN_SCALE).astype(q_refs[g].dtype)
        k_refs[g][...] = k.astype(k_refs[g].dtype)
        v_refs[g][...] = v.astype(v_refs[g].dtype)
        r = kv_rows[g]
        kv_refs[g][:, 0:gw] = k[tm - r:, :]
        kv_refs[g][:, gw:2 * gw] = v[tm - r:, :]
    mq = _mm(lhs, w_ref[:, 9 * gw:9 * gw + MEM_WIDTH])
    mq_ref[...] = (mq * ATTN_SCALE).astype(mq_ref.dtype)


def _in_proj_a(x2d, w, tabs, seq_len, n_seq, tile, full_kv):
    w_bf = w
    act = F32 if full_kv else BF16
    m = x2d.shape[0]
    nt = m // tile
    gw = DIL_WIDTH
    row_spec = lambda w: pl.BlockSpec((tile, w), lambda i: (i, 0))
    if full_kv:
        tab_spec = pl.BlockSpec((tile, V7X_LANES), lambda i: (i, 0))
        kv_rows = (tile,) * 3
        kv_shapes = [jax.ShapeDtypeStruct((m, 2 * gw), F32)] * 3
        kv_specs = [row_spec(2 * gw)] * 3
    else:
        tps = seq_len // tile
        tab_spec = pl.BlockSpec((tile, V7X_LANES), lambda i: (i % tps, 0))
        kv_rows, kv_shapes, kv_specs = [], [], []
        for win, _ in DIL_PAIRS:
            wb = min(win, seq_len)
            r = min(wb, tile)
            nblk = wb // r
            kv_rows.append(r)
            kv_shapes.append(jax.ShapeDtypeStruct((n_seq * wb, 2 * gw), F32))
            kv_specs.append(pl.BlockSpec(
                (r, 2 * gw),
                lambda i, nblk=nblk: ((i // tps) * nblk + jnp.maximum(i % tps - (tps - nblk), 0), 0)))
        kv_rows = tuple(kv_rows)
    out_shapes = [jax.ShapeDtypeStruct((m, gw), act)] * 9 + [jax.ShapeDtypeStruct((m, MEM_WIDTH), act)] + kv_shapes
    out_specs = [row_spec(gw)] * 9 + [row_spec(MEM_WIDTH)] + kv_specs
    return pl.pallas_call(
        functools.partial(_in_proj_a_body, kv_rows=kv_rows, precise=full_kv),
        grid=(nt,),
        in_specs=[row_spec(D_MODEL), pl.BlockSpec(w_bf.shape, lambda i: (0, 0)), tab_spec, tab_spec, tab_spec],
        out_specs=out_specs,
        out_shape=out_shapes,
        compiler_params=_cparams(1, VMEM_LIMIT),
        name="in_proj_a",
    )(x2d, w_bf, *tabs)


def _dil_attn_body(q_ref, k_ref, v_ref, kp_ref, vp_ref, o_ref, lse_ref, *, span):
    i = pl.program_id(2)
    q, k, v = q_ref[...], k_ref[...], v_ref[...]
    kp, vp = kp_ref[...], vp_ref[...]
    qi = lax.broadcasted_iota(jnp.int32, (QBLK, 2 * QBLK), 0) + QBLK
    ki = lax.broadcasted_iota(jnp.int32, (QBLK, 2 * QBLK), 1)
    band = (qi >= ki) & (qi - ki <= span) & ((i > 0) | (ki >= QBLK))
    lses = []
    for h in range(HEADS_PER_GROUP):
        hs = slice(h * HEAD_DIM, (h + 1) * HEAD_DIM)
        kc = jnp.concatenate([kp[:, hs], k[:, hs]], axis=0)
        vc = jnp.concatenate([vp[:, hs], v[:, hs]], axis=0)
        s = jnp.where(band, _nt_dot(q[:, hs], kc), NEG_INF)
        m = jnp.max(s, axis=1, keepdims=True)
        p = jnp.exp(s - m)
        den = jnp.sum(p, axis=1, keepdims=True)
        o_ref[:, hs] = _dot((p / den).astype(BF16), vc).astype(o_ref.dtype)
        lses.append(m + jnp.log(den))
    lse_ref[...] = jnp.concatenate(lses, axis=1)


def _dil_attn(q, k, v, n_seq, seq_len, dil, span):
    gw = DIL_WIDTH
    L = seq_len // dil
    nb = L // QBLK
    view = lambda t: t.reshape(n_seq, L, dil * gw)
    blk = pl.BlockSpec((None, QBLK, gw), lambda b, r, i: (b, i, r))
    prev = pl.BlockSpec((None, QBLK, gw), lambda b, r, i: (b, jnp.maximum(i - 1, 0), r))
    o, lse = pl.pallas_call(
        functools.partial(_dil_attn_body, span=span),
        grid=(n_seq, dil, nb),
        in_specs=[blk, blk, blk, prev, prev],
        out_specs=[blk, pl.BlockSpec((None, None, QBLK, HEADS_PER_GROUP), lambda b, r, i: (b, r, i, 0))],
        out_shape=[jax.ShapeDtypeStruct((n_seq, L, dil * gw), BF16),
                   jax.ShapeDtypeStruct((n_seq, dil, L, HEADS_PER_GROUP), F32)],
        compiler_params=_cparams(3),
        name=f"dil_attn_d{dil}",
    )(view(q), view(k), view(v), view(k), view(v))
    o = o.reshape(n_seq * seq_len, gw)
    lse = jnp.transpose(lse, (0, 2, 1, 3)).reshape(n_seq * seq_len, HEADS_PER_GROUP)
    return o, lse


def _layer_norm_rows(x, g, b):
    mu = jnp.mean(x, axis=1, keepdims=True)
    xc = x - mu
    var = jnp.mean(xc * xc, axis=1, keepdims=True)
    return xc * lax.rsqrt(var + LN_EPS) * g + b


def _out_proj_body(a_ref, mo_ref, h_ref, w_ref, g_ref, b_ref, o_ref, *, precise):
    ka = a_ref.shape[1]
    y = _mm(_lhs(a_ref[...], precise), w_ref[0:ka, :]) + _mm(_lhs(mo_ref[...], precise), w_ref[ka:, :])
    o_ref[...] = _layer_norm_rows(ALPHA * h_ref[...] + y, g_ref[...], b_ref[...])


def _out_proj(a, mo, h, h_off, w, g, b, tile):
    m, ka = a.shape
    row = lambda width, off=0: pl.BlockSpec((tile, width), lambda i: (i + off // tile, 0))
    const = lambda s: pl.BlockSpec(s, lambda i: (0, 0))
    return pl.pallas_call(
        functools.partial(_out_proj_body, precise=(w.dtype == F32)),
        grid=(m // tile,),
        in_specs=[row(ka), row(MEM_WIDTH), row(D_MODEL, h_off), const(w.shape), const((1, D_MODEL)),
                  const((1, D_MODEL))],
        out_specs=row(D_MODEL),
        out_shape=jax.ShapeDtypeStruct((m, D_MODEL), F32),
        compiler_params=_cparams(1, VMEM_LIMIT),
        name="out_proj_ln",
    )(a, mo, h, w, g.reshape(1, D_MODEL), b.reshape(1, D_MODEL))


def _fused_out_body(*refs, n_parts, n_tiles, has_tail):
    parts = refs[0:n_parts]
    k = n_parts
    lses = refs[k:k + n_parts] if n_parts > 1 else ()
    k += len(lses)
    mq_ref, kv_ref, h_ref, w_ref, g_ref, b_ref = refs[k:k + 6]
    tail_ref = refs[k + 6] if has_tail else None
    o_ref = refs[-1]

    def rows():
        if n_parts == 1:
            a = parts[0][...]
        else:
            ls = [l[...] for l in lses]
            heads = []
            for h in range(HEADS_PER_GROUP):
                hs = slice(h * HEAD_DIM, (h + 1) * HEAD_DIM)
                lh = [l[:, h:h + 1] for l in ls]
                mx = jnp.maximum(jnp.maximum(lh[0], lh[1]), lh[2])
                e = [jnp.exp(x - mx) for x in lh]
                tot = e[0] + e[1] + e[2]
                acc = (e[0] / tot) * parts[0][:, hs].astype(F32)
                acc = acc + (e[1] / tot) * parts[1][:, hs].astype(F32)
                acc = acc + (e[2] / tot) * parts[2][:, hs].astype(F32)
                heads.append(acc)
            a = jnp.concatenate(heads, axis=1).astype(BF16)
        ka = a.shape[1]
        q = mq_ref[...]
        kv = kv_ref[...].astype(BF16)
        mos = []
        for h in range(MEM_HEADS):
            hs = slice(h * HEAD_DIM, (h + 1) * HEAD_DIM)
            vs = slice(MEM_WIDTH + h * HEAD_DIM, MEM_WIDTH + (h + 1) * HEAD_DIM)
            s = _nt_dot(q[:, hs], kv[:, hs])
            m = jnp.max(s, axis=1, keepdims=True)
            p = jnp.exp(s - m)
            den = jnp.sum(p, axis=1, keepdims=True)
            mos.append(_dot((p / den).astype(BF16), kv[:, vs]))
        mo = jnp.concatenate(mos, axis=1).astype(BF16)
        y = _dot(a, w_ref[0:ka, :]) + _dot(mo, w_ref[ka:, :])
        o_ref[...] = _layer_norm_rows(ALPHA * h_ref[...] + y, g_ref[...], b_ref[...])

    if not has_tail:
        rows()
    else:
        pl.when(pl.program_id(0) < n_tiles)(rows)

        @pl.when(pl.program_id(0) == n_tiles)
        def _():
            o_ref[0:tail_ref.shape[0], :] = tail_ref[...]


def _fused_out(parts, lses, mq, mem_kv, h, w_bf, g, b, tile, seq_len, tail):
    m = parts[0].shape[0]
    nt = m // tile
    last = nt - 1
    tps = seq_len // tile
    clamp = lambda i: jnp.minimum(i, last)
    row = lambda w: pl.BlockSpec((tile, w), lambda i: (clamp(i), 0))
    const = lambda s: pl.BlockSpec(s, lambda i: (0,) * len(s))
    in_specs = [row(p.shape[1]) for p in parts] + [row(l.shape[1]) for l in lses]
    in_specs += [row(MEM_WIDTH), pl.BlockSpec((None, N_MEM, 2 * MEM_WIDTH), lambda i: (clamp(i) // tps, 0, 0)),
                 row(D_MODEL), const(w_bf.shape), const((1, D_MODEL)), const((1, D_MODEL))]
    args = list(parts) + list(lses) + [mq, mem_kv, h, w_bf, g.reshape(1, D_MODEL), b.reshape(1, D_MODEL)]
    out_rows, steps = m, nt
    if tail is not None:
        assert tail.shape[0] <= tile
        in_specs.append(const(tail.shape))
        args.append(tail)
        out_rows, steps = m + tail.shape[0], nt + 1
    return pl.pallas_call(
        functools.partial(_fused_out_body, n_parts=len(parts), n_tiles=nt, has_tail=tail is not None),
        grid=(steps,),
        in_specs=in_specs,
        out_specs=pl.BlockSpec((tile, D_MODEL), lambda i: (i, 0)),
        out_shape=jax.ShapeDtypeStruct((out_rows, D_MODEL), F32),
        compiler_params=_cparams(1, VMEM_LIMIT),
        name="attn_out_proj_ln",
    )(*args)


def _matmul_body(x_ref, w_ref, o_ref):
    o_ref[...] = _dot(x_ref[...].astype(BF16), w_ref[...].astype(BF16))


def _matmul(x, w, tile):
    m, k = x.shape
    n = w.shape[1]
    return pl.pallas_call(
        _matmul_body,
        grid=(m // tile,),
        in_specs=[pl.BlockSpec((tile, k), lambda i: (i, 0)), pl.BlockSpec((k, n), lambda i: (0, 0))],
        out_specs=pl.BlockSpec((tile, n), lambda i: (i, 0)),
        out_shape=jax.ShapeDtypeStruct((m, n), F32),
        compiler_params=_cparams(1),
        name="mem_kv_proj",
    )(x, w)


def _col_attend(q_row, kmat, vmat, kmask, knew, vnew, nmask):
    pk = kmat * q_row
    pn = None if knew is None else knew * q_row
    outs, lses = [], []
    for h in range(HEADS_PER_GROUP):
        hs = slice(h * HEAD_DIM, (h + 1) * HEAD_DIM)
        s = jnp.sum(pk[:, hs], axis=1, keepdims=True)
        if kmask is not None:
            s = jnp.where(kmask, s, NEG_INF)
        m = jnp.max(s, axis=0, keepdims=True)
        if pn is not None:
            sn = jnp.where(nmask, jnp.sum(pn[:, hs], axis=1, keepdims=True), NEG_INF)
            m = jnp.maximum(m, jnp.max(sn, axis=0, keepdims=True))
        p = jnp.exp(s - m)
        den = jnp.sum(p, axis=0, keepdims=True)
        acc = jnp.sum(p * vmat[:, hs], axis=0, keepdims=True)
        if pn is not None:
            pnw = jnp.exp(sn - m)
            den = den + jnp.sum(pnw, axis=0, keepdims=True)
            acc = acc + jnp.sum(pnw * vnew[:, hs], axis=0, keepdims=True)
        outs.append(acc / den)
        lses.append(m + jnp.log(den))
    return outs, lses


def _sample_attn_body(q1, q2, q3, kn1, kn2, kn3, vn1, vn2, vn3, c1, c2, c3, mix_ref, *, n_new):
    qs = (q1[...], q2[...], q3[...])
    kns = (kn1[...], kn2[...], kn3[...])
    vns = (vn1[...], vn2[...], vn3[...])
    gw = DIL_WIDTH
    t_idx = lax.broadcasted_iota(jnp.int32, (n_new, 1), 0)
    rows = []
    for t in range(n_new):
        per_group = []
        for g, (win, dil) in enumerate(DIL_PAIRS):
            q_row = qs[g][t:t + 1, :]
            cache = (c1, c2, c3)[g]
            if dil == 1:
                kmat, vmat = cache[:, 0:gw], cache[:, gw:2 * gw]
                r_idx = lax.broadcasted_iota(jnp.int32, (kmat.shape[0], 1), 0)
                per_group.append(_col_attend(q_row, kmat, vmat, r_idx >= t, kns[g], vns[g], t_idx <= t))
            else:
                base = t * 2 * gw
                kmat, vmat = cache[:, base:base + gw], cache[:, base + gw:base + 2 * gw]
                per_group.append(_col_attend(q_row, kmat, vmat, None, kns[g], vns[g], t_idx == t))
        heads = []
        for h in range(HEADS_PER_GROUP):
            lh = [per_group[g][1][h] for g in range(N_DIL_GROUPS)]
            mx = jnp.maximum(jnp.maximum(lh[0], lh[1]), lh[2])
            e = [jnp.exp(x - mx) for x in lh]
            tot = e[0] + e[1] + e[2]
            acc = (e[0] / tot) * per_group[0][0][h]
            acc = acc + (e[1] / tot) * per_group[1][0][h]
            acc = acc + (e[2] / tot) * per_group[2][0][h]
            heads.append(acc)
        rows.append(jnp.concatenate(heads, axis=1))
    mix_ref[...] = jnp.concatenate(rows, axis=0)


def _sample_attn(qs, kns, vns, caches, n_b, n_new):
    gw = DIL_WIDTH
    small = pl.BlockSpec((None, n_new, gw), lambda b: (b, 0, 0))
    cviews, cspecs = [], []
    for c, (win, dil) in zip(caches, DIL_PAIRS):
        blocks = win // dil
        cviews.append(c.reshape(n_b, blocks, dil * 2 * gw))
        lanes = min(dil, n_new) * 2 * gw
        cspecs.append(pl.BlockSpec((None, blocks, lanes), lambda b: (b, 0, 0)))
    return pl.pallas_call(
        functools.partial(_sample_attn_body, n_new=n_new),
        grid=(n_b,),
        in_specs=[small] * 9 + cspecs,
        out_specs=small,
        out_shape=jax.ShapeDtypeStruct((n_b, n_new, gw), F32),
        compiler_params=_cparams(1, VMEM_LIMIT),
        name="sample_dil_attn",
    )(*qs, *kns, *vns, *cviews)


def _sample_mem_attn_body(q_ref, kv_ref, o_ref, *, n_new):
    q = q_ref[...]
    kmat, vmat = kv_ref[:, 0:MEM_WIDTH], kv_ref[:, MEM_WIDTH:2 * MEM_WIDTH]
    rows = []
    for t in range(n_new):
        outs, _ = _col_attend(q[t:t + 1, :], kmat, vmat, None, None, None, None)
        rows.append(jnp.concatenate(outs, axis=1))
    o_ref[...] = jnp.concatenate(rows, axis=0)


def _sample_mem_attn(q, mem_kv, n_b, n_new):
    small = pl.BlockSpec((None, n_new, MEM_WIDTH), lambda b: (b, 0, 0))
    return pl.pallas_call(
        functools.partial(_sample_mem_attn_body, n_new=n_new),
        grid=(n_b,),
        in_specs=[small, pl.BlockSpec((None, N_MEM, 2 * MEM_WIDTH), lambda b: (b, 0, 0))],
        out_specs=small,
        out_shape=jax.ShapeDtypeStruct((n_b, n_new, MEM_WIDTH), F32),
        compiler_params=_cparams(1),
        name="sample_mem_attn",
    )(q, mem_kv)


ROUTE_LANES = V7X_LANES


def _router_body(x_ref, w_ref, b_ref, cnt0_ref, o_ref, cnt_ref):
    xh, xl = _split_bf16(x_ref[...])
    wh, wl = _split_bf16(w_ref[...])
    logits = _dot(xh, wh) + (_dot(xh, wl) + _dot(xl, wh)) + b_ref[...]
    lane = lax.broadcasted_iota(jnp.int32, logits.shape, 1)
    big = jnp.int32(ROUTE_LANES)
    is_grp = (lane >= N_EXPERTS) & (lane < N_EXPERTS + N_GROUPS)
    gl = jnp.where(is_grp, logits, NEG_INF)
    gmax = jnp.max(gl, axis=1, keepdims=True)
    gsel = jnp.min(jnp.where(gl == gmax, lane, big), axis=1, keepdims=True) - N_EXPERTS
    gp = 1.0 / jnp.sum(jnp.exp(gl - gmax), axis=1, keepdims=True)
    in_grp = (lane < N_EXPERTS) & ((lane // EXPERTS_PER_GROUP) == gsel)
    el = jnp.where(in_grp, logits, NEG_INF)
    v1 = jnp.max(el, axis=1, keepdims=True)
    i1 = jnp.min(jnp.where(el == v1, lane, big), axis=1, keepdims=True)
    el2 = jnp.where(lane == i1, NEG_INF, el)
    v2 = jnp.max(el2, axis=1, keepdims=True)
    i2 = jnp.min(jnp.where(el2 == v2, lane, big), axis=1, keepdims=True)
    e2 = jnp.exp(v2 - v1)
    w1 = (1.0 / (1.0 + e2)) * gp
    w2 = (e2 / (1.0 + e2)) * gp
    @pl.when(pl.program_id(0) == 0)
    def _():
        cnt_ref[...] = cnt0_ref[...]

    tm = logits.shape[0]
    oh1 = (lane == i1).astype(F32)
    oh2 = (lane == i2).astype(F32)
    tri = (lax.broadcasted_iota(jnp.int32, (tm, tm), 0) > lax.broadcasted_iota(jnp.int32, (tm, tm), 1)).astype(BF16)
    base = cnt_ref[0:1, :]
    c1 = jnp.sum(oh1, axis=0, keepdims=True)
    c2 = jnp.sum(oh2, axis=0, keepdims=True)
    r1 = jnp.sum(oh1 * (_dot(tri, oh1.astype(BF16)) + base), axis=1, keepdims=True)
    r2 = jnp.sum(oh2 * (_dot(tri, oh2.astype(BF16)) + (base + c1)), axis=1, keepdims=True)
    cnt_ref[...] = jnp.broadcast_to(base + c1 + c2, cnt_ref.shape)
    out = jnp.where(lane == 0, i1.astype(F32), 0.0)
    out = jnp.where(lane == 1, i2.astype(F32), out)
    out = jnp.where(lane == 2, w1, out)
    out = jnp.where(lane == 3, w2, out)
    out = jnp.where(lane == 4, r1, out)
    out = jnp.where(lane == 5, r2, out)
    o_ref[...] = out


def _router(x, row_off, n_rows, w_grp, b_grp, w_exp, b_exp, tile, counts0):
    pad = ROUTE_LANES - N_EXPERTS - N_GROUPS
    w = jnp.concatenate([w_exp, w_grp, jnp.zeros((D_MODEL, pad), F32)], axis=1)
    b = jnp.concatenate([b_exp, b_grp, jnp.zeros((pad,), F32)]).reshape(1, ROUTE_LANES)
    ob = row_off // tile
    const = lambda s: pl.BlockSpec(s, lambda i: (0, 0))
    cshape = (V7X_SUBLANES, ROUTE_LANES)
    return pl.pallas_call(
        _router_body,
        grid=(n_rows // tile,),
        in_specs=[pl.BlockSpec((tile, D_MODEL), lambda i: (i + ob, 0)), const(w.shape), const(b.shape), const(cshape)],
        out_specs=[pl.BlockSpec((tile, ROUTE_LANES), lambda i: (i, 0)), const(cshape)],
        out_shape=[jax.ShapeDtypeStruct((n_rows, ROUTE_LANES), F32), jax.ShapeDtypeStruct(cshape, F32)],
        compiler_params=_cparams(1, VMEM_LIMIT),
        name="moe_router",
    )(x, w, b, counts0)


DMA_UNROLL = True


def _dispatch_body(pos_ref, x_ref, xs_hbm, xbuf, sem, *, nt):
    t = pl.program_id(0)
    s = t % 2
    tile = x_ref.shape[0]
    xbuf[s] = x_ref[...]

    def body(j, c):
        src = xbuf.at[s, pl.ds(j, 1)]
        pltpu.make_async_copy(src, xs_hbm.at[pl.ds(pos_ref[0, 0, j], 1)], sem.at[s]).start(priority=0)
        pltpu.make_async_copy(src, xs_hbm.at[pl.ds(pos_ref[0, 0, tile + j], 1)], sem.at[s]).start(priority=1)
        return c
    lax.fori_loop(0, tile, body, 0, unroll=DMA_UNROLL)

    def drain(slot):
        for _ in range(2):
            pltpu.make_async_copy(xbuf.at[slot], xs_hbm.at[pl.ds(0, tile)], sem.at[slot]).wait()

    @pl.when(t >= 1)
    def _():
        drain(1 - s)

    @pl.when(t == nt - 1)
    def _():
        drain(s)


def _dispatch(x, pos, tile):
    m = x.shape[0]
    return pl.pallas_call(
        functools.partial(_dispatch_body, nt=m // tile),
        grid=(m // tile,),
        in_specs=[pl.BlockSpec((1, 1, 2 * tile), lambda i: (i, 0, 0), memory_space=pltpu.SMEM),
                  pl.BlockSpec((tile, D_MODEL), lambda i: (i, 0))],
        out_specs=pl.BlockSpec(memory_space=pl.ANY),
        out_shape=jax.ShapeDtypeStruct((2 * m, D_MODEL), F32),
        scratch_shapes=[pltpu.VMEM((2, tile, D_MODEL), F32), pltpu.SemaphoreType.DMA((2,))],
        compiler_params=_cparams(1),
        name="moe_dispatch",
    )(pos, x)


def _expert_body(it_ref, ie_ref, lo_ref, hi_ref, x_ref, wg_ref, wu_ref, wd_ref, y_ref):
    w = pl.program_id(0)
    tile = it_ref[w]
    lo, hi = lo_ref[w], hi_ref[w]
    first = (w == 0) | (it_ref[jnp.maximum(w - 1, 0)] != tile)

    @pl.when(first)
    def _():
        y_ref[...] = jnp.zeros_like(y_ref)

    @pl.when(lo < hi)
    def _():
        xb = x_ref[...].astype(BF16)
        hg = _dot(xb, wg_ref[...])
        hu = _dot(xb, wu_ref[...])
        he = (hg * jax.nn.sigmoid(hg)) * hu
        y = _dot(he.astype(BF16), wd_ref[...])
        row = tile * EXPERT_ROWS + lax.broadcasted_iota(jnp.int32, (EXPERT_ROWS, 1), 0)
        y_ref[...] += jnp.where((row >= lo) & (row < hi), y, 0.0)


def _experts(xs, items, w_gate, w_up, w_down, layer):
    n_items = items[0].shape[0]
    rows = EXPERT_ROWS
    tspec = pl.BlockSpec((rows, D_MODEL), lambda w, it, ie, lo, hi: (it[w], 0))
    wspec = lambda shp: pl.BlockSpec((None, None) + shp, lambda w, it, ie, lo, hi: (layer, ie[w], 0, 0))
    grid_spec = pltpu.PrefetchScalarGridSpec(
        num_scalar_prefetch=4,
        grid=(n_items,),
        in_specs=[tspec, wspec((D_MODEL, D_EXPERT)), wspec((D_MODEL, D_EXPERT)), wspec((D_EXPERT, D_MODEL))],
        out_specs=tspec)
    return pl.pallas_call(
        _expert_body,
        grid_spec=grid_spec,
        out_shape=jax.ShapeDtypeStruct(xs.shape, F32),
        compiler_params=_cparams(1, VMEM_LIMIT),
        name="moe_experts",
    )(*items, xs, w_gate, w_up, w_down)


def _moe_combine_body(pos_ref, h_ref, r_ref, g_ref, b_ref, ys_hbm, o_ref, ybuf, sem, *, nt):
    t = pl.program_id(0)
    tile = h_ref.shape[0]

    @pl.when(t < nt)
    def _():
        s = t % 2

        def body(j, c):
            for k in range(2):
                jj = j + k * tile
                pltpu.make_async_copy(ys_hbm.at[pl.ds(pos_ref[0, 0, jj], 1)], ybuf.at[s, pl.ds(jj, 1)],
                                      sem.at[s]).start(priority=k)
            return c
        lax.fori_loop(0, tile, body, 0, unroll=DMA_UNROLL)

    @pl.when(t >= 1)
    def _():
        s = (t + 1) % 2
        pltpu.make_async_copy(ys_hbm.at[pl.ds(0, 2 * tile)], ybuf.at[s], sem.at[s]).wait()
        r = r_ref[...]
        moe = r[:, 2:3] * ybuf[s, 0:tile, :] + r[:, 3:4] * ybuf[s, tile:2 * tile, :]
        o_ref[...] = _layer_norm_rows(ALPHA * h_ref[...] + moe, g_ref[...], b_ref[...])


def _moe_combine(h, ys, pos, route, g, b, tile, row_off, n_rows):
    ob = row_off // tile
    nt = n_rows // tile
    done = lambda i: jnp.maximum(i - 1, 0)
    row = lambda w: pl.BlockSpec((tile, w), lambda i: (done(i) + ob, 0))
    const = lambda s: pl.BlockSpec(s, lambda i: (0, 0))
    return pl.pallas_call(
        functools.partial(_moe_combine_body, nt=nt),
        grid=(nt + 1,),
        in_specs=[pl.BlockSpec((1, 1, 2 * tile), lambda i: (jnp.minimum(i, nt - 1) + ob, 0, 0),
                               memory_space=pltpu.SMEM),
                  row(D_MODEL), row(ROUTE_LANES), const((1, D_MODEL)), const((1, D_MODEL)),
                  pl.BlockSpec(memory_space=pl.ANY)],
        out_specs=pl.BlockSpec((tile, D_MODEL), lambda i: (done(i), 0)),
        out_shape=jax.ShapeDtypeStruct((n_rows, D_MODEL), F32),
        scratch_shapes=[pltpu.VMEM((2, 2 * tile, D_MODEL), F32), pltpu.SemaphoreType.DMA((2,))],
        compiler_params=_cparams(1, VMEM_LIMIT),
        name="moe_combine_ln",
    )(pos, h, route, g.reshape(1, D_MODEL), b.reshape(1, D_MODEL), ys)


def _dispatch_plan(route, counts, m, tile):
    rows = EXPERT_ROWS
    nt = (2 * m) // rows
    cnt = counts[0, 0:N_EXPERTS].astype(jnp.int32)
    end = jnp.cumsum(cnt)
    start = end - cnt
    e = route[:, 0:2].astype(jnp.int32)
    rank = route[:, 4:6].astype(jnp.int32)
    onehot = (e[:, :, None] == jnp.arange(N_EXPERTS, dtype=jnp.int32)[None, None, :])
    pos = jnp.sum(jnp.where(onehot, start[None, None, :], 0), axis=2) + rank
    pos = jnp.transpose(pos.reshape(m // tile, tile, 2), (0, 2, 1)).reshape(m // tile, 1, 2 * tile)
    first_t = start // rows
    n_it = jnp.where(cnt > 0, (end - 1) // rows - first_t + 1, 0)
    it_end = jnp.cumsum(n_it)
    n_items = nt + N_EXPERTS
    w = jnp.arange(n_items, dtype=jnp.int32)
    ie = jnp.minimum(jnp.sum((it_end[None, :] <= w[:, None]).astype(jnp.int32), axis=1), N_EXPERTS - 1)
    pick = lambda a: jnp.sum(jnp.where(ie[:, None] == jnp.arange(N_EXPERTS)[None, :], a[None, :], 0), axis=1)
    valid = w < it_end[-1]
    it = pick(first_t) + (w - (pick(it_end) - pick(n_it)))
    it = jnp.where(valid, it, nt - 1).astype(jnp.int32)
    last_e = jnp.sum(jnp.where(w == it_end[-1] - 1, ie, 0))
    ie = jnp.where(valid, ie, last_e).astype(jnp.int32)
    lo = jnp.where(valid, pick(start), 0).astype(jnp.int32)
    hi = jnp.where(valid, pick(end), 0).astype(jnp.int32)
    return pos.astype(jnp.int32), (it, ie, lo, hi)


def _moe_layer(h, n_big, w_grp, b_grp, w_exp, b_exp, w_gate, w_up, w_down, layer, g, b, parts):
    m = h.shape[0]
    tile = SMALL_TILE
    assert m % tile == 0 and (2 * m) % EXPERT_ROWS == 0 and n_big % ROW_TILE == 0
    zero_counts = jnp.zeros((V7X_SUBLANES, ROUTE_LANES), F32)
    route_a, counts = _router(h, 0, n_big, w_grp, b_grp, w_exp, b_exp, ROW_TILE, zero_counts)
    route_b, counts = _router(h, n_big, m - n_big, w_grp, b_grp, w_exp, b_exp, tile, counts)
    route = jnp.concatenate([route_a, route_b], axis=0)
    pos, items = _dispatch_plan(route, counts, m, tile)
    xs = _dispatch(h, pos, tile)
    ys = _experts(xs, items, w_gate, w_up, w_down, layer)
    return [_moe_combine(h, ys, pos, route, g, b, tile, off, n) for off, n in parts]


GATE_LANES = V7X_LANES
GATE_F0 = 2 * V7X_SUBLANES


def _log_sigmoid(x):
    return jnp.minimum(x, 0.0) - jnp.log1p(jnp.exp(-jnp.abs(x)))


def _silu(x):
    return x * jax.nn.sigmoid(x)


def _in_proj_b_body(x_ref, w_ref, cw_ref, bias_ref, p1_ref, p2_ref, p3_ref,
                    q_ref, kt_ref, v_ref, og_ref, gt_ref, gtt_ref, mq_ref, u_ref, carry, *, seq_rows):
    w = MLSTM_WIDTH
    lhs = _lhs(x_ref[...], w_ref.dtype == F32)
    tm = x_ref.shape[0]
    u = _mm(lhs, w_ref[:, 0:2 * w])
    row = lax.broadcasted_iota(jnp.int32, (tm, 1), 0)
    if seq_rows is None:
        @pl.when(pl.program_id(1) == 0)
        def _():
            carry[...] = p1_ref[...]
        uc = jnp.concatenate([carry[...], u], axis=0)
        shifted = [uc[V7X_SUBLANES - k:V7X_SUBLANES - k + tm, :] for k in (1, 2, 3)]
        carry[...] = u[tm - V7X_SUBLANES:, :]
        u_ref[...] = u[tm - (CONV_W - 1):, :]
    else:
        t = row % seq_rows
        prevs = (p1_ref[...], p2_ref[...], p3_ref[...])
        shifted = [jnp.where(t >= k, pltpu.roll(u, k, 0), prevs[k - 1]) for k in (1, 2, 3)]
        u_ref[...] = u
    cw = cw_ref[...]
    y = shifted[2] * cw[0:1, :]
    y = y + shifted[1] * cw[1:2, :]
    y = y + shifted[0] * cw[2:3, :]
    y = y + u * cw[3:4, :]
    qk = _silu(y)
    q_ref[...] = qk[:, 0:w].astype(BF16)
    kt_ref[...] = (qk[:, w:2 * w] * ATTN_SCALE).T.astype(BF16)
    v_ref[...] = _mm(lhs, w_ref[:, 2 * w:3 * w]).astype(BF16)
    og_ref[...] = jax.nn.sigmoid(_mm(lhs, w_ref[:, 3 * w:4 * w])).astype(BF16)
    gates = _mm(lhs, w_ref[:, 4 * w:4 * w + GATE_LANES]) + bias_ref[...]
    lane = lax.broadcasted_iota(jnp.int32, gates.shape, 1)
    is_f = (lane >= GATE_F0) & (lane < GATE_F0 + MLSTM_HEADS)
    gt = jnp.where(is_f, _log_sigmoid(gates), gates)
    gt_ref[...] = gt
    gtt_ref[...] = gt.T
    mq_ref[...] = (_mm(lhs, w_ref[:, 4 * w + GATE_LANES:]) * ATTN_SCALE).astype(mq_ref.dtype)


def _in_proj_b(x, x_off, m, w_bf, conv_w, bias, prevs, n_seq, seq_len, tile, per_tile_seqs):
    w = MLSTM_WIDTH
    tps = max(seq_len // tile, 1)
    ob = x_off // tile
    row = lambda width: pl.BlockSpec((tile, width), lambda b, i: (b * tps + i, 0))
    col = lambda height: pl.BlockSpec((height, tile), lambda b, i: (0, b * tps + i))
    const = lambda s: pl.BlockSpec(s, lambda b, i: (0, 0))
    if per_tile_seqs:
        pspecs = [row(2 * w)] * 3
        u_rows, u_spec = m, row(2 * w)
    else:
        pspecs = [pl.BlockSpec((V7X_SUBLANES, 2 * w), lambda b, i: (b, 0))] * 3
        u_rows = n_seq * (CONV_W - 1)
        u_spec = pl.BlockSpec((None, CONV_W - 1, 2 * w), lambda b, i: (b, 0, 0))
    u_shape = (jax.ShapeDtypeStruct((m, 2 * w), F32) if per_tile_seqs
               else jax.ShapeDtypeStruct((n_seq, CONV_W - 1, 2 * w), F32))
    return pl.pallas_call(
        functools.partial(_in_proj_b_body, seq_rows=seq_len if per_tile_seqs else None),
        grid=(m // (tps * tile), tps),
        in_specs=[pl.BlockSpec((tile, D_MODEL), lambda b, i: (b * tps + i + ob, 0)),
                  const(w_bf.shape), const(conv_w.shape), const(bias.shape)] + pspecs,
        out_specs=[row(w), col(w), row(w), row(w), row(GATE_LANES), col(GATE_LANES), row(MEM_WIDTH), u_spec],
        out_shape=[jax.ShapeDtypeStruct((m, w), BF16), jax.ShapeDtypeStruct((w, m), BF16),
                   jax.ShapeDtypeStruct((m, w), BF16), jax.ShapeDtypeStruct((m, w), BF16),
                   jax.ShapeDtypeStruct((m, GATE_LANES), F32), jax.ShapeDtypeStruct((GATE_LANES, m), F32),
                   jax.ShapeDtypeStruct((m, MEM_WIDTH), BF16), u_shape],
        scratch_shapes=[pltpu.VMEM((V7X_SUBLANES, 2 * w), F32)],
        compiler_params=_cparams(2, VMEM_LIMIT),
        name="in_proj_b",
    )(x, w_bf, conv_w, bias, *prevs)


def _split3_dot(x, sel, terms=3):
    x1 = x.astype(BF16)
    r1 = x - x1.astype(F32)
    x2 = r1.astype(BF16)
    if terms == 2:
        return _dot(x1, sel) + _dot(x2, sel)
    x3 = (r1 - x2.astype(F32)).astype(BF16)
    return _dot(x1, sel) + (_dot(x2, sel) + _dot(x3, sel))


def _scan_rows(x, op, fill):
    n = x.shape[0]
    row = lax.broadcasted_iota(jnp.int32, (n, 1), 0)
    sh = 1
    while sh < n:
        x = op(x, jnp.where(row >= sh, pltpu.roll(x, sh, 0), fill))
        sh *= 2
    return x


def _mlstm_body(q_ref, kt_ref, v_ref, og_ref, gt_ref, gtt_ref, ng_ref, sel_ref, ln_ref, st0_ref, m0_ref,
                 cell_ref, st_out, m_out, st, m_rows, qk_s):
    c = pl.program_id(1)
    L = q_ref.shape[0]
    E = HEAD_DIM
    H = MLSTM_HEADS
    W = V7X_LANES

    @pl.when(c == 0)
    def _():
        st[...] = st0_ref[...]
        m_rows[...] = m0_ref[...]

    gt = gt_ref[...]
    lane = lax.broadcasted_iota(jnp.int32, (1, W), 1)
    head_lane = (lane >= GATE_F0) & (lane < GATE_F0 + H)
    b_cols = _scan_rows(gt, jnp.add, 0.0)
    r_cols = pltpu.roll(gt, GATE_F0, 1) - b_cols
    cm = _scan_rows(r_cols, jnp.maximum, NEG_INF)
    m_all = m_rows[...]
    sub = lax.broadcasted_iota(jnp.int32, m_all.shape, 0)
    m_lane = jnp.max(jnp.where(lax.broadcasted_iota(jnp.int32, m_all.shape, 1) == sub + GATE_F0, m_all, NEG_INF),
                     axis=0, keepdims=True)
    a_cols = jnp.maximum(m_lane, cm)
    zero = lambda x: jnp.where(head_lane, x, 0.0)
    sel = sel_ref[...]
    a_b = _split3_dot(zero(a_cols), sel)
    wi_b = _split3_dot(zero(jnp.exp(m_lane - a_cols)), sel)
    em_b = _split3_dot(zero(jnp.exp(-(b_cols + a_cols))), sel)

    gtt = gtt_ref[0:2 * GATE_F0, :]
    lane_t = lax.broadcasted_iota(jnp.int32, (1, L), 1)
    b_rows = gtt
    sh = 1
    while sh < L:
        b_rows = b_rows + jnp.where(lane_t >= sh, pltpu.roll(b_rows, sh, 1), 0.0)
        sh *= 2
    r_rows = gtt[0:GATE_F0, :] - b_rows[GATE_F0:2 * GATE_F0, :]
    b_last = b_rows[GATE_F0:2 * GATE_F0, L - 1:L]
    m_prev = m_all[:, 0:1]
    m_new = jnp.maximum(b_last + m_prev, b_last + jnp.max(r_rows, axis=1, keepdims=True))
    decay = jnp.exp(b_last + m_prev - m_new)
    wk_scale = jnp.exp(b_last + r_rows - m_new)
    m_next = jnp.broadcast_to(m_new, m_all.shape)
    m_rows[...] = m_next
    m_out[...] = m_next

    causal = lax.broadcasted_iota(jnp.int32, (L, L), 0) >= lax.broadcasted_iota(jnp.int32, (L, L), 1)
    sub_k = lax.broadcasted_iota(jnp.int32, (W, 1), 0)
    lane_w = lax.broadcasted_iota(jnp.int32, (1, W), 1)
    ones_rhs = jnp.ones((L, W), BF16)

    for h in range(H):
        p, odd = h // 2, h % 2
        own_rows = (sub_k >= E) if odd else (sub_k < E)
        kt_pad = jnp.where(own_rows, kt_ref[p * W:(p + 1) * W, :], jnp.zeros((), BF16))
        s = _dot(q_ref[:, p * W:(p + 1) * W], kt_pad)
        w_intra = jnp.exp(jnp.where(causal, r_rows[h:h + 1, :] - a_b[:, h * W:(h + 1) * W], NEG_INF))
        qk_s[h] = s * w_intra

    ln_sel = ln_ref[...]
    for p in range(H // 2):
        slab = slice(p * W, (p + 1) * W)
        q_pair = q_ref[:, slab]
        v_aug = jnp.concatenate([v_ref[:, slab], ones_rhs], axis=1)
        halves = []
        for odd in range(2):
            h = 2 * p + odd
            qk = qk_s[h].astype(BF16)
            wib = wi_b[:, h * W:(h + 1) * W]
            nd = _dot(qk, v_aug) + jnp.concatenate([wib, wib], axis=1) * _dot(q_pair, st[h].astype(BF16))
            halves.append(nd[:, 0:W] / jnp.maximum(jnp.abs(nd[:, W:2 * W]), em_b[:, h * W:(h + 1) * W]))
        hv = jnp.where(lane_w < E, halves[0], halves[1])
        mu = _split3_dot(hv, ln_sel, 2) * (1.0 / E)
        hc = hv - mu
        var = _split3_dot(hc * hc, ln_sel, 2) * (1.0 / E)
        cell_ref[:, slab] = (og_ref[:, slab].astype(F32) * (hc * lax.rsqrt(var + LN_EPS) * ng_ref[:, slab])
                             ).astype(cell_ref.dtype)

    for h in range(H):
        p, odd = h // 2, h % 2
        rows = slice(odd * E, (odd + 1) * E)
        own_lanes = (lane_w >= E) if odd else (lane_w < E)
        wkt = (kt_ref[h * E:(h + 1) * E, :].astype(F32) * wk_scale[h:h + 1, :]).astype(BF16)
        d = decay[h:h + 1, :]
        v_aug = jnp.concatenate([v_ref[:, p * W:(p + 1) * W], ones_rhs], axis=1)
        own2 = jnp.concatenate([own_lanes, own_lanes], axis=1)
        st[h, rows, :] = d * st[h, rows, :] + jnp.where(own2, _dot(wkt, v_aug), 0.0)
    st_out[...] = st[...]


def _mlstm_selectors():
    h_of_col = jnp.arange(MLSTM_HEADS * V7X_LANES) // V7X_LANES
    sel = (jnp.arange(V7X_LANES)[:, None] == GATE_F0 + h_of_col[None, :]).astype(BF16)
    g = jnp.arange(V7X_LANES) // HEAD_DIM
    ln = (g[:, None] == g[None, :]).astype(BF16)
    return sel, ln


def _mlstm(q, kt, v, og, gt, gtt, norm_g, c0, n0, m0, n_seq, seq_len, chunk):
    w, hh, e, lanes = MLSTM_WIDTH, MLSTM_HEADS, HEAD_DIM, V7X_LANES
    nc = seq_len // chunk
    mrows = 2 * V7X_SUBLANES
    blk = lambda a: jnp.stack([jnp.pad(a[:, h], ((0, 0), ((h % 2) * e, (1 - h % 2) * e), ((h % 2) * e, (1 - h % 2) * e)))
                               for h in range(hh)], axis=1)
    unblk = lambda s: jnp.stack([s[:, h, (h % 2) * e:(h % 2 + 1) * e, (h % 2) * e:(h % 2 + 1) * e]
                                 for h in range(hh)], axis=1)
    st0 = jnp.concatenate([blk(c0), blk(jnp.broadcast_to(n0[..., None], (n_seq, hh, e, e)))], axis=-1)
    m0b = jnp.pad(jnp.broadcast_to(m0[..., None], (n_seq, hh, lanes)), ((0, 0), (0, mrows - hh), (0, 0)))
    sel, ln = _mlstm_selectors()
    row = lambda width: pl.BlockSpec((chunk, width), lambda b, c: (b * nc + c, 0))
    col = lambda height: pl.BlockSpec((height, chunk), lambda b, c: (0, b * nc + c))
    const = lambda a: pl.BlockSpec(a.shape, lambda b, c: (0,) * a.ndim)
    st = lambda shp: pl.BlockSpec((None,) + shp, lambda b, c: (b,) + (0,) * len(shp))
    slab = (hh, lanes, 2 * lanes)
    cell, st_out, m_out = pl.pallas_call(
        _mlstm_body,
        grid=(n_seq, nc),
        in_specs=[row(w), col(w), row(w), row(w), row(GATE_LANES), col(GATE_LANES),
                  pl.BlockSpec((1, w), lambda b, c: (0, 0)), const(sel), const(ln),
                  st(slab), st((mrows, lanes))],
        out_specs=[row(w), st(slab), st((mrows, lanes))],
        out_shape=[jax.ShapeDtypeStruct((n_seq * seq_len, w), BF16),
                   jax.ShapeDtypeStruct((n_seq,) + slab, F32),
                   jax.ShapeDtypeStruct((n_seq, mrows, lanes), F32)],
        scratch_shapes=[pltpu.VMEM(slab, F32), pltpu.VMEM((mrows, lanes), F32),
                        pltpu.VMEM((hh, chunk, chunk), F32)],
        compiler_params=_cparams(2, VMEM_LIMIT),
        name="mlstm_chunks",
    )(q, kt, v, og, gt, gtt, norm_g.reshape(1, w), sel, ln, st0, m0b)
    return cell, unblk(st_out[..., 0:lanes]), unblk(st_out[..., lanes:])[..., 0], m_out[:, 0:hh, 0]


def _w_in_b_regroup(w_in, b_if):
    w4 = 4 * MLSTM_WIDTH
    hh = MLSTM_HEADS
    z = lambda n: jnp.zeros((D_MODEL, n), w_in.dtype)
    w = jnp.concatenate([w_in[:, :w4 + hh], z(GATE_F0 - hh), w_in[:, w4 + hh:w4 + 2 * hh],
                         z(GATE_LANES - GATE_F0 - hh), w_in[:, w4 + 2 * hh:]], axis=1)
    zb = lambda n: jnp.zeros((n,), F32)
    bias = jnp.concatenate([b_if[0].astype(F32), zb(GATE_F0 - hh), b_if[1].astype(F32),
                            zb(GATE_LANES - GATE_F0 - hh)]).reshape(1, GATE_LANES)
    return w, bias


def _layer_b_prompt(h_all, w_bf, bias, conv_w, norm_g, w_out, mem_prompt, w_mem, ln_g, ln_b, n_seq, seq_len,
                    tail):
    m = n_seq * seq_len
    tile = min(ROW_TILE, seq_len)
    chunk = min(MLSTM_CHUNK, seq_len)
    zstate = jnp.zeros((n_seq * V7X_SUBLANES, 2 * MLSTM_WIDTH), F32)
    q, kt, v, og, gt, gtt, mq, conv = _in_proj_b(h_all, 0, m, w_bf, conv_w, bias, [zstate] * 3, n_seq, seq_len,
                                                 tile, False)
    hh, e = MLSTM_HEADS, HEAD_DIM
    cell, c_out, n_out, m_out = _mlstm(q, kt, v, og, gt, gtt, norm_g, jnp.zeros((n_seq, hh, e, e), F32),
                                       jnp.zeros((n_seq, hh, e), F32), jnp.zeros((n_seq, hh), F32),
                                       n_seq, seq_len, chunk)
    mem_kv = _matmul(mem_prompt.reshape(n_seq * N_MEM, D_MODEL), w_mem, N_MEM)
    h1 = _fused_out([cell], [], mq, mem_kv.reshape(n_seq, N_MEM, 2 * MEM_WIDTH), h_all, w_out.astype(BF16),
                    ln_g, ln_b, tile, seq_len, tail)
    return h1, conv, c_out, n_out.reshape(n_seq, hh, e), m_out.reshape(n_seq, hh), mem_kv


def _layer_b_sample(h_all, h_off, w_bf, bias, conv_w, norm_g, w_out, mem_kv, conv_state, c0, n0, m0, ln_g, ln_b,
                    n_b, n_new):
    m = n_b * n_new
    w2 = 2 * MLSTM_WIDTH
    hh, e = MLSTM_HEADS, HEAD_DIM
    chunk = MLSTM_CHUNK
    t = jnp.arange(n_new)
    prevs = []
    for kk in (1, 2, 3):
        idx = jnp.clip(CONV_W - 1 - kk + t, 0, CONV_W - 2)
        prevs.append(conv_state[:, idx, :].reshape(m, w2))
    q, kt, v, og, gt, gtt, mq, u = _in_proj_b(h_all, h_off, m, w_bf, conv_w, bias, prevs, n_b, n_new, m, True)
    conv_new = jnp.concatenate([conv_state, u.reshape(n_b, n_new, w2)], axis=1)[:, n_new:]
    npad = chunk - n_new
    pad3 = lambda a: jnp.pad(a.reshape(n_b, n_new, a.shape[1]), ((0, 0), (0, npad), (0, 0)))
    gate_pad = jnp.where(jnp.arange(GATE_LANES) < hh, NEG_INF, 0.0).astype(F32)
    gtp = jnp.concatenate([gt.reshape(n_b, n_new, GATE_LANES),
                           jnp.broadcast_to(gate_pad, (n_b, npad, GATE_LANES))], axis=1)
    gttp = jnp.concatenate([gtt.reshape(GATE_LANES, n_b, n_new),
                            jnp.broadcast_to(gate_pad[:, None, None], (GATE_LANES, n_b, npad))], axis=2)
    ktp = jnp.pad(kt.reshape(MLSTM_WIDTH, n_b, n_new), ((0, 0), (0, 0), (0, npad)))
    flat = lambda a: a.reshape(n_b * chunk, a.shape[2])
    flat_t = lambda a: a.reshape(a.shape[0], n_b * chunk)
    cell, c_out, n_out, m_out = _mlstm(flat(pad3(q)), flat_t(ktp), flat(pad3(v)), flat(pad3(og)), flat(gtp),
                                       flat_t(gttp), norm_g, c0, n0, m0, n_b, chunk, chunk)
    cell = cell.reshape(n_b, chunk, MLSTM_WIDTH)[:, :n_new].reshape(m, MLSTM_WIDTH)
    mo = _sample_mem_attn(mq.astype(F32).reshape(n_b, n_new, MEM_WIDTH),
                          mem_kv.reshape(n_b, N_MEM, 2 * MEM_WIDTH), n_b, n_new)
    h1 = _out_proj(cell, mo.reshape(m, MEM_WIDTH), h_all, h_off, w_out, ln_g, ln_b, m)
    return h1, conv_new, c_out, n_out.reshape(n_b, hh, e), m_out.reshape(n_b, hh)


def _layer_a_sample(h2d, w_in, w_out, mem_kv, caches, ln_g, ln_b, n_b, n_new):
    m = n_b * n_new
    pos = PAST_LEN + (jnp.arange(m, dtype=jnp.int32) % n_new)
    outs = _in_proj_a(h2d, w_in, _rope_tables(pos), n_new, n_b, m, full_kv=True)
    qs, ks, vs, mq, kvs = outs[0:3], outs[3:6], outs[6:9], outs[9], outs[10:13]
    f3 = lambda t: t.reshape(n_b, n_new, t.shape[1])
    caches = [c.reshape(n_b, c.shape[1], 2 * DIL_WIDTH) for c in caches]
    mix = _sample_attn([f3(t) for t in qs], [f3(t) for t in ks], [f3(t) for t in vs], caches, n_b, n_new)
    mo = _sample_mem_attn(f3(mq), mem_kv.reshape(n_b, N_MEM, 2 * MEM_WIDTH), n_b, n_new)
    h1 = _out_proj(mix.reshape(m, DIL_WIDTH), mo.reshape(m, MEM_WIDTH), h2d, 0, w_out, ln_g, ln_b, m)
    return h1, kvs


def _layer_a_prompt(h2d, mem_prompt, w_in, w_out, w_mem, ln_g, ln_b, n_seq, seq_len, tail):
    tile = min(ROW_TILE, seq_len)
    tabs = _rope_tables(jnp.arange(seq_len, dtype=jnp.int32))
    outs = _in_proj_a(h2d, w_in.astype(BF16), tabs, seq_len, n_seq, tile, full_kv=False)
    qs, ks, vs, mq, kvs = outs[0:3], outs[3:6], outs[6:9], outs[9], outs[10:13]
    mem_kv = _matmul(mem_prompt.reshape(n_seq * N_MEM, D_MODEL), w_mem, N_MEM)
    os_, lses = [], []
    for g, (win, dil) in enumerate(DIL_PAIRS):
        o, lse = _dil_attn(qs[g], ks[g], vs[g], n_seq, seq_len, dil, win // dil)
        os_.append(o)
        lses.append(lse)
    h1 = _fused_out(os_, lses, mq, mem_kv.reshape(n_seq, N_MEM, 2 * MEM_WIDTH), h2d, w_out.astype(BF16),
                    ln_g, ln_b, tile, seq_len, tail)
    return h1, kvs, mem_kv


def kernel(x_prompt, x_sample, mem_prompt, cache_win1_kv, cache_win2_kv, cache_win3_kv, cache_mem_kv,
           state_mlstm_conv, state_mlstm_C, state_mlstm_n, state_mlstm_m,
           w_in_A, w_out_A, w_in_B, mlstm_conv_w, mlstm_b_if, mlstm_norm_g, w_out_B,
           w_mem_kv, ln_g, ln_b, w_grp, b_grp, w_exp, b_exp, w_gate, w_up, w_down):
    n_p, seq, d = x_prompt.shape
    n_s, t_new, _ = x_sample.shape
    assert d == D_MODEL and w_in_A.shape[0] == 1 and w_in_B.shape[0] == 1
    mp, ms = n_p * seq, n_s * t_new
    m_all = mp + ms
    assert mp % ROW_TILE == 0 and ms % SMALL_TILE == 0 and mp % ms == 0
    xp = x_prompt.reshape(mp, d)
    xs = x_sample.reshape(ms, d)

    wg_bf, wu_bf, wd_bf = w_gate.astype(BF16), w_up.astype(BF16), w_down.astype(BF16)

    def moe(h, i, parts):
        return _moe_layer(h, mp, w_grp[i], b_grp[i], w_exp[i], b_exp[i], wg_bf, wu_bf, wd_bf, i,
                          ln_g[i, 1], ln_b[i, 1], parts)

    caches = (cache_win1_kv[0], cache_win2_kv[0], cache_win3_kv[0])
    hs, kv_s = _layer_a_sample(xs, w_in_A[0], w_out_A[0], cache_mem_kv[0], caches, ln_g[0, 0], ln_b[0, 0],
                               n_s, t_new)
    h, kv_p, mem_kv0 = _layer_a_prompt(xp, mem_prompt, w_in_A[0], w_out_A[0], w_mem_kv[0], ln_g[0, 0], ln_b[0, 0],
                                       n_p, seq, hs)
    (h,) = moe(h, 0, [(0, m_all)])

    w_b, bias = _w_in_b_regroup(w_in_B[0], mlstm_b_if[0])
    w_bf = w_b.astype(BF16)
    hs, conv_s, c_s, nv_s, mm_s = _layer_b_sample(
        h, mp, w_b, bias, mlstm_conv_w[0], mlstm_norm_g[0], w_out_B[0], cache_mem_kv[1],
        state_mlstm_conv[0], state_mlstm_C[0], state_mlstm_n[0], state_mlstm_m[0],
        ln_g[1, 0], ln_b[1, 0], n_s, t_new)
    h1, conv_p, c_p, nv_p, mm_p, mem_kv1 = _layer_b_prompt(
        h, w_bf, bias, mlstm_conv_w[0], mlstm_norm_g[0], w_out_B[0], mem_prompt, w_mem_kv[1],
        ln_g[1, 0], ln_b[1, 0], n_p, seq, hs)
    yp, ys = moe(h1, 1, [(0, mp), (mp, ms)])

    kv_shape = (2, HEADS_PER_GROUP, HEAD_DIM)
    wins = []
    for g in range(N_DIL_GROUPS):
        wins.append(kv_p[g].reshape((1, n_p, -1) + kv_shape))
        wins.append(kv_s[g].reshape((1, n_s, t_new) + kv_shape))
    mem_kv_p = jnp.stack([mem_kv0, mem_kv1]).reshape((2, n_p, N_MEM, 2, MEM_HEADS, HEAD_DIM))
    return (yp.reshape(n_p, seq, d), ys.reshape(n_s, t_new, d), *wins,
            conv_p[None], conv_s[None], c_p[None], c_s[None], nv_p[None], nv_s[None], mm_p[None], mm_s[None],
            mem_kv_p)
```

```python
import functools

import jax
import jax.numpy as jnp
import numpy as np
from jax import lax
from jax.experimental import pallas as pl
from jax.experimental.pallas import tpu as pltpu

D_MODEL = 1024
HEAD_DIM = 64
ATTN_SCALE = HEAD_DIM ** -0.5
PAST_LEN = 16384
N_MEM = 256
MEM_HEADS = 4
MEM_WIDTH = MEM_HEADS * HEAD_DIM
DIL_PAIRS = ((128, 1), (512, 4), (2048, 16))
N_DIL_GROUPS = len(DIL_PAIRS)
HEADS_PER_GROUP = 4
DIL_WIDTH = HEADS_PER_GROUP * HEAD_DIM
QBLK = 128
ROPE_THETA = 500000.0
ROPE_DIMS = HEAD_DIM // 4
MLSTM_HEADS = 12
MLSTM_WIDTH = MLSTM_HEADS * HEAD_DIM
CONV_W = 4
N_GROUPS = 4
EXPERTS_PER_GROUP = 8
N_EXPERTS = N_GROUPS * EXPERTS_PER_GROUP
D_EXPERT = 256
DEPTH = 2
ALPHA = (2 * DEPTH) ** 0.25
LN_EPS = 1e-5

V7X_LANES = 128
V7X_SUBLANES = 8
V7X_VMEM_BYTES = 64 * 1024 * 1024
VMEM_LIMIT = 48 * 1024 * 1024

ROW_TILE = 512
SMALL_TILE = 128
MLSTM_CHUNK = 128
EXPERT_ROWS = 256

BF16 = jnp.bfloat16
F32 = jnp.float32
NEG_INF = float("-inf")


def _cparams(n_axes, vmem=None, flags=None):
    return pltpu.CompilerParams(dimension_semantics=("arbitrary",) * n_axes,
                                vmem_limit_bytes=vmem, flags=flags)


def _nt_dot(a, b):
    return lax.dot_general(a, b, (((1,), (1,)), ((), ())), preferred_element_type=F32)


def _dot(a, b):
    return jnp.dot(a, b, preferred_element_type=F32)


def _split_bf16(x):
    hi = x.astype(BF16)
    lo = (x - hi.astype(F32)).astype(BF16)
    return hi, lo


def _lhs(x, precise):
    return _split_bf16(x) if precise else (x.astype(BF16),)


def _mm(lhs, w):
    if len(lhs) == 1:
        return _dot(lhs[0], w)
    wh, wl = _split_bf16(w)
    return _dot(lhs[0], wh) + (_dot(lhs[0], wl) + _dot(lhs[1], wh))


def _rope_tables(pos):
    half = ROPE_DIMS // 2
    inv = jnp.power(ROPE_THETA, -jnp.arange(half, dtype=F32) * (2.0 / ROPE_DIMS))
    ang = pos.astype(F32)[:, None] * inv[None, :]
    cos, sin = jnp.cos(ang), jnp.sin(ang)
    n = pos.shape[0]
    one = jnp.ones((n, HEAD_DIM - ROPE_DIMS), F32)
    zero8 = jnp.zeros((n, half), F32)
    zrest = jnp.zeros((n, HEAD_DIM - ROPE_DIMS), F32)
    a = jnp.concatenate([cos, cos, one], axis=1)
    b = jnp.concatenate([zero8, sin, zrest], axis=1)
    c = jnp.concatenate([-sin, zero8, zrest], axis=1)
    rep = V7X_LANES // HEAD_DIM
    return jnp.tile(a, (1, rep)), jnp.tile(b, (1, rep)), jnp.tile(c, (1, rep))


def _rope_apply(x, ra, rb, rc):
    parts = []
    for s in range(x.shape[1] // V7X_LANES):
        v = x[:, s * V7X_LANES:(s + 1) * V7X_LANES]
        parts.append(v * ra + pltpu.roll(v, ROPE_DIMS // 2, 1) * rb
                     + pltpu.roll(v, V7X_LANES - ROPE_DIMS // 2, 1) * rc)
    return jnp.concatenate(parts, axis=1)


def _in_proj_a_body(x_ref, w_ref, ra_ref, rb_ref, rc_ref, *outs, kv_rows, precise):
    q_refs, k_refs, v_refs = outs[0:3], outs[3:6], outs[6:9]
    mq_ref = outs[9]
    kv_refs = outs[10:13]
    lhs = _lhs(x_ref[...], precise)
    ra, rb, rc = ra_ref[...], rb_ref[...], rc_ref[...]
    tm = x_ref.shape[0]
    gw = DIL_WIDTH
    for g in range(N_DIL_GROUPS):
        q = _rope_apply(_mm(lhs, w_ref[:, g * gw:(g + 1) * gw]), ra, rb, rc)
        k = _rope_apply(_mm(lhs, w_ref[:, (3 + g) * gw:(4 + g) * gw]), ra, rb, rc)
        v = _mm(lhs, w_ref[:, (6 + g) * gw:(7 + g) * gw])
        q_refs[g][...] = (q * ATTN_SCALE).astype(q_refs[g].dtype)
        k_refs[g][...] = k.astype(k_refs[g].dtype)
        v_refs[g][...] = v.astype(v_refs[g].dtype)
        r = kv_rows[g]
        kv_refs[g][:, 0:gw] = k[tm - r:, :]
        kv_refs[g][:, gw:2 * gw] = v[tm - r:, :]
    mq = _mm(lhs, w_ref[:, 9 * gw:9 * gw + MEM_WIDTH])
    mq_ref[...] = (mq * ATTN_SCALE).astype(mq_ref.dtype)


def _in_proj_a(x2d, w, tabs, seq_len, n_seq, tile, full_kv):
    w_bf = w
    act = F32 if full_kv else BF16
    m = x2d.shape[0]
    nt = m // tile
    gw = DIL_WIDTH
    row_spec = lambda w: pl.BlockSpec((tile, w), lambda i: (i, 0))
    if full_kv:
        tab_spec = pl.BlockSpec((tile, V7X_LANES), lambda i: (i, 0))
        kv_rows = (tile,) * 3
        kv_shapes = [jax.ShapeDtypeStruct((m, 2 * gw), F32)] * 3
        kv_specs = [row_spec(2 * gw)] * 3
    else:
        tps = seq_len // tile
        tab_spec = pl.BlockSpec((tile, V7X_LANES), lambda i: (i % tps, 0))
        kv_rows, kv_shapes, kv_specs = [], [], []
        for win, _ in DIL_PAIRS:
            wb = min(win, seq_len)
            r = min(wb, tile)
            nblk = wb // r
            kv_rows.append(r)
            kv_shapes.append(jax.ShapeDtypeStruct((n_seq * wb, 2 * gw), F32))
            kv_specs.append(pl.BlockSpec(
                (r, 2 * gw),
                lambda i, nblk=nblk: ((i // tps) * nblk + jnp.maximum(i % tps - (tps - nblk), 0), 0)))
        kv_rows = tuple(kv_rows)
    out_shapes = [jax.ShapeDtypeStruct((m, gw), act)] * 9 + [jax.ShapeDtypeStruct((m, MEM_WIDTH), act)] + kv_shapes
    out_specs = [row_spec(gw)] * 9 + [row_spec(MEM_WIDTH)] + kv_specs
    return pl.pallas_call(
        functools.partial(_in_proj_a_body, kv_rows=kv_rows, precise=full_kv),
        grid=(nt,),
        in_specs=[row_spec(D_MODEL), pl.BlockSpec(w_bf.shape, lambda i: (0, 0)), tab_spec, tab_spec, tab_spec],
        out_specs=out_specs,
        out_shape=out_shapes,
        compiler_params=_cparams(1, VMEM_LIMIT),
        name="in_proj_a",
    )(x2d, w_bf, *tabs)


def _dil_attn_body(q_ref, k_ref, v_ref, kp_ref, vp_ref, o_ref, lse_ref, *, span):
    i = pl.program_id(2)
    q, k, v = q_ref[...], k_ref[...], v_ref[...]
    kp, vp = kp_ref[...], vp_ref[...]
    qi = lax.broadcasted_iota(jnp.int32, (QBLK, 2 * QBLK), 0) + QBLK
    ki = lax.broadcasted_iota(jnp.int32, (QBLK, 2 * QBLK), 1)
    band = (qi >= ki) & (qi - ki <= span) & ((i > 0) | (ki >= QBLK))
    lses = []
    for h in range(HEADS_PER_GROUP):
        hs = slice(h * HEAD_DIM, (h + 1) * HEAD_DIM)
        kc = jnp.concatenate([kp[:, hs], k[:, hs]], axis=0)
        vc = jnp.concatenate([vp[:, hs], v[:, hs]], axis=0)
        s = jnp.where(band, _nt_dot(q[:, hs], kc), NEG_INF)
        m = jnp.max(s, axis=1, keepdims=True)
        p = jnp.exp(s - m)
        den = jnp.sum(p, axis=1, keepdims=True)
        o_ref[:, hs] = _dot((p / den).astype(BF16), vc).astype(o_ref.dtype)
        lses.append(m + jnp.log(den))
    lse_ref[...] = jnp.concatenate(lses, axis=1)


def _dil_attn(q, k, v, n_seq, seq_len, dil, span):
    gw = DIL_WIDTH
    L = seq_len // dil
    nb = L // QBLK
    view = lambda t: t.reshape(n_seq, L, dil * gw)
    blk = pl.BlockSpec((None, QBLK, gw), lambda b, r, i: (b, i, r))
    prev = pl.BlockSpec((None, QBLK, gw), lambda b, r, i: (b, jnp.maximum(i - 1, 0), r))
    o, lse = pl.pallas_call(
        functools.partial(_dil_attn_body, span=span),
        grid=(n_seq, dil, nb),
        in_specs=[blk, blk, blk, prev, prev],
        out_specs=[blk, pl.BlockSpec((None, None, QBLK, HEADS_PER_GROUP), lambda b, r, i: (b, r, i, 0))],
        out_shape=[jax.ShapeDtypeStruct((n_seq, L, dil * gw), BF16),
                   jax.ShapeDtypeStruct((n_seq, dil, L, HEADS_PER_GROUP), F32)],
        compiler_params=_cparams(3),
        name=f"dil_attn_d{dil}",
    )(view(q), view(k), view(v), view(k), view(v))
    o = o.reshape(n_seq * seq_len, gw)
    lse = jnp.transpose(lse, (0, 2, 1, 3)).reshape(n_seq * seq_len, HEADS_PER_GROUP)
    return o, lse


def _layer_norm_rows(x, g, b):
    mu = jnp.mean(x, axis=1, keepdims=True)
    xc = x - mu
    var = jnp.mean(xc * xc, axis=1, keepdims=True)
    return xc * lax.rsqrt(var + LN_EPS) * g + b


def _out_proj_body(a_ref, mo_ref, h_ref, w_ref, g_ref, b_ref, o_ref, *, precise):
    ka = a_ref.shape[1]
    y = _mm(_lhs(a_ref[...], precise), w_ref[0:ka, :]) + _mm(_lhs(mo_ref[...], precise), w_ref[ka:, :])
    o_ref[...] = _layer_norm_rows(ALPHA * h_ref[...] + y, g_ref[...], b_ref[...])


def _out_proj(a, mo, h, h_off, w, g, b, tile):
    m, ka = a.shape
    row = lambda width, off=0: pl.BlockSpec((tile, width), lambda i: (i + off // tile, 0))
    const = lambda s: pl.BlockSpec(s, lambda i: (0, 0))
    return pl.pallas_call(
        functools.partial(_out_proj_body, precise=(w.dtype == F32)),
        grid=(m // tile,),
        in_specs=[row(ka), row(MEM_WIDTH), row(D_MODEL, h_off), const(w.shape), const((1, D_MODEL)),
                  const((1, D_MODEL))],
        out_specs=row(D_MODEL),
        out_shape=jax.ShapeDtypeStruct((m, D_MODEL), F32),
        compiler_params=_cparams(1, VMEM_LIMIT),
        name="out_proj_ln",
    )(a, mo, h, w, g.reshape(1, D_MODEL), b.reshape(1, D_MODEL))


def _fused_out_body(*refs, n_parts, n_tiles, has_tail):
    parts = refs[0:n_parts]
    k = n_parts
    lses = refs[k:k + n_parts] if n_parts > 1 else ()
    k += len(lses)
    mq_ref, kv_ref, h_ref, w_ref, g_ref, b_ref = refs[k:k + 6]
    tail_ref = refs[k + 6] if has_tail else None
    o_ref = refs[-1]

    def rows():
        if n_parts == 1:
            a = parts[0][...]
        else:
            ls = [l[...] for l in lses]
            heads = []
            for h in range(HEADS_PER_GROUP):
                hs = slice(h * HEAD_DIM, (h + 1) * HEAD_DIM)
                lh = [l[:, h:h + 1] for l in ls]
                mx = jnp.maximum(jnp.maximum(lh[0], lh[1]), lh[2])
                e = [jnp.exp(x - mx) for x in lh]
                tot = e[0] + e[1] + e[2]
                acc = (e[0] / tot) * parts[0][:, hs].astype(F32)
                acc = acc + (e[1] / tot) * parts[1][:, hs].astype(F32)
                acc = acc + (e[2] / tot) * parts[2][:, hs].astype(F32)
                heads.append(acc)
            a = jnp.concatenate(heads, axis=1).astype(BF16)
        ka = a.shape[1]
        q = mq_ref[...]
        kv = kv_ref[...].astype(BF16)
        mos = []
        for h in range(MEM_HEADS):
            hs = slice(h * HEAD_DIM, (h + 1) * HEAD_DIM)
            vs = slice(MEM_WIDTH + h * HEAD_DIM, MEM_WIDTH + (h + 1) * HEAD_DIM)
            s = _nt_dot(q[:, hs], kv[:, hs])
            m = jnp.max(s, axis=1, keepdims=True)
            p = jnp.exp(s - m)
            den = jnp.sum(p, axis=1, keepdims=True)
            mos.append(_dot((p / den).astype(BF16), kv[:, vs]))
        mo = jnp.concatenate(mos, axis=1).astype(BF16)
        y = _dot(a, w_ref[0:ka, :]) + _dot(mo, w_ref[ka:, :])
        o_ref[...] = _layer_norm_rows(ALPHA * h_ref[...] + y, g_ref[...], b_ref[...])

    if not has_tail:
        rows()
    else:
        pl.when(pl.program_id(0) < n_tiles)(rows)

        @pl.when(pl.program_id(0) == n_tiles)
        def _():
            o_ref[0:tail_ref.shape[0], :] = tail_ref[...]


def _fused_out(parts, lses, mq, mem_kv, h, w_bf, g, b, tile, seq_len, tail):
    m = parts[0].shape[0]
    nt = m // tile
    last = nt - 1
    tps = seq_len // tile
    clamp = lambda i: jnp.minimum(i, last)
    row = lambda w: pl.BlockSpec((tile, w), lambda i: (clamp(i), 0))
    const = lambda s: pl.BlockSpec(s, lambda i: (0,) * len(s))
    in_specs = [row(p.shape[1]) for p in parts] + [row(l.shape[1]) for l in lses]
    in_specs += [row(MEM_WIDTH), pl.BlockSpec((None, N_MEM, 2 * MEM_WIDTH), lambda i: (clamp(i) // tps, 0, 0)),
                 row(D_MODEL), const(w_bf.shape), const((1, D_MODEL)), const((1, D_MODEL))]
    args = list(parts) + list(lses) + [mq, mem_kv, h, w_bf, g.reshape(1, D_MODEL), b.reshape(1, D_MODEL)]
    out_rows, steps = m, nt
    if tail is not None:
        assert tail.shape[0] <= tile
        in_specs.append(const(tail.shape))
        args.append(tail)
        out_rows, steps = m + tail.shape[0], nt + 1
    return pl.pallas_call(
        functools.partial(_fused_out_body, n_parts=len(parts), n_tiles=nt, has_tail=tail is not None),
        grid=(steps,),
        in_specs=in_specs,
        out_specs=pl.BlockSpec((tile, D_MODEL), lambda i: (i, 0)),
        out_shape=jax.ShapeDtypeStruct((out_rows, D_MODEL), F32),
        compiler_params=_cparams(1, VMEM_LIMIT),
        name="attn_out_proj_ln",
    )(*args)


def _matmul_body(x_ref, w_ref, o_ref):
    o_ref[...] = _dot(x_ref[...].astype(BF16), w_ref[...].astype(BF16))


def _matmul(x, w, tile):
    m, k = x.shape
    n = w.shape[1]
    return pl.pallas_call(
        _matmul_body,
        grid=(m // tile,),
        in_specs=[pl.BlockSpec((tile, k), lambda i: (i, 0)), pl.BlockSpec((k, n), lambda i: (0, 0))],
        out_specs=pl.BlockSpec((tile, n), lambda i: (i, 0)),
        out_shape=jax.ShapeDtypeStruct((m, n), F32),
        compiler_params=_cparams(1),
        name="mem_kv_proj",
    )(x, w)


def _col_attend(q_row, kmat, vmat, kmask, knew, vnew, nmask):
    pk = kmat * q_row
    pn = None if knew is None else knew * q_row
    outs, lses = [], []
    for h in range(HEADS_PER_GROUP):
        hs = slice(h * HEAD_DIM, (h + 1) * HEAD_DIM)
        s = jnp.sum(pk[:, hs], axis=1, keepdims=True)
        if kmask is not None:
            s = jnp.where(kmask, s, NEG_INF)
        m = jnp.max(s, axis=0, keepdims=True)
        if pn is not None:
            sn = jnp.where(nmask, jnp.sum(pn[:, hs], axis=1, keepdims=True), NEG_INF)
            m = jnp.maximum(m, jnp.max(sn, axis=0, keepdims=True))
        p = jnp.exp(s - m)
        den = jnp.sum(p, axis=0, keepdims=True)
        acc = jnp.sum(p * vmat[:, hs], axis=0, keepdims=True)
        if pn is not None:
            pnw = jnp.exp(sn - m)
            den = den + jnp.sum(pnw, axis=0, keepdims=True)
            acc = acc + jnp.sum(pnw * vnew[:, hs], axis=0, keepdims=True)
        outs.append(acc / den)
        lses.append(m + jnp.log(den))
    return outs, lses


def _sample_attn_body(q1, q2, q3, kn1, kn2, kn3, vn1, vn2, vn3, c1, c2, c3, mix_ref, *, n_new):
    qs = (q1[...], q2[...], q3[...])
    kns = (kn1[...], kn2[...], kn3[...])
    vns = (vn1[...], vn2[...], vn3[...])
    gw = DIL_WIDTH
    t_idx = lax.broadcasted_iota(jnp.int32, (n_new, 1), 0)
    rows = []
    for t in range(n_new):
        per_group = []
        for g, (win, dil) in enumerate(DIL_PAIRS):
            q_row = qs[g][t:t + 1, :]
            cache = (c1, c2, c3)[g]
            if dil == 1:
                kmat, vmat = cache[:, 0:gw], cache[:, gw:2 * gw]
                r_idx = lax.broadcasted_iota(jnp.int32, (kmat.shape[0], 1), 0)
                per_group.append(_col_attend(q_row, kmat, vmat, r_idx >= t, kns[g], vns[g], t_idx <= t))
            else:
                base = t * 2 * gw
                kmat, vmat = cache[:, base:base + gw], cache[:, base + gw:base + 2 * gw]
                per_group.append(_col_attend(q_row, kmat, vmat, None, kns[g], vns[g], t_idx == t))
        heads = []
        for h in range(HEADS_PER_GROUP):
            lh = [per_group[g][1][h] for g in range(N_DIL_GROUPS)]
            mx = jnp.maximum(jnp.maximum(lh[0], lh[1]), lh[2])
            e = [jnp.exp(x - mx) for x in lh]
            tot = e[0] + e[1] + e[2]
            acc = (e[0] / tot) * per_group[0][0][h]
            acc = acc + (e[1] / tot) * per_group[1][0][h]
            acc = acc + (e[2] / tot) * per_group[2][0][h]
            heads.append(acc)
        rows.append(jnp.concatenate(heads, axis=1))
    mix_ref[...] = jnp.concatenate(rows, axis=0)


def _sample_attn(qs, kns, vns, caches, n_b, n_new):
    gw = DIL_WIDTH
    small = pl.BlockSpec((None, n_new, gw), lambda b: (b, 0, 0))
    cviews, cspecs = [], []
    for c, (win, dil) in zip(caches, DIL_PAIRS):
        blocks = win // dil
        cviews.append(c.reshape(n_b, blocks, dil * 2 * gw))
        lanes = min(dil, n_new) * 2 * gw
        cspecs.append(pl.BlockSpec((None, blocks, lanes), lambda b: (b, 0, 0)))
    return pl.pallas_call(
        functools.partial(_sample_attn_body, n_new=n_new),
        grid=(n_b,),
        in_specs=[small] * 9 + cspecs,
        out_specs=small,
        out_shape=jax.ShapeDtypeStruct((n_b, n_new, gw), F32),
        compiler_params=_cparams(1, VMEM_LIMIT),
        name="sample_dil_attn",
    )(*qs, *kns, *vns, *cviews)


def _sample_mem_attn_body(q_ref, kv_ref, o_ref, *, n_new):
    q = q_ref[...]
    kmat, vmat = kv_ref[:, 0:MEM_WIDTH], kv_ref[:, MEM_WIDTH:2 * MEM_WIDTH]
    rows = []
    for t in range(n_new):
        outs, _ = _col_attend(q[t:t + 1, :], kmat, vmat, None, None, None, None)
        rows.append(jnp.concatenate(outs, axis=1))
    o_ref[...] = jnp.concatenate(rows, axis=0)


def _sample_mem_attn(q, mem_kv, n_b, n_new):
    small = pl.BlockSpec((None, n_new, MEM_WIDTH), lambda b: (b, 0, 0))
    return pl.pallas_call(
        functools.partial(_sample_mem_attn_body, n_new=n_new),
        grid=(n_b,),
        in_specs=[small, pl.BlockSpec((None, N_MEM, 2 * MEM_WIDTH), lambda b: (b, 0, 0))],
        out_specs=small,
        out_shape=jax.ShapeDtypeStruct((n_b, n_new, MEM_WIDTH), F32),
        compiler_params=_cparams(1),
        name="sample_mem_attn",
    )(q, mem_kv)


ROUTE_LANES = V7X_LANES


def _router_body(x_ref, w_ref, b_ref, cnt0_ref, o_ref, cnt_ref):
    xh, xl = _split_bf16(x_ref[...])
    wh, wl = _split_bf16(w_ref[...])
    logits = _dot(xh, wh) + (_dot(xh, wl) + _dot(xl, wh)) + b_ref[...]
    lane = lax.broadcasted_iota(jnp.int32, logits.shape, 1)
    big = jnp.int32(ROUTE_LANES)
    is_grp = (lane >= N_EXPERTS) & (lane < N_EXPERTS + N_GROUPS)
    gl = jnp.where(is_grp, logits, NEG_INF)
    gmax = jnp.max(gl, axis=1, keepdims=True)
    gsel = jnp.min(jnp.where(gl == gmax, lane, big), axis=1, keepdims=True) - N_EXPERTS
    gp = 1.0 / jnp.sum(jnp.exp(gl - gmax), axis=1, keepdims=True)
    in_grp = (lane < N_EXPERTS) & ((lane // EXPERTS_PER_GROUP) == gsel)
    el = jnp.where(in_grp, logits, NEG_INF)
    v1 = jnp.max(el, axis=1, keepdims=True)
    i1 = jnp.min(jnp.where(el == v1, lane, big), axis=1, keepdims=True)
    el2 = jnp.where(lane == i1, NEG_INF, el)
    v2 = jnp.max(el2, axis=1, keepdims=True)
    i2 = jnp.min(jnp.where(el2 == v2, lane, big), axis=1, keepdims=True)
    e2 = jnp.exp(v2 - v1)
    w1 = (1.0 / (1.0 + e2)) * gp
    w2 = (e2 / (1.0 + e2)) * gp
    @pl.when(pl.program_id(0) == 0)
    def _():
        cnt_ref[...] = cnt0_ref[...]

    tm = logits.shape[0]
    oh1 = (lane == i1).astype(F32)
    oh2 = (lane == i2).astype(F32)
    tri = (lax.broadcasted_iota(jnp.int32, (tm, tm), 0) > lax.broadcasted_iota(jnp.int32, (tm, tm), 1)).astype(BF16)
    base = cnt_ref[0:1, :]
    c1 = jnp.sum(oh1, axis=0, keepdims=True)
    c2 = jnp.sum(oh2, axis=0, keepdims=True)
    r1 = jnp.sum(oh1 * (_dot(tri, oh1.astype(BF16)) + base), axis=1, keepdims=True)
    r2 = jnp.sum(oh2 * (_dot(tri, oh2.astype(BF16)) + (base + c1)), axis=1, keepdims=True)
    cnt_ref[...] = jnp.broadcast_to(base + c1 + c2, cnt_ref.shape)
    out = jnp.where(lane == 0, i1.astype(F32), 0.0)
    out = jnp.where(lane == 1, i2.astype(F32), out)
    out = jnp.where(lane == 2, w1, out)
    out = jnp.where(lane == 3, w2, out)
    out = jnp.where(lane == 4, r1, out)
    out = jnp.where(lane == 5, r2, out)
    o_ref[...] = out


def _router(x, row_off, n_rows, w_grp, b_grp, w_exp, b_exp, tile, counts0):
    pad = ROUTE_LANES - N_EXPERTS - N_GROUPS
    w = jnp.concatenate([w_exp, w_grp, jnp.zeros((D_MODEL, pad), F32)], axis=1)
    b = jnp.concatenate([b_exp, b_grp, jnp.zeros((pad,), F32)]).reshape(1, ROUTE_LANES)
    ob = row_off // tile
    const = lambda s: pl.BlockSpec(s, lambda i: (0, 0))
    cshape = (V7X_SUBLANES, ROUTE_LANES)
    return pl.pallas_call(
        _router_body,
        grid=(n_rows // tile,),
        in_specs=[pl.BlockSpec((tile, D_MODEL), lambda i: (i + ob, 0)), const(w.shape), const(b.shape), const(cshape)],
        out_specs=[pl.BlockSpec((tile, ROUTE_LANES), lambda i: (i, 0)), const(cshape)],
        out_shape=[jax.ShapeDtypeStruct((n_rows, ROUTE_LANES), F32), jax.ShapeDtypeStruct(cshape, F32)],
        compiler_params=_cparams(1, VMEM_LIMIT),
        name="moe_router",
    )(x, w, b, counts0)


DMA_UNROLL = True


def _dispatch_body(pos_ref, x_ref, xs_hbm, xbuf, sem, *, nt):
    t = pl.program_id(0)
    s = t % 2
    tile = x_ref.shape[0]
    xbuf[s] = x_ref[...]

    def body(j, c):
        src = xbuf.at[s, pl.ds(j, 1)]
        pltpu.make_async_copy(src, xs_hbm.at[pl.ds(pos_ref[0, 0, j], 1)], sem.at[s]).start(priority=0)
        pltpu.make_async_copy(src, xs_hbm.at[pl.ds(pos_ref[0, 0, tile + j], 1)], sem.at[s]).start(priority=1)
        return c
    lax.fori_loop(0, tile, body, 0, unroll=DMA_UNROLL)

    def drain(slot):
        for _ in range(2):
            pltpu.make_async_copy(xbuf.at[slot], xs_hbm.at[pl.ds(0, tile)], sem.at[slot]).wait()

    @pl.when(t >= 1)
    def _():
        drain(1 - s)

    @pl.when(t == nt - 1)
    def _():
        drain(s)


def _dispatch(x, pos, tile):
    m = x.shape[0]
    return pl.pallas_call(
        functools.partial(_dispatch_body, nt=m // tile),
        grid=(m // tile,),
        in_specs=[pl.BlockSpec((1, 1, 2 * tile), lambda i: (i, 0, 0), memory_space=pltpu.SMEM),
                  pl.BlockSpec((tile, D_MODEL), lambda i: (i, 0))],
        out_specs=pl.BlockSpec(memory_space=pl.ANY),
        out_shape=jax.ShapeDtypeStruct((2 * m, D_MODEL), F32),
        scratch_shapes=[pltpu.VMEM((2, tile, D_MODEL), F32), pltpu.SemaphoreType.DMA((2,))],
        compiler_params=_cparams(1),
        name="moe_dispatch",
    )(pos, x)


def _expert_body(it_ref, ie_ref, lo_ref, hi_ref, x_ref, wg_ref, wu_ref, wd_ref, y_ref, acc, *, n_items):
    w = pl.program_id(0)
    tile = it_ref[w]
    lo, hi = lo_ref[w], hi_ref[w]
    first = (w == 0) | (it_ref[jnp.maximum(w - 1, 0)] != tile)
    last = (w == n_items - 1) | (it_ref[jnp.minimum(w + 1, n_items - 1)] != tile)

    @pl.when(first)
    def _():
        acc[...] = jnp.zeros_like(acc)

    @pl.when(lo < hi)
    def _():
        xb = x_ref[...].astype(BF16)
        hg = _dot(xb, wg_ref[...])
        hu = _dot(xb, wu_ref[...])
        he = (hg * jax.nn.sigmoid(hg)) * hu
        y = _dot(he.astype(BF16), wd_ref[...])
        row = tile * EXPERT_ROWS + lax.broadcasted_iota(jnp.int32, (EXPERT_ROWS, 1), 0)
        acc[...] += jnp.where((row >= lo) & (row < hi), y, 0.0)

    @pl.when(last)
    def _():
        y_ref[...] = acc[...]


def _experts(xs, items, w_gate, w_up, w_down, layer):
    n_items = items[0].shape[0]
    rows = EXPERT_ROWS
    tspec = pl.BlockSpec((rows, D_MODEL), lambda w, it, ie, lo, hi: (it[w], 0))
    wspec = lambda shp: pl.BlockSpec((None, None) + shp, lambda w, it, ie, lo, hi: (layer, ie[w], 0, 0))
    grid_spec = pltpu.PrefetchScalarGridSpec(
        num_scalar_prefetch=4,
        grid=(n_items,),
        in_specs=[tspec, wspec((D_MODEL, D_EXPERT)), wspec((D_MODEL, D_EXPERT)), wspec((D_EXPERT, D_MODEL))],
        out_specs=tspec,
        scratch_shapes=[pltpu.VMEM((rows, D_MODEL), F32)])
    return pl.pallas_call(
        functools.partial(_expert_body, n_items=n_items),
        grid_spec=grid_spec,
        out_shape=jax.ShapeDtypeStruct(xs.shape, F32),
        compiler_params=_cparams(1, VMEM_LIMIT),
        name="moe_experts",
    )(*items, xs, w_gate, w_up, w_down)


def _moe_combine_body(pos_ref, h_ref, r_ref, g_ref, b_ref, ys_hbm, o_ref, ybuf, sem, *, nt):
    t = pl.program_id(0)
    tile = h_ref.shape[0]

    @pl.when(t < nt)
    def _():
        s = t % 2

        def body(j, c):
            for k in range(2):
                jj = j + k * tile
                pltpu.make_async_copy(ys_hbm.at[pl.ds(pos_ref[0, 0, jj], 1)], ybuf.at[s, pl.ds(jj, 1)],
                                      sem.at[s]).start(priority=k)
            return c
        lax.fori_loop(0, tile, body, 0, unroll=DMA_UNROLL)

    @pl.when(t >= 1)
    def _():
        s = (t + 1) % 2
        pltpu.make_async_copy(ys_hbm.at[pl.ds(0, 2 * tile)], ybuf.at[s], sem.at[s]).wait()
        r = r_ref[...]
        moe = r[:, 2:3] * ybuf[s, 0:tile, :] + r[:, 3:4] * ybuf[s, tile:2 * tile, :]
        o_ref[...] = _layer_norm_rows(ALPHA * h_ref[...] + moe, g_ref[...], b_ref[...])


def _moe_combine(h, ys, pos, route, g, b, tile, row_off, n_rows):
    ob = row_off // tile
    nt = n_rows // tile
    done = lambda i: jnp.maximum(i - 1, 0)
    row = lambda w: pl.BlockSpec((tile, w), lambda i: (done(i) + ob, 0))
    const = lambda s: pl.BlockSpec(s, lambda i: (0, 0))
    return pl.pallas_call(
        functools.partial(_moe_combine_body, nt=nt),
        grid=(nt + 1,),
        in_specs=[pl.BlockSpec((1, 1, 2 * tile), lambda i: (jnp.minimum(i, nt - 1) + ob, 0, 0),
                               memory_space=pltpu.SMEM),
                  row(D_MODEL), row(ROUTE_LANES), const((1, D_MODEL)), const((1, D_MODEL)),
                  pl.BlockSpec(memory_space=pl.ANY)],
        out_specs=pl.BlockSpec((tile, D_MODEL), lambda i: (done(i), 0)),
        out_shape=jax.ShapeDtypeStruct((n_rows, D_MODEL), F32),
        scratch_shapes=[pltpu.VMEM((2, 2 * tile, D_MODEL), F32), pltpu.SemaphoreType.DMA((2,))],
        compiler_params=_cparams(1, VMEM_LIMIT),
        name="moe_combine_ln",
    )(pos, h, route, g.reshape(1, D_MODEL), b.reshape(1, D_MODEL), ys)


def _dispatch_plan(route, counts, m, tile):
    rows = EXPERT_ROWS
    nt = (2 * m) // rows
    cnt = counts[0, 0:N_EXPERTS].astype(jnp.int32)
    end = jnp.cumsum(cnt)
    start = end - cnt
    e = route[:, 0:2].astype(jnp.int32)
    rank = route[:, 4:6].astype(jnp.int32)
    onehot = (e[:, :, None] == jnp.arange(N_EXPERTS, dtype=jnp.int32)[None, None, :])
    pos = jnp.sum(jnp.where(onehot, start[None, None, :], 0), axis=2) + rank
    pos = jnp.transpose(pos.reshape(m // tile, tile, 2), (0, 2, 1)).reshape(m // tile, 1, 2 * tile)
    first_t = start // rows
    n_it = jnp.where(cnt > 0, (end - 1) // rows - first_t + 1, 0)
    it_end = jnp.cumsum(n_it)
    n_items = nt + N_EXPERTS
    w = jnp.arange(n_items, dtype=jnp.int32)
    ie = jnp.minimum(jnp.sum((it_end[None, :] <= w[:, None]).astype(jnp.int32), axis=1), N_EXPERTS - 1)
    pick = lambda a: jnp.sum(jnp.where(ie[:, None] == jnp.arange(N_EXPERTS)[None, :], a[None, :], 0), axis=1)
    valid = w < it_end[-1]
    it = pick(first_t) + (w - (pick(it_end) - pick(n_it)))
    it = jnp.where(valid, it, nt - 1).astype(jnp.int32)
    last_e = jnp.sum(jnp.where(w == it_end[-1] - 1, ie, 0))
    ie = jnp.where(valid, ie, last_e).astype(jnp.int32)
    lo = jnp.where(valid, pick(start), 0).astype(jnp.int32)
    hi = jnp.where(valid, pick(end), 0).astype(jnp.int32)
    return pos.astype(jnp.int32), (it, ie, lo, hi)


def _moe_layer(h, n_big, w_grp, b_grp, w_exp, b_exp, w_gate, w_up, w_down, layer, g, b, parts):
    m = h.shape[0]
    tile = SMALL_TILE
    assert m % tile == 0 and (2 * m) % EXPERT_ROWS == 0 and n_big % ROW_TILE == 0
    zero_counts = jnp.zeros((V7X_SUBLANES, ROUTE_LANES), F32)
    route_a, counts = _router(h, 0, n_big, w_grp, b_grp, w_exp, b_exp, ROW_TILE, zero_counts)
    route_b, counts = _router(h, n_big, m - n_big, w_grp, b_grp, w_exp, b_exp, tile, counts)
    route = jnp.concatenate([route_a, route_b], axis=0)
    pos, items = _dispatch_plan(route, counts, m, tile)
    xs = _dispatch(h, pos, tile)
    ys = _experts(xs, items, w_gate, w_up, w_down, layer)
    return [_moe_combine(h, ys, pos, route, g, b, tile, off, n) for off, n in parts]


GATE_LANES = V7X_LANES
GATE_F0 = 2 * V7X_SUBLANES


def _log_sigmoid(x):
    return jnp.minimum(x, 0.0) - jnp.log1p(jnp.exp(-jnp.abs(x)))


def _silu(x):
    return x * jax.nn.sigmoid(x)


def _in_proj_b_body(x_ref, w_ref, cw_ref, bias_ref, p1_ref, p2_ref, p3_ref,
                    q_ref, kt_ref, v_ref, og_ref, gt_ref, gtt_ref, mq_ref, u_ref, carry, *, seq_rows):
    w = MLSTM_WIDTH
    lhs = _lhs(x_ref[...], w_ref.dtype == F32)
    tm = x_ref.shape[0]
    u = _mm(lhs, w_ref[:, 0:2 * w])
    row = lax.broadcasted_iota(jnp.int32, (tm, 1), 0)
    if seq_rows is None:
        @pl.when(pl.program_id(1) == 0)
        def _():
            carry[...] = p1_ref[...]
        uc = jnp.concatenate([carry[...], u], axis=0)
        shifted = [uc[V7X_SUBLANES - k:V7X_SUBLANES - k + tm, :] for k in (1, 2, 3)]
        carry[...] = u[tm - V7X_SUBLANES:, :]
        u_ref[...] = u[tm - (CONV_W - 1):, :]
    else:
        t = row % seq_rows
        prevs = (p1_ref[...], p2_ref[...], p3_ref[...])
        shifted = [jnp.where(t >= k, pltpu.roll(u, k, 0), prevs[k - 1]) for k in (1, 2, 3)]
        u_ref[...] = u
    cw = cw_ref[...]
    y = shifted[2] * cw[0:1, :]
    y = y + shifted[1] * cw[1:2, :]
    y = y + shifted[0] * cw[2:3, :]
    y = y + u * cw[3:4, :]
    qk = _silu(y)
    q_ref[...] = qk[:, 0:w].astype(BF16)
    kt_ref[...] = (qk[:, w:2 * w] * ATTN_SCALE).T.astype(BF16)
    v_ref[...] = _mm(lhs, w_ref[:, 2 * w:3 * w]).astype(BF16)
    og_ref[...] = jax.nn.sigmoid(_mm(lhs, w_ref[:, 3 * w:4 * w])).astype(BF16)
    gates = _mm(lhs, w_ref[:, 4 * w:4 * w + GATE_LANES]) + bias_ref[...]
    lane = lax.broadcasted_iota(jnp.int32, gates.shape, 1)
    is_f = (lane >= GATE_F0) & (lane < GATE_F0 + MLSTM_HEADS)
    gt = jnp.where(is_f, _log_sigmoid(gates), gates)
    gt_ref[...] = gt
    gtt_ref[...] = gt.T
    mq_ref[...] = (_mm(lhs, w_ref[:, 4 * w + GATE_LANES:]) * ATTN_SCALE).astype(mq_ref.dtype)


def _in_proj_b(x, x_off, m, w_bf, conv_w, bias, prevs, n_seq, seq_len, tile, per_tile_seqs):
    w = MLSTM_WIDTH
    tps = max(seq_len // tile, 1)
    ob = x_off // tile
    row = lambda width: pl.BlockSpec((tile, width), lambda b, i: (b * tps + i, 0))
    col = lambda height: pl.BlockSpec((height, tile), lambda b, i: (0, b * tps + i))
    const = lambda s: pl.BlockSpec(s, lambda b, i: (0, 0))
    if per_tile_seqs:
        pspecs = [row(2 * w)] * 3
        u_rows, u_spec = m, row(2 * w)
    else:
        pspecs = [pl.BlockSpec((V7X_SUBLANES, 2 * w), lambda b, i: (b, 0))] * 3
        u_rows = n_seq * (CONV_W - 1)
        u_spec = pl.BlockSpec((None, CONV_W - 1, 2 * w), lambda b, i: (b, 0, 0))
    u_shape = (jax.ShapeDtypeStruct((m, 2 * w), F32) if per_tile_seqs
               else jax.ShapeDtypeStruct((n_seq, CONV_W - 1, 2 * w), F32))
    return pl.pallas_call(
        functools.partial(_in_proj_b_body, seq_rows=seq_len if per_tile_seqs else None),
        grid=(m // (tps * tile), tps),
        in_specs=[pl.BlockSpec((tile, D_MODEL), lambda b, i: (b * tps + i + ob, 0)),
                  const(w_bf.shape), const(conv_w.shape), const(bias.shape)] + pspecs,
        out_specs=[row(w), col(w), row(w), row(w), row(GATE_LANES), col(GATE_LANES), row(MEM_WIDTH), u_spec],
        out_shape=[jax.ShapeDtypeStruct((m, w), BF16), jax.ShapeDtypeStruct((w, m), BF16),
                   jax.ShapeDtypeStruct((m, w), BF16), jax.ShapeDtypeStruct((m, w), BF16),
                   jax.ShapeDtypeStruct((m, GATE_LANES), F32), jax.ShapeDtypeStruct((GATE_LANES, m), F32),
                   jax.ShapeDtypeStruct((m, MEM_WIDTH), BF16), u_shape],
        scratch_shapes=[pltpu.VMEM((V7X_SUBLANES, 2 * w), F32)],
        compiler_params=_cparams(2, VMEM_LIMIT),
        name="in_proj_b",
    )(x, w_bf, conv_w, bias, *prevs)


def _split3_dot(x, sel, terms=3):
    x1 = x.astype(BF16)
    r1 = x - x1.astype(F32)
    x2 = r1.astype(BF16)
    if terms == 2:
        return _dot(x1, sel) + _dot(x2, sel)
    x3 = (r1 - x2.astype(F32)).astype(BF16)
    return _dot(x1, sel) + (_dot(x2, sel) + _dot(x3, sel))


def _scan_rows(x, op, fill):
    n = x.shape[0]
    row = lax.broadcasted_iota(jnp.int32, (n, 1), 0)
    sh = 1
    while sh < n:
        x = op(x, jnp.where(row >= sh, pltpu.roll(x, sh, 0), fill))
        sh *= 2
    return x


def _mlstm_body(q_ref, kt_ref, v_ref, og_ref, gt_ref, gtt_ref, ng_ref, ln_ref, st0_ref, m0_ref,
                cell_ref, st_out, m_out, st, m_rows, qk_s):
    c = pl.program_id(1)
    L = q_ref.shape[0]
    E = HEAD_DIM
    H = MLSTM_HEADS
    W = V7X_LANES

    @pl.when(c == 0)
    def _():
        st[...] = st0_ref[...]
        m_rows[...] = m0_ref[...]

    gt = gt_ref[...]
    b_cols = _scan_rows(gt, jnp.add, 0.0)
    r_cols = pltpu.roll(gt, GATE_F0, 1) - b_cols
    cm = _scan_rows(r_cols, jnp.maximum, NEG_INF)
    m_all = m_rows[...]
    sub = lax.broadcasted_iota(jnp.int32, m_all.shape, 0)
    m_lane = jnp.max(jnp.where(lax.broadcasted_iota(jnp.int32, m_all.shape, 1) == sub + GATE_F0, m_all, NEG_INF),
                     axis=0, keepdims=True)
    a_cols = jnp.maximum(m_lane, cm)
    wi_cols = jnp.exp(m_lane - a_cols)
    em_cols = jnp.exp(-(b_cols + a_cols))

    gtt = gtt_ref[0:2 * GATE_F0, :]
    lane_t = lax.broadcasted_iota(jnp.int32, (1, L), 1)
    b_rows = gtt
    sh = 1
    while sh < L:
        b_rows = b_rows + jnp.where(lane_t >= sh, pltpu.roll(b_rows, sh, 1), 0.0)
        sh *= 2
    r_rows = gtt[0:GATE_F0, :] - b_rows[GATE_F0:2 * GATE_F0, :]
    b_last = b_rows[GATE_F0:2 * GATE_F0, L - 1:L]
    m_prev = m_all[:, 0:1]
    m_new = jnp.maximum(b_last + m_prev, b_last + jnp.max(r_rows, axis=1, keepdims=True))
    decay = jnp.exp(b_last + m_prev - m_new)
    wk_scale = jnp.exp(b_last + r_rows - m_new)
    m_next = jnp.broadcast_to(m_new, m_all.shape)
    m_rows[...] = m_next
    m_out[...] = m_next

    causal = lax.broadcasted_iota(jnp.int32, (L, L), 0) >= lax.broadcasted_iota(jnp.int32, (L, L), 1)
    sub_k = lax.broadcasted_iota(jnp.int32, (W, 1), 0)
    lane_w = lax.broadcasted_iota(jnp.int32, (1, W), 1)
    ones_rhs = jnp.ones((L, W), BF16)

    for h in range(H):
        p, odd = h // 2, h % 2
        own_rows = (sub_k >= E) if odd else (sub_k < E)
        kt_pad = jnp.where(own_rows, kt_ref[p * W:(p + 1) * W, :], jnp.zeros((), BF16))
        s = _dot(q_ref[:, p * W:(p + 1) * W], kt_pad)
        a_bh = jnp.broadcast_to(a_cols[:, GATE_F0 + h:GATE_F0 + h + 1], (L, L))
        w_intra = jnp.exp(jnp.where(causal, r_rows[h:h + 1, :] - a_bh, NEG_INF))
        qk_s[h] = s * w_intra

    ln_sel = ln_ref[...]
    for p in range(H // 2):
        slab = slice(p * W, (p + 1) * W)
        q_pair = q_ref[:, slab]
        v_aug = jnp.concatenate([v_ref[:, slab], ones_rhs], axis=1)
        halves = []
        for odd in range(2):
            h = 2 * p + odd
            qk = qk_s[h].astype(BF16)
            wib = jnp.broadcast_to(wi_cols[:, GATE_F0 + h:GATE_F0 + h + 1], (L, W))
            emb = jnp.broadcast_to(em_cols[:, GATE_F0 + h:GATE_F0 + h + 1], (L, W))
            nd = _dot(qk, v_aug) + jnp.concatenate([wib, wib], axis=1) * _dot(q_pair, st[h].astype(BF16))
            halves.append(nd[:, 0:W] / jnp.maximum(jnp.abs(nd[:, W:2 * W]), emb))
        hv = jnp.where(lane_w < E, halves[0], halves[1])
        mu = _split3_dot(hv, ln_sel, 2) * (1.0 / E)
        hc = hv - mu
        var = _split3_dot(hc * hc, ln_sel, 2) * (1.0 / E)
        cell_ref[:, slab] = (og_ref[:, slab].astype(F32) * (hc * lax.rsqrt(var + LN_EPS) * ng_ref[:, slab])
                             ).astype(cell_ref.dtype)

    for h in range(H):
        p, odd = h // 2, h % 2
        rows = slice(odd * E, (odd + 1) * E)
        own_lanes = (lane_w >= E) if odd else (lane_w < E)
        wkt = (kt_ref[h * E:(h + 1) * E, :].astype(F32) * wk_scale[h:h + 1, :]).astype(BF16)
        d = decay[h:h + 1, :]
        v_aug = jnp.concatenate([v_ref[:, p * W:(p + 1) * W], ones_rhs], axis=1)
        own2 = jnp.concatenate([own_lanes, own_lanes], axis=1)
        st[h, rows, :] = d * st[h, rows, :] + jnp.where(own2, _dot(wkt, v_aug), 0.0)
    st_out[...] = st[...]


def _ln_selector():
    g = jnp.arange(V7X_LANES) // HEAD_DIM
    return (g[:, None] == g[None, :]).astype(BF16)


def _mlstm(q, kt, v, og, gt, gtt, norm_g, c0, n0, m0, n_seq, seq_len, chunk):
    w, hh, e, lanes = MLSTM_WIDTH, MLSTM_HEADS, HEAD_DIM, V7X_LANES
    nc = seq_len // chunk
    mrows = 2 * V7X_SUBLANES
    blk = lambda a: jnp.stack([jnp.pad(a[:, h], ((0, 0), ((h % 2) * e, (1 - h % 2) * e), ((h % 2) * e, (1 - h % 2) * e)))
                               for h in range(hh)], axis=1)
    unblk = lambda s: jnp.stack([s[:, h, (h % 2) * e:(h % 2 + 1) * e, (h % 2) * e:(h % 2 + 1) * e]
                                 for h in range(hh)], axis=1)
    st0 = jnp.concatenate([blk(c0), blk(jnp.broadcast_to(n0[..., None], (n_seq, hh, e, e)))], axis=-1)
    m0b = jnp.pad(jnp.broadcast_to(m0[..., None], (n_seq, hh, lanes)), ((0, 0), (0, mrows - hh), (0, 0)))
    ln = _ln_selector()
    row = lambda width: pl.BlockSpec((chunk, width), lambda b, c: (b * nc + c, 0))
    col = lambda height: pl.BlockSpec((height, chunk), lambda b, c: (0, b * nc + c))
    const = lambda a: pl.BlockSpec(a.shape, lambda b, c: (0,) * a.ndim)
    st = lambda shp: pl.BlockSpec((None,) + shp, lambda b, c: (b,) + (0,) * len(shp))
    slab = (hh, lanes, 2 * lanes)
    cell, st_out, m_out = pl.pallas_call(
        _mlstm_body,
        grid=(n_seq, nc),
        in_specs=[row(w), col(w), row(w), row(w), row(GATE_LANES), col(GATE_LANES),
                  pl.BlockSpec((1, w), lambda b, c: (0, 0)), const(ln),
                  st(slab), st((mrows, lanes))],
        out_specs=[row(w), st(slab), st((mrows, lanes))],
        out_shape=[jax.ShapeDtypeStruct((n_seq * seq_len, w), BF16),
                   jax.ShapeDtypeStruct((n_seq,) + slab, F32),
                   jax.ShapeDtypeStruct((n_seq, mrows, lanes), F32)],
        scratch_shapes=[pltpu.VMEM(slab, F32), pltpu.VMEM((mrows, lanes), F32),
                        pltpu.VMEM((hh, chunk, chunk), F32)],
        compiler_params=_cparams(2, VMEM_LIMIT),
        name="mlstm_chunks",
    )(q, kt, v, og, gt, gtt, norm_g.reshape(1, w), ln, st0, m0b)
    return cell, unblk(st_out[..., 0:lanes]), unblk(st_out[..., lanes:])[..., 0], m_out[:, 0:hh, 0]


def _w_in_b_regroup(w_in, b_if):
    w4 = 4 * MLSTM_WIDTH
    hh = MLSTM_HEADS
    z = lambda n: jnp.zeros((D_MODEL, n), w_in.dtype)
    w = jnp.concatenate([w_in[:, :w4 + hh], z(GATE_F0 - hh), w_in[:, w4 + hh:w4 + 2 * hh],
                         z(GATE_LANES - GATE_F0 - hh), w_in[:, w4 + 2 * hh:]], axis=1)
    zb = lambda n: jnp.zeros((n,), F32)
    bias = jnp.concatenate([b_if[0].astype(F32), zb(GATE_F0 - hh), b_if[1].astype(F32),
                            zb(GATE_LANES - GATE_F0 - hh)]).reshape(1, GATE_LANES)
    return w, bias


def _layer_b_prompt(h_all, w_bf, bias, conv_w, norm_g, w_out, mem_prompt, w_mem, ln_g, ln_b, n_seq, seq_len,
                    tail):
    m = n_seq * seq_len
    tile = min(ROW_TILE, seq_len)
    chunk = min(MLSTM_CHUNK, seq_len)
    zstate = jnp.zeros((n_seq * V7X_SUBLANES, 2 * MLSTM_WIDTH), F32)
    q, kt, v, og, gt, gtt, mq, conv = _in_proj_b(h_all, 0, m, w_bf, conv_w, bias, [zstate] * 3, n_seq, seq_len,
                                                 tile, False)
    hh, e = MLSTM_HEADS, HEAD_DIM
    cell, c_out, n_out, m_out = _mlstm(q, kt, v, og, gt, gtt, norm_g, jnp.zeros((n_seq, hh, e, e), F32),
                                       jnp.zeros((n_seq, hh, e), F32), jnp.zeros((n_seq, hh), F32),
                                       n_seq, seq_len, chunk)
    mem_kv = _matmul(mem_prompt.reshape(n_seq * N_MEM, D_MODEL), w_mem, N_MEM)
    h1 = _fused_out([cell], [], mq, mem_kv.reshape(n_seq, N_MEM, 2 * MEM_WIDTH), h_all, w_out.astype(BF16),
                    ln_g, ln_b, tile, seq_len, tail)
    return h1, conv, c_out, n_out.reshape(n_seq, hh, e), m_out.reshape(n_seq, hh), mem_kv


def _layer_b_sample(h_all, h_off, w_bf, bias, conv_w, norm_g, w_out, mem_kv, conv_state, c0, n0, m0, ln_g, ln_b,
                    n_b, n_new):
    m = n_b * n_new
    w2 = 2 * MLSTM_WIDTH
    hh, e = MLSTM_HEADS, HEAD_DIM
    chunk = MLSTM_CHUNK
    t = jnp.arange(n_new)
    prevs = []
    for kk in (1, 2, 3):
        idx = jnp.clip(CONV_W - 1 - kk + t, 0, CONV_W - 2)
        prevs.append(conv_state[:, idx, :].reshape(m, w2))
    q, kt, v, og, gt, gtt, mq, u = _in_proj_b(h_all, h_off, m, w_bf, conv_w, bias, prevs, n_b, n_new, m, True)
    conv_new = jnp.concatenate([conv_state, u.reshape(n_b, n_new, w2)], axis=1)[:, n_new:]
    npad = chunk - n_new
    pad3 = lambda a: jnp.pad(a.reshape(n_b, n_new, a.shape[1]), ((0, 0), (0, npad), (0, 0)))
    gate_pad = jnp.where(jnp.arange(GATE_LANES) < hh, NEG_INF, 0.0).astype(F32)
    gtp = jnp.concatenate([gt.reshape(n_b, n_new, GATE_LANES),
                           jnp.broadcast_to(gate_pad, (n_b, npad, GATE_LANES))], axis=1)
    gttp = jnp.concatenate([gtt.reshape(GATE_LANES, n_b, n_new),
                            jnp.broadcast_to(gate_pad[:, None, None], (GATE_LANES, n_b, npad))], axis=2)
    ktp = jnp.pad(kt.reshape(MLSTM_WIDTH, n_b, n_new), ((0, 0), (0, 0), (0, npad)))
    flat = lambda a: a.reshape(n_b * chunk, a.shape[2])
    flat_t = lambda a: a.reshape(a.shape[0], n_b * chunk)
    cell, c_out, n_out, m_out = _mlstm(flat(pad3(q)), flat_t(ktp), flat(pad3(v)), flat(pad3(og)), flat(gtp),
                                       flat_t(gttp), norm_g, c0, n0, m0, n_b, chunk, chunk)
    cell = cell.reshape(n_b, chunk, MLSTM_WIDTH)[:, :n_new].reshape(m, MLSTM_WIDTH)
    mo = _sample_mem_attn(mq.astype(F32).reshape(n_b, n_new, MEM_WIDTH),
                          mem_kv.reshape(n_b, N_MEM, 2 * MEM_WIDTH), n_b, n_new)
    h1 = _out_proj(cell, mo.reshape(m, MEM_WIDTH), h_all, h_off, w_out, ln_g, ln_b, m)
    return h1, conv_new, c_out, n_out.reshape(n_b, hh, e), m_out.reshape(n_b, hh)


def _layer_a_sample(h2d, w_in, w_out, mem_kv, caches, ln_g, ln_b, n_b, n_new):
    m = n_b * n_new
    pos = PAST_LEN + (jnp.arange(m, dtype=jnp.int32) % n_new)
    outs = _in_proj_a(h2d, w_in, _rope_tables(pos), n_new, n_b, m, full_kv=True)
    qs, ks, vs, mq, kvs = outs[0:3], outs[3:6], outs[6:9], outs[9], outs[10:13]
    f3 = lambda t: t.reshape(n_b, n_new, t.shape[1])
    caches = [c.reshape(n_b, c.shape[1], 2 * DIL_WIDTH) for c in caches]
    mix = _sample_attn([f3(t) for t in qs], [f3(t) for t in ks], [f3(t) for t in vs], caches, n_b, n_new)
    mo = _sample_mem_attn(f3(mq), mem_kv.reshape(n_b, N_MEM, 2 * MEM_WIDTH), n_b, n_new)
    h1 = _out_proj(mix.reshape(m, DIL_WIDTH), mo.reshape(m, MEM_WIDTH), h2d, 0, w_out, ln_g, ln_b, m)
    return h1, kvs


def _layer_a_prompt(h2d, mem_prompt, w_in, w_out, w_mem, ln_g, ln_b, n_seq, seq_len, tail):
    tile = min(ROW_TILE, seq_len)
    tabs = _rope_tables(jnp.arange(seq_len, dtype=jnp.int32))
    outs = _in_proj_a(h2d, w_in.astype(BF16), tabs, seq_len, n_seq, tile, full_kv=False)
    qs, ks, vs, mq, kvs = outs[0:3], outs[3:6], outs[6:9], outs[9], outs[10:13]
    mem_kv = _matmul(mem_prompt.reshape(n_seq * N_MEM, D_MODEL), w_mem, N_MEM)
    os_, lses = [], []
    for g, (win, dil) in enumerate(DIL_PAIRS):
        o, lse = _dil_attn(qs[g], ks[g], vs[g], n_seq, seq_len, dil, win // dil)
        os_.append(o)
        lses.append(lse)
    h1 = _fused_out(os_, lses, mq, mem_kv.reshape(n_seq, N_MEM, 2 * MEM_WIDTH), h2d, w_out.astype(BF16),
                    ln_g, ln_b, tile, seq_len, tail)
    return h1, kvs, mem_kv


def kernel(x_prompt, x_sample, mem_prompt, cache_win1_kv, cache_win2_kv, cache_win3_kv, cache_mem_kv,
           state_mlstm_conv, state_mlstm_C, state_mlstm_n, state_mlstm_m,
           w_in_A, w_out_A, w_in_B, mlstm_conv_w, mlstm_b_if, mlstm_norm_g, w_out_B,
           w_mem_kv, ln_g, ln_b, w_grp, b_grp, w_exp, b_exp, w_gate, w_up, w_down):
    n_p, seq, d = x_prompt.shape
    n_s, t_new, _ = x_sample.shape
    assert d == D_MODEL and w_in_A.shape[0] == 1 and w_in_B.shape[0] == 1
    mp, ms = n_p * seq, n_s * t_new
    m_all = mp + ms
    assert mp % ROW_TILE == 0 and ms % SMALL_TILE == 0 and mp % ms == 0
    xp = x_prompt.reshape(mp, d)
    xs = x_sample.reshape(ms, d)

    wg_bf, wu_bf, wd_bf = w_gate.astype(BF16), w_up.astype(BF16), w_down.astype(BF16)

    def moe(h, i, parts):
        return _moe_layer(h, mp, w_grp[i], b_grp[i], w_exp[i], b_exp[i], wg_bf, wu_bf, wd_bf, i,
                          ln_g[i, 1], ln_b[i, 1], parts)

    caches = (cache_win1_kv[0], cache_win2_kv[0], cache_win3_kv[0])
    hs, kv_s = _layer_a_sample(xs, w_in_A[0], w_out_A[0], cache_mem_kv[0], caches, ln_g[0, 0], ln_b[0, 0],
                               n_s, t_new)
    h, kv_p, mem_kv0 = _layer_a_prompt(xp, mem_prompt, w_in_A[0], w_out_A[0], w_mem_kv[0], ln_g[0, 0], ln_b[0, 0],
                                       n_p, seq, hs)
    (h,) = moe(h, 0, [(0, m_all)])

    w_b, bias = _w_in_b_regroup(w_in_B[0], mlstm_b_if[0])
    w_bf = w_b.astype(BF16)
    hs, conv_s, c_s, nv_s, mm_s = _layer_b_sample(
        h, mp, w_b, bias, mlstm_conv_w[0], mlstm_norm_g[0], w_out_B[0], cache_mem_kv[1],
        state_mlstm_conv[0], state_mlstm_C[0], state_mlstm_n[0], state_mlstm_m[0],
        ln_g[1, 0], ln_b[1, 0], n_s, t_new)
    h1, conv_p, c_p, nv_p, mm_p, mem_kv1 = _layer_b_prompt(
        h, w_bf, bias, mlstm_conv_w[0], mlstm_norm_g[0], w_out_B[0], mem_prompt, w_mem_kv[1],
        ln_g[1, 0], ln_b[1, 0], n_p, seq, hs)
    yp, ys = moe(h1, 1, [(0, mp), (mp, ms)])

    kv_shape = (2, HEADS_PER_GROUP, HEAD_DIM)
    wins = []
    for g in range(N_DIL_GROUPS):
        wins.append(kv_p[g].reshape((1, n_p, -1) + kv_shape))
        wins.append(kv_s[g].reshape((1, n_s, t_new) + kv_shape))
    mem_kv_p = jnp.stack([mem_kv0, mem_kv1]).reshape((2, n_p, N_MEM, 2, MEM_HEADS, HEAD_DIM))
    return (yp.reshape(n_p, seq, d), ys.reshape(n_s, t_new, d), *wins,
            conv_p[None], conv_s[None], c_p[None], c_s[None], nv_p[None], nv_s[None], mm_p[None], mm_s[None],
            mem_kv_p)
```

```python
import functools

import jax
import jax.numpy as jnp
import numpy as np
from jax import lax
from jax.experimental import pallas as pl
from jax.experimental.pallas import tpu as pltpu

D_MODEL = 1024
HEAD_DIM = 64
ATTN_SCALE = HEAD_DIM ** -0.5
PAST_LEN = 16384
N_MEM = 256
MEM_HEADS = 4
MEM_WIDTH = MEM_HEADS * HEAD_DIM
DIL_PAIRS = ((128, 1), (512, 4), (2048, 16))
N_DIL_GROUPS = len(DIL_PAIRS)
HEADS_PER_GROUP = 4
DIL_WIDTH = HEADS_PER_GROUP * HEAD_DIM
QBLK = 128
ROPE_THETA = 500000.0
ROPE_DIMS = HEAD_DIM // 4
MLSTM_HEADS = 12
MLSTM_WIDTH = MLSTM_HEADS * HEAD_DIM
CONV_W = 4
N_GROUPS = 4
EXPERTS_PER_GROUP = 8
N_EXPERTS = N_GROUPS * EXPERTS_PER_GROUP
D_EXPERT = 256
DEPTH = 2
ALPHA = (2 * DEPTH) ** 0.25
LN_EPS = 1e-5

V7X_LANES = 128
V7X_SUBLANES = 8
V7X_VMEM_BYTES = 64 * 1024 * 1024
VMEM_LIMIT = 48 * 1024 * 1024

ROW_TILE = 512
SMALL_TILE = 128
MLSTM_CHUNK = 128
EXPERT_ROWS = 256

BF16 = jnp.bfloat16
F32 = jnp.float32
NEG_INF = float("-inf")


def _cparams(n_axes, vmem=None, flags=None):
    return pltpu.CompilerParams(dimension_semantics=("arbitrary",) * n_axes,
                                vmem_limit_bytes=vmem, flags=flags)


def _nt_dot(a, b):
    return lax.dot_general(a, b, (((1,), (1,)), ((), ())), preferred_element_type=F32)


def _dot(a, b):
    return jnp.dot(a, b, preferred_element_type=F32)


def _split_bf16(x):
    hi = x.astype(BF16)
    lo = (x - hi.astype(F32)).astype(BF16)
    return hi, lo


def _lhs(x, precise):
    return _split_bf16(x) if precise else (x.astype(BF16),)


def _mm(lhs, w):
    if len(lhs) == 1:
        return _dot(lhs[0], w)
    wh, wl = _split_bf16(w)
    return _dot(lhs[0], wh) + (_dot(lhs[0], wl) + _dot(lhs[1], wh))


def _rope_tables(pos):
    half = ROPE_DIMS // 2
    inv = jnp.power(ROPE_THETA, -jnp.arange(half, dtype=F32) * (2.0 / ROPE_DIMS))
    ang = pos.astype(F32)[:, None] * inv[None, :]
    cos, sin = jnp.cos(ang), jnp.sin(ang)
    n = pos.shape[0]
    one = jnp.ones((n, HEAD_DIM - ROPE_DIMS), F32)
    zero8 = jnp.zeros((n, half), F32)
    zrest = jnp.zeros((n, HEAD_DIM - ROPE_DIMS), F32)
    a = jnp.concatenate([cos, cos, one], axis=1)
    b = jnp.concatenate([zero8, sin, zrest], axis=1)
    c = jnp.concatenate([-sin, zero8, zrest], axis=1)
    rep = V7X_LANES // HEAD_DIM
    return jnp.tile(a, (1, rep)), jnp.tile(b, (1, rep)), jnp.tile(c, (1, rep))


def _rope_apply(x, ra, rb, rc):
    parts = []
    for s in range(x.shape[1] // V7X_LANES):
        v = x[:, s * V7X_LANES:(s + 1) * V7X_LANES]
        parts.append(v * ra + pltpu.roll(v, ROPE_DIMS // 2, 1) * rb
                     + pltpu.roll(v, V7X_LANES - ROPE_DIMS // 2, 1) * rc)
    return jnp.concatenate(parts, axis=1)


def _in_proj_a_body(x_ref, w_ref, ra_ref, rb_ref, rc_ref, *outs, kv_rows, precise):
    q_refs, k_refs, v_refs = outs[0:3], outs[3:6], outs[6:9]
    mq_ref = outs[9]
    kv_refs = outs[10:13]
    lhs = _lhs(x_ref[...], precise)
    ra, rb, rc = ra_ref[...], rb_ref[...], rc_ref[...]
    tm = x_ref.shape[0]
    gw = DIL_WIDTH
    for g in range(N_DIL_GROUPS):
        q = _rope_apply(_mm(lhs, w_ref[:, g * gw:(g + 1) * gw]), ra, rb, rc)
        k = _rope_apply(_mm(lhs, w_ref[:, (3 + g) * gw:(4 + g) * gw]), ra, rb, rc)
        v = _mm(lhs, w_ref[:, (6 + g) * gw:(7 + g) * gw])
        q_refs[g][...] = (q * ATTN_SCALE).astype(q_refs[g].dtype)
        k_refs[g][...] = k.astype(k_refs[g].dtype)
        v_refs[g][...] = v.astype(v_refs[g].dtype)
        r = kv_rows[g]
        kv_refs[g][:, 0:gw] = k[tm - r:, :]
        kv_refs[g][:, gw:2 * gw] = v[tm - r:, :]
    mq = _mm(lhs, w_ref[:, 9 * gw:9 * gw + MEM_WIDTH])
    mq_ref[...] = (mq * ATTN_SCALE).astype(mq_ref.dtype)


def _in_proj_a(x2d, w, tabs, seq_len, n_seq, tile, full_kv):
    w_bf = w
    act = F32 if full_kv else BF16
    m = x2d.shape[0]
    nt = m // tile
    gw = DIL_WIDTH
    row_spec = lambda w: pl.BlockSpec((tile, w), lambda i: (i, 0))
    if full_kv:
        tab_spec = pl.BlockSpec((tile, V7X_LANES), lambda i: (i, 0))
        kv_rows = (tile,) * 3
        kv_shapes = [jax.ShapeDtypeStruct((m, 2 * gw), F32)] * 3
        kv_specs = [row_spec(2 * gw)] * 3
    else:
        tps = seq_len // tile
        tab_spec = pl.BlockSpec((tile, V7X_LANES), lambda i: (i % tps, 0))
        kv_rows, kv_shapes, kv_specs = [], [], []
        for win, _ in DIL_PAIRS:
            wb = min(win, seq_len)
            r = min(wb, tile)
            nblk = wb // r
            kv_rows.append(r)
            kv_shapes.append(jax.ShapeDtypeStruct((n_seq * wb, 2 * gw), F32))
            kv_specs.append(pl.BlockSpec(
                (r, 2 * gw),
                lambda i, nblk=nblk: ((i // tps) * nblk + jnp.maximum(i % tps - (tps - nblk), 0), 0)))
        kv_rows = tuple(kv_rows)
    out_shapes = [jax.ShapeDtypeStruct((m, gw), act)] * 9 + [jax.ShapeDtypeStruct((m, MEM_WIDTH), act)] + kv_shapes
    out_specs = [row_spec(gw)] * 9 + [row_spec(MEM_WIDTH)] + kv_specs
    return pl.pallas_call(
        functools.partial(_in_proj_a_body, kv_rows=kv_rows, precise=full_kv),
        grid=(nt,),
        in_specs=[row_spec(D_MODEL), pl.BlockSpec(w_bf.shape, lambda i: (0, 0)), tab_spec, tab_spec, tab_spec],
        out_specs=out_specs,
        out_shape=out_shapes,
        compiler_params=_cparams(1, VMEM_LIMIT),
        name="in_proj_a",
    )(x2d, w_bf, *tabs)


def _dil_attn_body(q_ref, k_ref, v_ref, kp_ref, vp_ref, o_ref, lse_ref, *, span):
    i = pl.program_id(2)
    q, k, v = q_ref[...], k_ref[...], v_ref[...]
    kp, vp = kp_ref[...], vp_ref[...]
    qi = lax.broadcasted_iota(jnp.int32, (QBLK, 2 * QBLK), 0) + QBLK
    ki = lax.broadcasted_iota(jnp.int32, (QBLK, 2 * QBLK), 1)
    band = (qi >= ki) & (qi - ki <= span) & ((i > 0) | (ki >= QBLK))
    lses = []
    for h in range(HEADS_PER_GROUP):
        hs = slice(h * HEAD_DIM, (h + 1) * HEAD_DIM)
        kc = jnp.concatenate([kp[:, hs], k[:, hs]], axis=0)
        vc = jnp.concatenate([vp[:, hs], v[:, hs]], axis=0)
        s = jnp.where(band, _nt_dot(q[:, hs], kc), NEG_INF)
        m = jnp.max(s, axis=1, keepdims=True)
        p = jnp.exp(s - m)
        den = jnp.sum(p, axis=1, keepdims=True)
        o_ref[:, hs] = _dot((p / den).astype(BF16), vc).astype(o_ref.dtype)
        lses.append(m + jnp.log(den))
    lse_ref[...] = jnp.concatenate(lses, axis=1)


def _dil_attn(q, k, v, n_seq, seq_len, dil, span):
    gw = DIL_WIDTH
    L = seq_len // dil
    nb = L // QBLK
    view = lambda t: t.reshape(n_seq, L, dil * gw)
    blk = pl.BlockSpec((None, QBLK, gw), lambda b, r, i: (b, i, r))
    prev = pl.BlockSpec((None, QBLK, gw), lambda b, r, i: (b, jnp.maximum(i - 1, 0), r))
    o, lse = pl.pallas_call(
        functools.partial(_dil_attn_body, span=span),
        grid=(n_seq, dil, nb),
        in_specs=[blk, blk, blk, prev, prev],
        out_specs=[blk, pl.BlockSpec((None, None, QBLK, HEADS_PER_GROUP), lambda b, r, i: (b, r, i, 0))],
        out_shape=[jax.ShapeDtypeStruct((n_seq, L, dil * gw), BF16),
                   jax.ShapeDtypeStruct((n_seq, dil, L, HEADS_PER_GROUP), F32)],
        compiler_params=_cparams(3),
        name=f"dil_attn_d{dil}",
    )(view(q), view(k), view(v), view(k), view(v))
    o = o.reshape(n_seq * seq_len, gw)
    lse = jnp.transpose(lse, (0, 2, 1, 3)).reshape(n_seq * seq_len, HEADS_PER_GROUP)
    return o, lse


def _layer_norm_rows(x, g, b):
    mu = jnp.mean(x, axis=1, keepdims=True)
    xc = x - mu
    var = jnp.mean(xc * xc, axis=1, keepdims=True)
    return xc * lax.rsqrt(var + LN_EPS) * g + b


def _out_proj_body(a_ref, mo_ref, h_ref, w_ref, g_ref, b_ref, o_ref, *, precise):
    ka = a_ref.shape[1]
    y = _mm(_lhs(a_ref[...], precise), w_ref[0:ka, :]) + _mm(_lhs(mo_ref[...], precise), w_ref[ka:, :])
    o_ref[...] = _layer_norm_rows(ALPHA * h_ref[...] + y, g_ref[...], b_ref[...])


def _out_proj(a, mo, h, h_off, w, g, b, tile):
    m, ka = a.shape
    row = lambda width, off=0: pl.BlockSpec((tile, width), lambda i: (i + off // tile, 0))
    const = lambda s: pl.BlockSpec(s, lambda i: (0, 0))
    return pl.pallas_call(
        functools.partial(_out_proj_body, precise=(w.dtype == F32)),
        grid=(m // tile,),
        in_specs=[row(ka), row(MEM_WIDTH), row(D_MODEL, h_off), const(w.shape), const((1, D_MODEL)),
                  const((1, D_MODEL))],
        out_specs=row(D_MODEL),
        out_shape=jax.ShapeDtypeStruct((m, D_MODEL), F32),
        compiler_params=_cparams(1, VMEM_LIMIT),
        name="out_proj_ln",
    )(a, mo, h, w, g.reshape(1, D_MODEL), b.reshape(1, D_MODEL))


def _fused_out_body(*refs, n_parts, n_tiles, has_tail):
    parts = refs[0:n_parts]
    k = n_parts
    lses = refs[k:k + n_parts] if n_parts > 1 else ()
    k += len(lses)
    mq_ref, kv_ref, h_ref, w_ref, g_ref, b_ref = refs[k:k + 6]
    tail_ref = refs[k + 6] if has_tail else None
    o_ref = refs[-1]

    def rows():
        if n_parts == 1:
            a = parts[0][...]
        else:
            ls = [l[...] for l in lses]
            heads = []
            for h in range(HEADS_PER_GROUP):
                hs = slice(h * HEAD_DIM, (h + 1) * HEAD_DIM)
                lh = [l[:, h:h + 1] for l in ls]
                mx = jnp.maximum(jnp.maximum(lh[0], lh[1]), lh[2])
                e = [jnp.exp(x - mx) for x in lh]
                tot = e[0] + e[1] + e[2]
                acc = (e[0] / tot) * parts[0][:, hs].astype(F32)
                acc = acc + (e[1] / tot) * parts[1][:, hs].astype(F32)
                acc = acc + (e[2] / tot) * parts[2][:, hs].astype(F32)
                heads.append(acc)
            a = jnp.concatenate(heads, axis=1).astype(BF16)
        ka = a.shape[1]
        q = mq_ref[...]
        kv = kv_ref[...].astype(BF16)
        mos = []
        for h in range(MEM_HEADS):
            hs = slice(h * HEAD_DIM, (h + 1) * HEAD_DIM)
            vs = slice(MEM_WIDTH + h * HEAD_DIM, MEM_WIDTH + (h + 1) * HEAD_DIM)
            s = _nt_dot(q[:, hs], kv[:, hs])
            m = jnp.max(s, axis=1, keepdims=True)
            p = jnp.exp(s - m)
            den = jnp.sum(p, axis=1, keepdims=True)
            mos.append(_dot((p / den).astype(BF16), kv[:, vs]))
        mo = jnp.concatenate(mos, axis=1).astype(BF16)
        y = _dot(a, w_ref[0:ka, :]) + _dot(mo, w_ref[ka:, :])
        o_ref[...] = _layer_norm_rows(ALPHA * h_ref[...] + y, g_ref[...], b_ref[...])

    if not has_tail:
        rows()
    else:
        pl.when(pl.program_id(0) < n_tiles)(rows)

        @pl.when(pl.program_id(0) == n_tiles)
        def _():
            o_ref[0:tail_ref.shape[0], :] = tail_ref[...]


def _fused_out(parts, lses, mq, mem_kv, h, w_bf, g, b, tile, seq_len, tail):
    m = parts[0].shape[0]
    nt = m // tile
    last = nt - 1
    tps = seq_len // tile
    clamp = lambda i: jnp.minimum(i, last)
    row = lambda w: pl.BlockSpec((tile, w), lambda i: (clamp(i), 0))
    const = lambda s: pl.BlockSpec(s, lambda i: (0,) * len(s))
    in_specs = [row(p.shape[1]) for p in parts] + [row(l.shape[1]) for l in lses]
    in_specs += [row(MEM_WIDTH), pl.BlockSpec((None, N_MEM, 2 * MEM_WIDTH), lambda i: (clamp(i) // tps, 0, 0)),
                 row(D_MODEL), const(w_bf.shape), const((1, D_MODEL)), const((1, D_MODEL))]
    args = list(parts) + list(lses) + [mq, mem_kv, h, w_bf, g.reshape(1, D_MODEL), b.reshape(1, D_MODEL)]
    out_rows, steps = m, nt
    if tail is not None:
        assert tail.shape[0] <= tile
        in_specs.append(const(tail.shape))
        args.append(tail)
        out_rows, steps = m + tail.shape[0], nt + 1
    return pl.pallas_call(
        functools.partial(_fused_out_body, n_parts=len(parts), n_tiles=nt, has_tail=tail is not None),
        grid=(steps,),
        in_specs=in_specs,
        out_specs=pl.BlockSpec((tile, D_MODEL), lambda i: (i, 0)),
        out_shape=jax.ShapeDtypeStruct((out_rows, D_MODEL), F32),
        compiler_params=_cparams(1, VMEM_LIMIT),
        name="attn_out_proj_ln",
    )(*args)


def _matmul_body(x_ref, w_ref, o_ref):
    o_ref[...] = _dot(x_ref[...].astype(BF16), w_ref[...].astype(BF16))


def _matmul(x, w, tile):
    m, k = x.shape
    n = w.shape[1]
    return pl.pallas_call(
        _matmul_body,
        grid=(m // tile,),
        in_specs=[pl.BlockSpec((tile, k), lambda i: (i, 0)), pl.BlockSpec((k, n), lambda i: (0, 0))],
        out_specs=pl.BlockSpec((tile, n), lambda i: (i, 0)),
        out_shape=jax.ShapeDtypeStruct((m, n), F32),
        compiler_params=_cparams(1),
        name="mem_kv_proj",
    )(x, w)


def _col_attend(q_row, kmat, vmat, kmask, knew, vnew, nmask):
    pk = kmat * q_row
    pn = None if knew is None else knew * q_row
    outs, lses = [], []
    for h in range(HEADS_PER_GROUP):
        hs = slice(h * HEAD_DIM, (h + 1) * HEAD_DIM)
        s = jnp.sum(pk[:, hs], axis=1, keepdims=True)
        if kmask is not None:
            s = jnp.where(kmask, s, NEG_INF)
        m = jnp.max(s, axis=0, keepdims=True)
        if pn is not None:
            sn = jnp.where(nmask, jnp.sum(pn[:, hs], axis=1, keepdims=True), NEG_INF)
            m = jnp.maximum(m, jnp.max(sn, axis=0, keepdims=True))
        p = jnp.exp(s - m)
        den = jnp.sum(p, axis=0, keepdims=True)
        acc = jnp.sum(p * vmat[:, hs], axis=0, keepdims=True)
        if pn is not None:
            pnw = jnp.exp(sn - m)
            den = den + jnp.sum(pnw, axis=0, keepdims=True)
            acc = acc + jnp.sum(pnw * vnew[:, hs], axis=0, keepdims=True)
        outs.append(acc / den)
        lses.append(m + jnp.log(den))
    return outs, lses


def _sample_attn_body(q1, q2, q3, kn1, kn2, kn3, vn1, vn2, vn3, c1, c2, c3, mix_ref, *, n_new):
    qs = (q1[...], q2[...], q3[...])
    kns = (kn1[...], kn2[...], kn3[...])
    vns = (vn1[...], vn2[...], vn3[...])
    gw = DIL_WIDTH
    t_idx = lax.broadcasted_iota(jnp.int32, (n_new, 1), 0)
    rows = []
    for t in range(n_new):
        per_group = []
        for g, (win, dil) in enumerate(DIL_PAIRS):
            q_row = qs[g][t:t + 1, :]
            cache = (c1, c2, c3)[g]
            if dil == 1:
                kmat, vmat = cache[:, 0:gw], cache[:, gw:2 * gw]
                r_idx = lax.broadcasted_iota(jnp.int32, (kmat.shape[0], 1), 0)
                per_group.append(_col_attend(q_row, kmat, vmat, r_idx >= t, kns[g], vns[g], t_idx <= t))
            else:
                base = t * 2 * gw
                kmat, vmat = cache[:, base:base + gw], cache[:, base + gw:base + 2 * gw]
                per_group.append(_col_attend(q_row, kmat, vmat, None, kns[g], vns[g], t_idx == t))
        heads = []
        for h in range(HEADS_PER_GROUP):
            lh = [per_group[g][1][h] for g in range(N_DIL_GROUPS)]
            mx = jnp.maximum(jnp.maximum(lh[0], lh[1]), lh[2])
            e = [jnp.exp(x - mx) for x in lh]
            tot = e[0] + e[1] + e[2]
            acc = (e[0] / tot) * per_group[0][0][h]
            acc = acc + (e[1] / tot) * per_group[1][0][h]
            acc = acc + (e[2] / tot) * per_group[2][0][h]
            heads.append(acc)
        rows.append(jnp.concatenate(heads, axis=1))
    mix_ref[...] = jnp.concatenate(rows, axis=0)


def _sample_attn(qs, kns, vns, caches, n_b, n_new):
    gw = DIL_WIDTH
    small = pl.BlockSpec((None, n_new, gw), lambda b: (b, 0, 0))
    cviews, cspecs = [], []
    for c, (win, dil) in zip(caches, DIL_PAIRS):
        blocks = win // dil
        res = min(dil, n_new)
        picked = c.reshape((n_b, blocks, dil) + c.shape[2:])[:, :, 0:res]
        cviews.append(picked.reshape(n_b, blocks, res * 2 * gw))
        cspecs.append(pl.BlockSpec((None, blocks, res * 2 * gw), lambda b: (b, 0, 0)))
    return pl.pallas_call(
        functools.partial(_sample_attn_body, n_new=n_new),
        grid=(n_b,),
        in_specs=[small] * 9 + cspecs,
        out_specs=small,
        out_shape=jax.ShapeDtypeStruct((n_b, n_new, gw), F32),
        compiler_params=_cparams(1, VMEM_LIMIT),
        name="sample_dil_attn",
    )(*qs, *kns, *vns, *cviews)


def _sample_mem_attn_body(q_ref, kv_ref, o_ref, *, n_new):
    q = q_ref[...]
    kmat, vmat = kv_ref[:, 0:MEM_WIDTH], kv_ref[:, MEM_WIDTH:2 * MEM_WIDTH]
    rows = []
    for t in range(n_new):
        outs, _ = _col_attend(q[t:t + 1, :], kmat, vmat, None, None, None, None)
        rows.append(jnp.concatenate(outs, axis=1))
    o_ref[...] = jnp.concatenate(rows, axis=0)


def _sample_mem_attn(q, mem_kv, n_b, n_new):
    small = pl.BlockSpec((None, n_new, MEM_WIDTH), lambda b: (b, 0, 0))
    return pl.pallas_call(
        functools.partial(_sample_mem_attn_body, n_new=n_new),
        grid=(n_b,),
        in_specs=[small, pl.BlockSpec((None, N_MEM, 2 * MEM_WIDTH), lambda b: (b, 0, 0))],
        out_specs=small,
        out_shape=jax.ShapeDtypeStruct((n_b, n_new, MEM_WIDTH), F32),
        compiler_params=_cparams(1),
        name="sample_mem_attn",
    )(q, mem_kv)


ROUTE_LANES = V7X_LANES


def _router_body(x_ref, w_ref, b_ref, cnt0_ref, o_ref, cnt_ref):
    xh, xl = _split_bf16(x_ref[...])
    wh, wl = _split_bf16(w_ref[...])
    logits = _dot(xh, wh) + (_dot(xh, wl) + _dot(xl, wh)) + b_ref[...]
    lane = lax.broadcasted_iota(jnp.int32, logits.shape, 1)
    big = jnp.int32(ROUTE_LANES)
    is_grp = (lane >= N_EXPERTS) & (lane < N_EXPERTS + N_GROUPS)
    gl = jnp.where(is_grp, logits, NEG_INF)
    gmax = jnp.max(gl, axis=1, keepdims=True)
    gsel = jnp.min(jnp.where(gl == gmax, lane, big), axis=1, keepdims=True) - N_EXPERTS
    gp = 1.0 / jnp.sum(jnp.exp(gl - gmax), axis=1, keepdims=True)
    in_grp = (lane < N_EXPERTS) & ((lane // EXPERTS_PER_GROUP) == gsel)
    el = jnp.where(in_grp, logits, NEG_INF)
    v1 = jnp.max(el, axis=1, keepdims=True)
    i1 = jnp.min(jnp.where(el == v1, lane, big), axis=1, keepdims=True)
    el2 = jnp.where(lane == i1, NEG_INF, el)
    v2 = jnp.max(el2, axis=1, keepdims=True)
    i2 = jnp.min(jnp.where(el2 == v2, lane, big), axis=1, keepdims=True)
    e2 = jnp.exp(v2 - v1)
    w1 = (1.0 / (1.0 + e2)) * gp
    w2 = (e2 / (1.0 + e2)) * gp
    @pl.when(pl.program_id(0) == 0)
    def _():
        cnt_ref[...] = cnt0_ref[...]

    tm = logits.shape[0]
    oh1 = (lane == i1).astype(F32)
    oh2 = (lane == i2).astype(F32)
    tri = (lax.broadcasted_iota(jnp.int32, (tm, tm), 0) > lax.broadcasted_iota(jnp.int32, (tm, tm), 1)).astype(BF16)
    base = cnt_ref[0:1, :]
    c1 = jnp.sum(oh1, axis=0, keepdims=True)
    c2 = jnp.sum(oh2, axis=0, keepdims=True)
    r1 = jnp.sum(oh1 * (_dot(tri, oh1.astype(BF16)) + base), axis=1, keepdims=True)
    r2 = jnp.sum(oh2 * (_dot(tri, oh2.astype(BF16)) + (base + c1)), axis=1, keepdims=True)
    cnt_ref[...] = jnp.broadcast_to(base + c1 + c2, cnt_ref.shape)
    out = jnp.where(lane == 0, i1.astype(F32), 0.0)
    out = jnp.where(lane == 1, i2.astype(F32), out)
    out = jnp.where(lane == 2, w1, out)
    out = jnp.where(lane == 3, w2, out)
    out = jnp.where(lane == 4, r1, out)
    out = jnp.where(lane == 5, r2, out)
    o_ref[...] = out


def _router(x, row_off, n_rows, w_grp, b_grp, w_exp, b_exp, tile, counts0):
    pad = ROUTE_LANES - N_EXPERTS - N_GROUPS
    w = jnp.concatenate([w_exp, w_grp, jnp.zeros((D_MODEL, pad), F32)], axis=1)
    b = jnp.concatenate([b_exp, b_grp, jnp.zeros((pad,), F32)]).reshape(1, ROUTE_LANES)
    ob = row_off // tile
    const = lambda s: pl.BlockSpec(s, lambda i: (0, 0))
    cshape = (V7X_SUBLANES, ROUTE_LANES)
    return pl.pallas_call(
        _router_body,
        grid=(n_rows // tile,),
        in_specs=[pl.BlockSpec((tile, D_MODEL), lambda i: (i + ob, 0)), const(w.shape), const(b.shape), const(cshape)],
        out_specs=[pl.BlockSpec((tile, ROUTE_LANES), lambda i: (i, 0)), const(cshape)],
        out_shape=[jax.ShapeDtypeStruct((n_rows, ROUTE_LANES), F32), jax.ShapeDtypeStruct(cshape, F32)],
        compiler_params=_cparams(1, VMEM_LIMIT),
        name="moe_router",
    )(x, w, b, counts0)


DMA_UNROLL = True


def _dispatch_body(pos_ref, x_ref, xs_hbm, xbuf, sem, *, nt):
    t = pl.program_id(0)
    s = t % 2
    tile = x_ref.shape[0]
    xbuf[s] = x_ref[...]

    def body(j, c):
        src = xbuf.at[s, pl.ds(j, 1)]
        pltpu.make_async_copy(src, xs_hbm.at[pl.ds(pos_ref[0, 0, j], 1)], sem.at[s]).start(priority=0)
        pltpu.make_async_copy(src, xs_hbm.at[pl.ds(pos_ref[0, 0, tile + j], 1)], sem.at[s]).start(priority=1)
        return c
    lax.fori_loop(0, tile, body, 0, unroll=DMA_UNROLL)

    def drain(slot):
        for _ in range(2):
            pltpu.make_async_copy(xbuf.at[slot], xs_hbm.at[pl.ds(0, tile)], sem.at[slot]).wait()

    @pl.when(t >= 1)
    def _():
        drain(1 - s)

    @pl.when(t == nt - 1)
    def _():
        drain(s)


def _dispatch(x, pos, tile):
    m = x.shape[0]
    return pl.pallas_call(
        functools.partial(_dispatch_body, nt=m // tile),
        grid=(m // tile,),
        in_specs=[pl.BlockSpec((1, 1, 2 * tile), lambda i: (i, 0, 0), memory_space=pltpu.SMEM),
                  pl.BlockSpec((tile, D_MODEL), lambda i: (i, 0))],
        out_specs=pl.BlockSpec(memory_space=pl.ANY),
        out_shape=jax.ShapeDtypeStruct((2 * m, D_MODEL), F32),
        scratch_shapes=[pltpu.VMEM((2, tile, D_MODEL), F32), pltpu.SemaphoreType.DMA((2,))],
        compiler_params=_cparams(1),
        name="moe_dispatch",
    )(pos, x)


def _expert_body(it_ref, ie_ref, lo_ref, hi_ref, x_ref, wg_ref, wu_ref, wd_ref, y_ref, acc, *, n_items):
    w = pl.program_id(0)
    tile = it_ref[w]
    lo, hi = lo_ref[w], hi_ref[w]
    first = (w == 0) | (it_ref[jnp.maximum(w - 1, 0)] != tile)
    last = (w == n_items - 1) | (it_ref[jnp.minimum(w + 1, n_items - 1)] != tile)

    @pl.when(first)
    def _():
        acc[...] = jnp.zeros_like(acc)

    @pl.when(lo < hi)
    def _():
        xb = x_ref[...].astype(BF16)
        hg = _dot(xb, wg_ref[...])
        hu = _dot(xb, wu_ref[...])
        he = (hg * jax.nn.sigmoid(hg)) * hu
        y = _dot(he.astype(BF16), wd_ref[...])
        row = tile * EXPERT_ROWS + lax.broadcasted_iota(jnp.int32, (EXPERT_ROWS, 1), 0)
        acc[...] += jnp.where((row >= lo) & (row < hi), y, 0.0)

    @pl.when(last)
    def _():
        y_ref[...] = acc[...]


def _experts(xs, items, w_gate, w_up, w_down, layer):
    n_items = items[0].shape[0]
    rows = EXPERT_ROWS
    tspec = pl.BlockSpec((rows, D_MODEL), lambda w, it, ie, lo, hi: (it[w], 0))
    wspec = lambda shp: pl.BlockSpec((None, None) + shp, lambda w, it, ie, lo, hi: (layer, ie[w], 0, 0))
    grid_spec = pltpu.PrefetchScalarGridSpec(
        num_scalar_prefetch=4,
        grid=(n_items,),
        in_specs=[tspec, wspec((D_MODEL, D_EXPERT)), wspec((D_MODEL, D_EXPERT)), wspec((D_EXPERT, D_MODEL))],
        out_specs=tspec,
        scratch_shapes=[pltpu.VMEM((rows, D_MODEL), F32)])
    return pl.pallas_call(
        functools.partial(_expert_body, n_items=n_items),
        grid_spec=grid_spec,
        out_shape=jax.ShapeDtypeStruct(xs.shape, F32),
        compiler_params=_cparams(1, VMEM_LIMIT),
        name="moe_experts",
    )(*items, xs, w_gate, w_up, w_down)


def _moe_combine_body(pos_ref, h_ref, r_ref, g_ref, b_ref, ys_hbm, o_ref, ybuf, sem, *, nt):
    t = pl.program_id(0)
    tile = h_ref.shape[0]

    @pl.when(t < nt)
    def _():
        s = t % 2

        def body(j, c):
            for k in range(2):
                jj = j + k * tile
                pltpu.make_async_copy(ys_hbm.at[pl.ds(pos_ref[0, 0, jj], 1)], ybuf.at[s, pl.ds(jj, 1)],
                                      sem.at[s]).start(priority=k)
            return c
        lax.fori_loop(0, tile, body, 0, unroll=DMA_UNROLL)

    @pl.when(t >= 1)
    def _():
        s = (t + 1) % 2
        pltpu.make_async_copy(ys_hbm.at[pl.ds(0, 2 * tile)], ybuf.at[s], sem.at[s]).wait()
        r = r_ref[...]
        moe = r[:, 2:3] * ybuf[s, 0:tile, :] + r[:, 3:4] * ybuf[s, tile:2 * tile, :]
        o_ref[...] = _layer_norm_rows(ALPHA * h_ref[...] + moe, g_ref[...], b_ref[...])


def _moe_combine(h, ys, pos, route, g, b, tile, row_off, n_rows):
    ob = row_off // tile
    nt = n_rows // tile
    done = lambda i: jnp.maximum(i - 1, 0)
    row = lambda w: pl.BlockSpec((tile, w), lambda i: (done(i) + ob, 0))
    const = lambda s: pl.BlockSpec(s, lambda i: (0, 0))
    return pl.pallas_call(
        functools.partial(_moe_combine_body, nt=nt),
        grid=(nt + 1,),
        in_specs=[pl.BlockSpec((1, 1, 2 * tile), lambda i: (jnp.minimum(i, nt - 1) + ob, 0, 0),
                               memory_space=pltpu.SMEM),
                  row(D_MODEL), row(ROUTE_LANES), const((1, D_MODEL)), const((1, D_MODEL)),
                  pl.BlockSpec(memory_space=pl.ANY)],
        out_specs=pl.BlockSpec((tile, D_MODEL), lambda i: (done(i), 0)),
        out_shape=jax.ShapeDtypeStruct((n_rows, D_MODEL), F32),
        scratch_shapes=[pltpu.VMEM((2, 2 * tile, D_MODEL), F32), pltpu.SemaphoreType.DMA((2,))],
        compiler_params=_cparams(1, VMEM_LIMIT),
        name="moe_combine_ln",
    )(pos, h, route, g.reshape(1, D_MODEL), b.reshape(1, D_MODEL), ys)


def _dispatch_plan(route, counts, m, tile):
    rows = EXPERT_ROWS
    nt = (2 * m) // rows
    cnt = counts[0, 0:N_EXPERTS].astype(jnp.int32)
    end = jnp.cumsum(cnt)
    start = end - cnt
    e = route[:, 0:2].astype(jnp.int32)
    rank = route[:, 4:6].astype(jnp.int32)
    onehot = (e[:, :, None] == jnp.arange(N_EXPERTS, dtype=jnp.int32)[None, None, :])
    pos = jnp.sum(jnp.where(onehot, start[None, None, :], 0), axis=2) + rank
    pos = jnp.transpose(pos.reshape(m // tile, tile, 2), (0, 2, 1)).reshape(m // tile, 1, 2 * tile)
    first_t = start // rows
    n_it = jnp.where(cnt > 0, (end - 1) // rows - first_t + 1, 0)
    it_end = jnp.cumsum(n_it)
    n_items = nt + N_EXPERTS
    w = jnp.arange(n_items, dtype=jnp.int32)
    ie = jnp.minimum(jnp.sum((it_end[None, :] <= w[:, None]).astype(jnp.int32), axis=1), N_EXPERTS - 1)
    pick = lambda a: jnp.sum(jnp.where(ie[:, None] == jnp.arange(N_EXPERTS)[None, :], a[None, :], 0), axis=1)
    valid = w < it_end[-1]
    it = pick(first_t) + (w - (pick(it_end) - pick(n_it)))
    it = jnp.where(valid, it, nt - 1).astype(jnp.int32)
    last_e = jnp.sum(jnp.where(w == it_end[-1] - 1, ie, 0))
    ie = jnp.where(valid, ie, last_e).astype(jnp.int32)
    lo = jnp.where(valid, pick(start), 0).astype(jnp.int32)
    hi = jnp.where(valid, pick(end), 0).astype(jnp.int32)
    return pos.astype(jnp.int32), (it, ie, lo, hi)


def _moe_layer(h, n_big, w_grp, b_grp, w_exp, b_exp, w_gate, w_up, w_down, layer, g, b, parts):
    m = h.shape[0]
    tile = SMALL_TILE
    assert m % tile == 0 and (2 * m) % EXPERT_ROWS == 0 and n_big % ROW_TILE == 0
    zero_counts = jnp.zeros((V7X_SUBLANES, ROUTE_LANES), F32)
    route_a, counts = _router(h, 0, n_big, w_grp, b_grp, w_exp, b_exp, ROW_TILE, zero_counts)
    route_b, counts = _router(h, n_big, m - n_big, w_grp, b_grp, w_exp, b_exp, tile, counts)
    route = jnp.concatenate([route_a, route_b], axis=0)
    pos, items = _dispatch_plan(route, counts, m, tile)
    xs = _dispatch(h, pos, tile)
    ys = _experts(xs, items, w_gate, w_up, w_down, layer)
    return [_moe_combine(h, ys, pos, route, g, b, tile, off, n) for off, n in parts]


GATE_LANES = V7X_LANES
GATE_F0 = 2 * V7X_SUBLANES


def _log_sigmoid(x):
    return jnp.minimum(x, 0.0) - jnp.log1p(jnp.exp(-jnp.abs(x)))


def _silu(x):
    return x * jax.nn.sigmoid(x)


def _in_proj_b_body(x_ref, w_ref, cw_ref, bias_ref, p1_ref, p2_ref, p3_ref,
                    q_ref, kt_ref, v_ref, og_ref, gt_ref, gtt_ref, mq_ref, u_ref, carry, *, seq_rows):
    w = MLSTM_WIDTH
    lhs = _lhs(x_ref[...], w_ref.dtype == F32)
    tm = x_ref.shape[0]
    u = _mm(lhs, w_ref[:, 0:2 * w])
    row = lax.broadcasted_iota(jnp.int32, (tm, 1), 0)
    if seq_rows is None:
        @pl.when(pl.program_id(1) == 0)
        def _():
            carry[...] = p1_ref[...]
        uc = jnp.concatenate([carry[...], u], axis=0)
        shifted = [uc[V7X_SUBLANES - k:V7X_SUBLANES - k + tm, :] for k in (1, 2, 3)]
        carry[...] = u[tm - V7X_SUBLANES:, :]
        u_ref[...] = u[tm - (CONV_W - 1):, :]
    else:
        t = row % seq_rows
        prevs = (p1_ref[...], p2_ref[...], p3_ref[...])
        shifted = [jnp.where(t >= k, pltpu.roll(u, k, 0), prevs[k - 1]) for k in (1, 2, 3)]
        u_ref[...] = u
    cw = cw_ref[...]
    y = shifted[2] * cw[0:1, :]
    y = y + shifted[1] * cw[1:2, :]
    y = y + shifted[0] * cw[2:3, :]
    y = y + u * cw[3:4, :]
    qk = _silu(y)
    q_ref[...] = qk[:, 0:w].astype(BF16)
    kt_ref[...] = (qk[:, w:2 * w] * ATTN_SCALE).T.astype(BF16)
    v_ref[...] = _mm(lhs, w_ref[:, 2 * w:3 * w]).astype(BF16)
    og_ref[...] = jax.nn.sigmoid(_mm(lhs, w_ref[:, 3 * w:4 * w])).astype(BF16)
    gates = _mm(lhs, w_ref[:, 4 * w:4 * w + GATE_LANES]) + bias_ref[...]
    lane = lax.broadcasted_iota(jnp.int32, gates.shape, 1)
    is_f = (lane >= GATE_F0) & (lane < GATE_F0 + MLSTM_HEADS)
    gt = jnp.where(is_f, _log_sigmoid(gates), gates)
    gt_ref[...] = gt
    gtt_ref[...] = gt.T
    mq_ref[...] = (_mm(lhs, w_ref[:, 4 * w + GATE_LANES:]) * ATTN_SCALE).astype(mq_ref.dtype)


def _in_proj_b(x, x_off, m, w_bf, conv_w, bias, prevs, n_seq, seq_len, tile, per_tile_seqs):
    w = MLSTM_WIDTH
    tps = max(seq_len // tile, 1)
    ob = x_off // tile
    row = lambda width: pl.BlockSpec((tile, width), lambda b, i: (b * tps + i, 0))
    col = lambda height: pl.BlockSpec((height, tile), lambda b, i: (0, b * tps + i))
    const = lambda s: pl.BlockSpec(s, lambda b, i: (0, 0))
    if per_tile_seqs:
        pspecs = [row(2 * w)] * 3
        u_rows, u_spec = m, row(2 * w)
    else:
        pspecs = [pl.BlockSpec((V7X_SUBLANES, 2 * w), lambda b, i: (b, 0))] * 3
        u_rows = n_seq * (CONV_W - 1)
        u_spec = pl.BlockSpec((None, CONV_W - 1, 2 * w), lambda b, i: (b, 0, 0))
    u_shape = (jax.ShapeDtypeStruct((m, 2 * w), F32) if per_tile_seqs
               else jax.ShapeDtypeStruct((n_seq, CONV_W - 1, 2 * w), F32))
    return pl.pallas_call(
        functools.partial(_in_proj_b_body, seq_rows=seq_len if per_tile_seqs else None),
        grid=(m // (tps * tile), tps),
        in_specs=[pl.BlockSpec((tile, D_MODEL), lambda b, i: (b * tps + i + ob, 0)),
                  const(w_bf.shape), const(conv_w.shape), const(bias.shape)] + pspecs,
        out_specs=[row(w), col(w), row(w), row(w), row(GATE_LANES), col(GATE_LANES), row(MEM_WIDTH), u_spec],
        out_shape=[jax.ShapeDtypeStruct((m, w), BF16), jax.ShapeDtypeStruct((w, m), BF16),
                   jax.ShapeDtypeStruct((m, w), BF16), jax.ShapeDtypeStruct((m, w), BF16),
                   jax.ShapeDtypeStruct((m, GATE_LANES), F32), jax.ShapeDtypeStruct((GATE_LANES, m), F32),
                   jax.ShapeDtypeStruct((m, MEM_WIDTH), BF16), u_shape],
        scratch_shapes=[pltpu.VMEM((V7X_SUBLANES, 2 * w), F32)],
        compiler_params=_cparams(2, VMEM_LIMIT),
        name="in_proj_b",
    )(x, w_bf, conv_w, bias, *prevs)


def _split3_dot(x, sel, terms=3):
    x1 = x.astype(BF16)
    r1 = x - x1.astype(F32)
    x2 = r1.astype(BF16)
    if terms == 2:
        return _dot(x1, sel) + _dot(x2, sel)
    x3 = (r1 - x2.astype(F32)).astype(BF16)
    return _dot(x1, sel) + (_dot(x2, sel) + _dot(x3, sel))


def _scan_rows(x, op, fill):
    n = x.shape[0]
    row = lax.broadcasted_iota(jnp.int32, (n, 1), 0)
    sh = 1
    while sh < n:
        x = op(x, jnp.where(row >= sh, pltpu.roll(x, sh, 0), fill))
        sh *= 2
    return x


def _mlstm_body(q_ref, kt_ref, v_ref, og_ref, gt_ref, gtt_ref, ng_ref, ln_ref, st0_ref, m0_ref,
                cell_ref, st_out, m_out, st, m_rows, qk_s):
    c = pl.program_id(1)
    L = q_ref.shape[0]
    E = HEAD_DIM
    H = MLSTM_HEADS
    W = V7X_LANES

    @pl.when(c == 0)
    def _():
        st[...] = st0_ref[...]
        m_rows[...] = m0_ref[...]

    gt = gt_ref[...]
    b_cols = _scan_rows(gt, jnp.add, 0.0)
    r_cols = pltpu.roll(gt, GATE_F0, 1) - b_cols
    cm = _scan_rows(r_cols, jnp.maximum, NEG_INF)
    m_all = m_rows[...]
    sub = lax.broadcasted_iota(jnp.int32, m_all.shape, 0)
    m_lane = jnp.max(jnp.where(lax.broadcasted_iota(jnp.int32, m_all.shape, 1) == sub + GATE_F0, m_all, NEG_INF),
                     axis=0, keepdims=True)
    a_cols = jnp.maximum(m_lane, cm)
    wi_cols = jnp.exp(m_lane - a_cols)
    em_cols = jnp.exp(-(b_cols + a_cols))

    gtt = gtt_ref[0:2 * GATE_F0, :]
    lane_t = lax.broadcasted_iota(jnp.int32, (1, L), 1)
    b_rows = gtt
    sh = 1
    while sh < L:
        b_rows = b_rows + jnp.where(lane_t >= sh, pltpu.roll(b_rows, sh, 1), 0.0)
        sh *= 2
    r_rows = gtt[0:GATE_F0, :] - b_rows[GATE_F0:2 * GATE_F0, :]
    b_last = b_rows[GATE_F0:2 * GATE_F0, L - 1:L]
    m_prev = m_all[:, 0:1]
    m_new = jnp.maximum(b_last + m_prev, b_last + jnp.max(r_rows, axis=1, keepdims=True))
    decay = jnp.exp(b_last + m_prev - m_new)
    wk_scale = jnp.exp(b_last + r_rows - m_new)
    m_next = jnp.broadcast_to(m_new, m_all.shape)
    m_rows[...] = m_next
    m_out[...] = m_next

    causal = lax.broadcasted_iota(jnp.int32, (L, L), 0) >= lax.broadcasted_iota(jnp.int32, (L, L), 1)
    sub_k = lax.broadcasted_iota(jnp.int32, (W, 1), 0)
    lane_w = lax.broadcasted_iota(jnp.int32, (1, W), 1)
    ones_rhs = jnp.ones((L, W), BF16)

    for h in range(H):
        p, odd = h // 2, h % 2
        own_rows = (sub_k >= E) if odd else (sub_k < E)
        kt_pad = jnp.where(own_rows, kt_ref[p * W:(p + 1) * W, :], jnp.zeros((), BF16))
        s = _dot(q_ref[:, p * W:(p + 1) * W], kt_pad)
        a_bh = jnp.broadcast_to(a_cols[:, GATE_F0 + h:GATE_F0 + h + 1], (L, L))
        w_intra = jnp.exp(jnp.where(causal, r_rows[h:h + 1, :] - a_bh, NEG_INF))
        qk_s[h] = s * w_intra

    ln_sel = ln_ref[...]
    for p in range(H // 2):
        slab = slice(p * W, (p + 1) * W)
        q_pair = q_ref[:, slab]
        v_aug = jnp.concatenate([v_ref[:, slab], ones_rhs], axis=1)
        halves = []
        for odd in range(2):
            h = 2 * p + odd
            qk = qk_s[h].astype(BF16)
            wib = jnp.broadcast_to(wi_cols[:, GATE_F0 + h:GATE_F0 + h + 1], (L, W))
            emb = jnp.broadcast_to(em_cols[:, GATE_F0 + h:GATE_F0 + h + 1], (L, W))
            nd = _dot(qk, v_aug) + jnp.concatenate([wib, wib], axis=1) * _dot(q_pair, st[h].astype(BF16))
            halves.append(nd[:, 0:W] / jnp.maximum(jnp.abs(nd[:, W:2 * W]), emb))
        hv = jnp.where(lane_w < E, halves[0], halves[1])
        mu = _split3_dot(hv, ln_sel, 2) * (1.0 / E)
        hc = hv - mu
        var = _split3_dot(hc * hc, ln_sel, 2) * (1.0 / E)
        cell_ref[:, slab] = (og_ref[:, slab].astype(F32) * (hc * lax.rsqrt(var + LN_EPS) * ng_ref[:, slab])
                             ).astype(cell_ref.dtype)

    for h in range(H):
        p, odd = h // 2, h % 2
        rows = slice(odd * E, (odd + 1) * E)
        own_lanes = (lane_w >= E) if odd else (lane_w < E)
        wkt = (kt_ref[h * E:(h + 1) * E, :].astype(F32) * wk_scale[h:h + 1, :]).astype(BF16)
        d = decay[h:h + 1, :]
        v_aug = jnp.concatenate([v_ref[:, p * W:(p + 1) * W], ones_rhs], axis=1)
        own2 = jnp.concatenate([own_lanes, own_lanes], axis=1)
        st[h, rows, :] = d * st[h, rows, :] + jnp.where(own2, _dot(wkt, v_aug), 0.0)
    st_out[...] = st[...]


def _ln_selector():
    g = jnp.arange(V7X_LANES) // HEAD_DIM
    return (g[:, None] == g[None, :]).astype(BF16)


def _mlstm(q, kt, v, og, gt, gtt, norm_g, c0, n0, m0, n_seq, seq_len, chunk):
    w, hh, e, lanes = MLSTM_WIDTH, MLSTM_HEADS, HEAD_DIM, V7X_LANES
    nc = seq_len // chunk
    mrows = 2 * V7X_SUBLANES
    blk = lambda a: jnp.stack([jnp.pad(a[:, h], ((0, 0), ((h % 2) * e, (1 - h % 2) * e), ((h % 2) * e, (1 - h % 2) * e)))
                               for h in range(hh)], axis=1)
    unblk = lambda s: jnp.stack([s[:, h, (h % 2) * e:(h % 2 + 1) * e, (h % 2) * e:(h % 2 + 1) * e]
                                 for h in range(hh)], axis=1)
    st0 = jnp.concatenate([blk(c0), blk(jnp.broadcast_to(n0[..., None], (n_seq, hh, e, e)))], axis=-1)
    m0b = jnp.pad(jnp.broadcast_to(m0[..., None], (n_seq, hh, lanes)), ((0, 0), (0, mrows - hh), (0, 0)))
    ln = _ln_selector()
    row = lambda width: pl.BlockSpec((chunk, width), lambda b, c: (b * nc + c, 0))
    col = lambda height: pl.BlockSpec((height, chunk), lambda b, c: (0, b * nc + c))
    const = lambda a: pl.BlockSpec(a.shape, lambda b, c: (0,) * a.ndim)
    st = lambda shp: pl.BlockSpec((None,) + shp, lambda b, c: (b,) + (0,) * len(shp))
    slab = (hh, lanes, 2 * lanes)
    cell, st_out, m_out = pl.pallas_call(
        _mlstm_body,
        grid=(n_seq, nc),
        in_specs=[row(w), col(w), row(w), row(w), row(GATE_LANES), col(GATE_LANES),
                  pl.BlockSpec((1, w), lambda b, c: (0, 0)), const(ln),
                  st(slab), st((mrows, lanes))],
        out_specs=[row(w), st(slab), st((mrows, lanes))],
        out_shape=[jax.ShapeDtypeStruct((n_seq * seq_len, w), BF16),
                   jax.ShapeDtypeStruct((n_seq,) + slab, F32),
                   jax.ShapeDtypeStruct((n_seq, mrows, lanes), F32)],
        scratch_shapes=[pltpu.VMEM(slab, F32), pltpu.VMEM((mrows, lanes), F32),
                        pltpu.VMEM((hh, chunk, chunk), F32)],
        compiler_params=_cparams(2, VMEM_LIMIT),
        name="mlstm_chunks",
    )(q, kt, v, og, gt, gtt, norm_g.reshape(1, w), ln, st0, m0b)
    return cell, unblk(st_out[..., 0:lanes]), unblk(st_out[..., lanes:])[..., 0], m_out[:, 0:hh, 0]


def _w_in_b_regroup(w_in, b_if):
    w4 = 4 * MLSTM_WIDTH
    hh = MLSTM_HEADS
    z = lambda n: jnp.zeros((D_MODEL, n), w_in.dtype)
    w = jnp.concatenate([w_in[:, :w4 + hh], z(GATE_F0 - hh), w_in[:, w4 + hh:w4 + 2 * hh],
                         z(GATE_LANES - GATE_F0 - hh), w_in[:, w4 + 2 * hh:]], axis=1)
    zb = lambda n: jnp.zeros((n,), F32)
    bias = jnp.concatenate([b_if[0].astype(F32), zb(GATE_F0 - hh), b_if[1].astype(F32),
                            zb(GATE_LANES - GATE_F0 - hh)]).reshape(1, GATE_LANES)
    return w, bias


def _layer_b_prompt(h_all, w_bf, bias, conv_w, norm_g, w_out, mem_prompt, w_mem, ln_g, ln_b, n_seq, seq_len,
                    tail):
    m = n_seq * seq_len
    tile = min(ROW_TILE, seq_len)
    chunk = min(MLSTM_CHUNK, seq_len)
    zstate = jnp.zeros((n_seq * V7X_SUBLANES, 2 * MLSTM_WIDTH), F32)
    q, kt, v, og, gt, gtt, mq, conv = _in_proj_b(h_all, 0, m, w_bf, conv_w, bias, [zstate] * 3, n_seq, seq_len,
                                                 tile, False)
    hh, e = MLSTM_HEADS, HEAD_DIM
    cell, c_out, n_out, m_out = _mlstm(q, kt, v, og, gt, gtt, norm_g, jnp.zeros((n_seq, hh, e, e), F32),
                                       jnp.zeros((n_seq, hh, e), F32), jnp.zeros((n_seq, hh), F32),
                                       n_seq, seq_len, chunk)
    mem_kv = _matmul(mem_prompt.reshape(n_seq * N_MEM, D_MODEL), w_mem, N_MEM)
    h1 = _fused_out([cell], [], mq, mem_kv.reshape(n_seq, N_MEM, 2 * MEM_WIDTH), h_all, w_out.astype(BF16),
                    ln_g, ln_b, tile, seq_len, tail)
    return h1, conv, c_out, n_out.reshape(n_seq, hh, e), m_out.reshape(n_seq, hh), mem_kv


def _layer_b_sample(h_all, h_off, w_bf, bias, conv_w, norm_g, w_out, mem_kv, conv_state, c0, n0, m0, ln_g, ln_b,
                    n_b, n_new):
    m = n_b * n_new
    w2 = 2 * MLSTM_WIDTH
    hh, e = MLSTM_HEADS, HEAD_DIM
    chunk = MLSTM_CHUNK
    t = jnp.arange(n_new)
    prevs = []
    for kk in (1, 2, 3):
        idx = jnp.clip(CONV_W - 1 - kk + t, 0, CONV_W - 2)
        prevs.append(conv_state[:, idx, :].reshape(m, w2))
    q, kt, v, og, gt, gtt, mq, u = _in_proj_b(h_all, h_off, m, w_bf, conv_w, bias, prevs, n_b, n_new, m, True)
    conv_new = jnp.concatenate([conv_state, u.reshape(n_b, n_new, w2)], axis=1)[:, n_new:]
    npad = chunk - n_new
    pad3 = lambda a: jnp.pad(a.reshape(n_b, n_new, a.shape[1]), ((0, 0), (0, npad), (0, 0)))
    gate_pad = jnp.where(jnp.arange(GATE_LANES) < hh, NEG_INF, 0.0).astype(F32)
    gtp = jnp.concatenate([gt.reshape(n_b, n_new, GATE_LANES),
                           jnp.broadcast_to(gate_pad, (n_b, npad, GATE_LANES))], axis=1)
    gttp = jnp.concatenate([gtt.reshape(GATE_LANES, n_b, n_new),
                            jnp.broadcast_to(gate_pad[:, None, None], (GATE_LANES, n_b, npad))], axis=2)
    ktp = jnp.pad(kt.reshape(MLSTM_WIDTH, n_b, n_new), ((0, 0), (0, 0), (0, npad)))
    flat = lambda a: a.reshape(n_b * chunk, a.shape[2])
    flat_t = lambda a: a.reshape(a.shape[0], n_b * chunk)
    cell, c_out, n_out, m_out = _mlstm(flat(pad3(q)), flat_t(ktp), flat(pad3(v)), flat(pad3(og)), flat(gtp),
                                       flat_t(gttp), norm_g, c0, n0, m0, n_b, chunk, chunk)
    cell = cell.reshape(n_b, chunk, MLSTM_WIDTH)[:, :n_new].reshape(m, MLSTM_WIDTH)
    mo = _sample_mem_attn(mq.astype(F32).reshape(n_b, n_new, MEM_WIDTH),
                          mem_kv.reshape(n_b, N_MEM, 2 * MEM_WIDTH), n_b, n_new)
    h1 = _out_proj(cell, mo.reshape(m, MEM_WIDTH), h_all, h_off, w_out, ln_g, ln_b, m)
    return h1, conv_new, c_out, n_out.reshape(n_b, hh, e), m_out.reshape(n_b, hh)


def _layer_a_sample(h2d, w_in, w_out, mem_kv, caches, ln_g, ln_b, n_b, n_new):
    m = n_b * n_new
    pos = PAST_LEN + (jnp.arange(m, dtype=jnp.int32) % n_new)
    outs = _in_proj_a(h2d, w_in, _rope_tables(pos), n_new, n_b, m, full_kv=True)
    qs, ks, vs, mq, kvs = outs[0:3], outs[3:6], outs[6:9], outs[9], outs[10:13]
    f3 = lambda t: t.reshape(n_b, n_new, t.shape[1])
    mix = _sample_attn([f3(t) for t in qs], [f3(t) for t in ks], [f3(t) for t in vs], caches, n_b, n_new)
    mo = _sample_mem_attn(f3(mq), mem_kv.reshape(n_b, N_MEM, 2 * MEM_WIDTH), n_b, n_new)
    h1 = _out_proj(mix.reshape(m, DIL_WIDTH), mo.reshape(m, MEM_WIDTH), h2d, 0, w_out, ln_g, ln_b, m)
    return h1, kvs


def _layer_a_prompt(h2d, mem_prompt, w_in, w_out, w_mem, ln_g, ln_b, n_seq, seq_len, tail):
    tile = min(ROW_TILE, seq_len)
    tabs = _rope_tables(jnp.arange(seq_len, dtype=jnp.int32))
    outs = _in_proj_a(h2d, w_in.astype(BF16), tabs, seq_len, n_seq, tile, full_kv=False)
    qs, ks, vs, mq, kvs = outs[0:3], outs[3:6], outs[6:9], outs[9], outs[10:13]
    mem_kv = _matmul(mem_prompt.reshape(n_seq * N_MEM, D_MODEL), w_mem, N_MEM)
    os_, lses = [], []
    for g, (win, dil) in enumerate(DIL_PAIRS):
        o, lse = _dil_attn(qs[g], ks[g], vs[g], n_seq, seq_len, dil, win // dil)
        os_.append(o)
        lses.append(lse)
    h1 = _fused_out(os_, lses, mq, mem_kv.reshape(n_seq, N_MEM, 2 * MEM_WIDTH), h2d, w_out.astype(BF16),
                    ln_g, ln_b, tile, seq_len, tail)
    return h1, kvs, mem_kv


def kernel(x_prompt, x_sample, mem_prompt, cache_win1_kv, cache_win2_kv, cache_win3_kv, cache_mem_kv,
           state_mlstm_conv, state_mlstm_C, state_mlstm_n, state_mlstm_m,
           w_in_A, w_out_A, w_in_B, mlstm_conv_w, mlstm_b_if, mlstm_norm_g, w_out_B,
           w_mem_kv, ln_g, ln_b, w_grp, b_grp, w_exp, b_exp, w_gate, w_up, w_down):
    n_p, seq, d = x_prompt.shape
    n_s, t_new, _ = x_sample.shape
    assert d == D_MODEL and w_in_A.shape[0] == 1 and w_in_B.shape[0] == 1
    mp, ms = n_p * seq, n_s * t_new
    m_all = mp + ms
    assert mp % ROW_TILE == 0 and ms % SMALL_TILE == 0 and mp % ms == 0
    xp = x_prompt.reshape(mp, d)
    xs = x_sample.reshape(ms, d)

    wg_bf, wu_bf, wd_bf = w_gate.astype(BF16), w_up.astype(BF16), w_down.astype(BF16)

    def moe(h, i, parts):
        return _moe_layer(h, mp, w_grp[i], b_grp[i], w_exp[i], b_exp[i], wg_bf, wu_bf, wd_bf, i,
                          ln_g[i, 1], ln_b[i, 1], parts)

    caches = (cache_win1_kv[0], cache_win2_kv[0], cache_win3_kv[0])
    hs, kv_s = _layer_a_sample(xs, w_in_A[0], w_out_A[0], cache_mem_kv[0], caches, ln_g[0, 0], ln_b[0, 0],
                               n_s, t_new)
    h, kv_p, mem_kv0 = _layer_a_prompt(xp, mem_prompt, w_in_A[0], w_out_A[0], w_mem_kv[0], ln_g[0, 0], ln_b[0, 0],
                                       n_p, seq, hs)
    (h,) = moe(h, 0, [(0, m_all)])

    w_b, bias = _w_in_b_regroup(w_in_B[0], mlstm_b_if[0])
    w_bf = w_b.astype(BF16)
    hs, conv_s, c_s, nv_s, mm_s = _layer_b_sample(
        h, mp, w_b, bias, mlstm_conv_w[0], mlstm_norm_g[0], w_out_B[0], cache_mem_kv[1],
        state_mlstm_conv[0], state_mlstm_C[0], state_mlstm_n[0], state_mlstm_m[0],
        ln_g[1, 0], ln_b[1, 0], n_s, t_new)
    h1, conv_p, c_p, nv_p, mm_p, mem_kv1 = _layer_b_prompt(
        h, w_bf, bias, mlstm_conv_w[0], mlstm_norm_g[0], w_out_B[0], mem_prompt, w_mem_kv[1],
        ln_g[1, 0], ln_b[1, 0], n_p, seq, hs)
    yp, ys = moe(h1, 1, [(0, mp), (mp, ms)])

    kv_shape = (2, HEADS_PER_GROUP, HEAD_DIM)
    wins = []
    for g in range(N_DIL_GROUPS):
        wins.append(kv_p[g].reshape((1, n_p, -1) + kv_shape))
        wins.append(kv_s[g].reshape((1, n_s, t_new) + kv_shape))
    mem_kv_p = jnp.stack([mem_kv0, mem_kv1]).reshape((2, n_p, N_MEM, 2, MEM_HEADS, HEAD_DIM))
    return (yp.reshape(n_p, seq, d), ys.reshape(n_s, t_new, d), *wins,
            conv_p[None], conv_s[None], c_p[None], c_s[None], nv_p[None], nv_s[None], mm_p[None], mm_s[None],
            mem_kv_p)
```

```python
import functools

import jax
import jax.numpy as jnp
import numpy as np
from jax import lax
from jax.experimental import pallas as pl
from jax.experimental.pallas import tpu as pltpu

D_MODEL = 1024
HEAD_DIM = 64
ATTN_SCALE = HEAD_DIM ** -0.5
PAST_LEN = 16384
N_MEM = 256
MEM_HEADS = 4
MEM_WIDTH = MEM_HEADS * HEAD_DIM
DIL_PAIRS = ((128, 1), (512, 4), (2048, 16))
N_DIL_GROUPS = len(DIL_PAIRS)
HEADS_PER_GROUP = 4
DIL_WIDTH = HEADS_PER_GROUP * HEAD_DIM
QBLK = 128
ROPE_THETA = 500000.0
ROPE_DIMS = HEAD_DIM // 4
MLSTM_HEADS = 12
MLSTM_WIDTH = MLSTM_HEADS * HEAD_DIM
CONV_W = 4
N_GROUPS = 4
EXPERTS_PER_GROUP = 8
N_EXPERTS = N_GROUPS * EXPERTS_PER_GROUP
D_EXPERT = 256
DEPTH = 2
ALPHA = (2 * DEPTH) ** 0.25
LN_EPS = 1e-5

V7X_LANES = 128
V7X_SUBLANES = 8
V7X_VMEM_BYTES = 64 * 1024 * 1024
VMEM_LIMIT = 48 * 1024 * 1024

ROW_TILE = 512
SMALL_TILE = 128
MLSTM_CHUNK = 128
EXPERT_ROWS = 256

BF16 = jnp.bfloat16
F32 = jnp.float32
NEG_INF = float("-inf")


def _cparams(n_axes, vmem=None, flags=None):
    return pltpu.CompilerParams(dimension_semantics=("arbitrary",) * n_axes,
                                vmem_limit_bytes=vmem, flags=flags)


def _nt_dot(a, b):
    return lax.dot_general(a, b, (((1,), (1,)), ((), ())), preferred_element_type=F32)


def _dot(a, b):
    return jnp.dot(a, b, preferred_element_type=F32)


def _split_bf16(x):
    hi = x.astype(BF16)
    lo = (x - hi.astype(F32)).astype(BF16)
    return hi, lo


def _lhs(x, precise):
    return _split_bf16(x) if precise else (x.astype(BF16),)


def _mm(lhs, w):
    if len(lhs) == 1:
        return _dot(lhs[0], w)
    wh, wl = _split_bf16(w)
    return _dot(lhs[0], wh) + (_dot(lhs[0], wl) + _dot(lhs[1], wh))


def _rope_tables(pos):
    half = ROPE_DIMS // 2
    inv = jnp.power(ROPE_THETA, -jnp.arange(half, dtype=F32) * (2.0 / ROPE_DIMS))
    ang = pos.astype(F32)[:, None] * inv[None, :]
    cos, sin = jnp.cos(ang), jnp.sin(ang)
    n = pos.shape[0]
    one = jnp.ones((n, HEAD_DIM - ROPE_DIMS), F32)
    zero8 = jnp.zeros((n, half), F32)
    zrest = jnp.zeros((n, HEAD_DIM - ROPE_DIMS), F32)
    a = jnp.concatenate([cos, cos, one], axis=1)
    b = jnp.concatenate([zero8, sin, zrest], axis=1)
    c = jnp.concatenate([-sin, zero8, zrest], axis=1)
    rep = V7X_LANES // HEAD_DIM
    return jnp.tile(a, (1, rep)), jnp.tile(b, (1, rep)), jnp.tile(c, (1, rep))


def _rope_apply(x, ra, rb, rc):
    parts = []
    for s in range(x.shape[1] // V7X_LANES):
        v = x[:, s * V7X_LANES:(s + 1) * V7X_LANES]
        parts.append(v * ra + pltpu.roll(v, ROPE_DIMS // 2, 1) * rb
                     + pltpu.roll(v, V7X_LANES - ROPE_DIMS // 2, 1) * rc)
    return jnp.concatenate(parts, axis=1)


def _in_proj_a_body(x_ref, w_ref, ra_ref, rb_ref, rc_ref, *outs, kv_rows, precise):
    q_refs, k_refs, v_refs = outs[0:3], outs[3:6], outs[6:9]
    mq_ref = outs[9]
    kv_refs = outs[10:13]
    lhs = _lhs(x_ref[...], precise)
    ra, rb, rc = ra_ref[...], rb_ref[...], rc_ref[...]
    tm = x_ref.shape[0]
    gw = DIL_WIDTH
    for g in range(N_DIL_GROUPS):
        q = _rope_apply(_mm(lhs, w_ref[:, g * gw:(g + 1) * gw]), ra, rb, rc)
        k = _rope_apply(_mm(lhs, w_ref[:, (3 + g) * gw:(4 + g) * gw]), ra, rb, rc)
        v = _mm(lhs, w_ref[:, (6 + g) * gw:(7 + g) * gw])
        q_refs[g][...] = (q * ATTN_SCALE).astype(q_refs[g].dtype)
        k_refs[g][...] = k.astype(k_refs[g].dtype)
        v_refs[g][...] = v.astype(v_refs[g].dtype)
        r = kv_rows[g]
        kv_refs[g][:, 0:gw] = k[tm - r:, :]
        kv_refs[g][:, gw:2 * gw] = v[tm - r:, :]
    mq = _mm(lhs, w_ref[:, 9 * gw:9 * gw + MEM_WIDTH])
    mq_ref[...] = (mq * ATTN_SCALE).astype(mq_ref.dtype)


def _in_proj_a(x2d, w, tabs, seq_len, n_seq, tile, full_kv):
    w_bf = w
    act = F32 if full_kv else BF16
    m = x2d.shape[0]
    nt = m // tile
    gw = DIL_WIDTH
    row_spec = lambda w: pl.BlockSpec((tile, w), lambda i: (i, 0))
    if full_kv:
        tab_spec = pl.BlockSpec((tile, V7X_LANES), lambda i: (i, 0))
        kv_rows = (tile,) * 3
        kv_shapes = [jax.ShapeDtypeStruct((m, 2 * gw), F32)] * 3
        kv_specs = [row_spec(2 * gw)] * 3
    else:
        tps = seq_len // tile
        tab_spec = pl.BlockSpec((tile, V7X_LANES), lambda i: (i % tps, 0))
        kv_rows, kv_shapes, kv_specs = [], [], []
        for win, _ in DIL_PAIRS:
            wb = min(win, seq_len)
            r = min(wb, tile)
            nblk = wb // r
            kv_rows.append(r)
            kv_shapes.append(jax.ShapeDtypeStruct((n_seq * wb, 2 * gw), F32))
            kv_specs.append(pl.BlockSpec(
                (r, 2 * gw),
                lambda i, nblk=nblk: ((i // tps) * nblk + jnp.maximum(i % tps - (tps - nblk), 0), 0)))
        kv_rows = tuple(kv_rows)
    out_shapes = [jax.ShapeDtypeStruct((m, gw), act)] * 9 + [jax.ShapeDtypeStruct((m, MEM_WIDTH), act)] + kv_shapes
    out_specs = [row_spec(gw)] * 9 + [row_spec(MEM_WIDTH)] + kv_specs
    if not full_kv:
        for idx in (0, N_DIL_GROUPS, 2 * N_DIL_GROUPS):
            assert DIL_PAIRS[0][1] == 1
            out_shapes[idx] = jax.ShapeDtypeStruct((n_seq, seq_len, gw), act)
            out_specs[idx] = pl.BlockSpec((None, tile, gw), lambda i: (i // tps, i % tps, 0))
    return pl.pallas_call(
        functools.partial(_in_proj_a_body, kv_rows=kv_rows, precise=full_kv),
        grid=(nt,),
        in_specs=[row_spec(D_MODEL), pl.BlockSpec(w_bf.shape, lambda i: (0, 0)), tab_spec, tab_spec, tab_spec],
        out_specs=out_specs,
        out_shape=out_shapes,
        compiler_params=_cparams(1, VMEM_LIMIT),
        name="in_proj_a",
    )(x2d, w_bf, *tabs)


def _dil_attn_body(q_ref, k_ref, v_ref, kp_ref, vp_ref, o_ref, lse_ref, *, span):
    i = pl.program_id(2)
    q, k, v = q_ref[...], k_ref[...], v_ref[...]
    kp, vp = kp_ref[...], vp_ref[...]
    qi = lax.broadcasted_iota(jnp.int32, (QBLK, 2 * QBLK), 0) + QBLK
    ki = lax.broadcasted_iota(jnp.int32, (QBLK, 2 * QBLK), 1)
    band = (qi >= ki) & (qi - ki <= span) & ((i > 0) | (ki >= QBLK))
    lses = []
    for h in range(HEADS_PER_GROUP):
        hs = slice(h * HEAD_DIM, (h + 1) * HEAD_DIM)
        kc = jnp.concatenate([kp[:, hs], k[:, hs]], axis=0)
        vc = jnp.concatenate([vp[:, hs], v[:, hs]], axis=0)
        s = jnp.where(band, _nt_dot(q[:, hs], kc), NEG_INF)
        m = jnp.max(s, axis=1, keepdims=True)
        p = jnp.exp(s - m)
        den = jnp.sum(p, axis=1, keepdims=True)
        o_ref[:, hs] = _dot((p / den).astype(BF16), vc).astype(o_ref.dtype)
        lses.append(m + jnp.log(den))
    lse_ref[...] = jnp.concatenate(lses, axis=1)


def _dil_attn(q, k, v, n_seq, seq_len, dil, span):
    gw = DIL_WIDTH
    L = seq_len // dil
    nb = L // QBLK
    view = lambda t: t.reshape(n_seq, L, dil * gw)
    blk = pl.BlockSpec((None, QBLK, gw), lambda b, r, i: (b, i, r))
    prev = pl.BlockSpec((None, QBLK, gw), lambda b, r, i: (b, jnp.maximum(i - 1, 0), r))
    o, lse = pl.pallas_call(
        functools.partial(_dil_attn_body, span=span),
        grid=(n_seq, dil, nb),
        in_specs=[blk, blk, blk, prev, prev],
        out_specs=[blk, pl.BlockSpec((None, None, QBLK, HEADS_PER_GROUP), lambda b, r, i: (b, r, i, 0))],
        out_shape=[jax.ShapeDtypeStruct((n_seq, L, dil * gw), BF16),
                   jax.ShapeDtypeStruct((n_seq, dil, L, HEADS_PER_GROUP), F32)],
        compiler_params=_cparams(3),
        name=f"dil_attn_d{dil}",
    )(view(q), view(k), view(v), view(k), view(v))
    if dil > 1:
        o = o.reshape(n_seq * seq_len, gw)
    lse = jnp.transpose(lse, (0, 2, 1, 3)).reshape(n_seq * seq_len, HEADS_PER_GROUP)
    return o, lse


def _layer_norm_rows(x, g, b):
    mu = jnp.mean(x, axis=1, keepdims=True)
    xc = x - mu
    var = jnp.mean(xc * xc, axis=1, keepdims=True)
    return xc * lax.rsqrt(var + LN_EPS) * g + b


def _out_proj_body(a_ref, mo_ref, h_ref, w_ref, g_ref, b_ref, o_ref, *, precise):
    ka = a_ref.shape[1]
    y = _mm(_lhs(a_ref[...], precise), w_ref[0:ka, :]) + _mm(_lhs(mo_ref[...], precise), w_ref[ka:, :])
    o_ref[...] = _layer_norm_rows(ALPHA * h_ref[...] + y, g_ref[...], b_ref[...])


def _out_proj(a, mo, h, h_off, w, g, b, tile):
    m, ka = a.shape
    row = lambda width, off=0: pl.BlockSpec((tile, width), lambda i: (i + off // tile, 0))
    const = lambda s: pl.BlockSpec(s, lambda i: (0, 0))
    return pl.pallas_call(
        functools.partial(_out_proj_body, precise=(w.dtype == F32)),
        grid=(m // tile,),
        in_specs=[row(ka), row(MEM_WIDTH), row(D_MODEL, h_off), const(w.shape), const((1, D_MODEL)),
                  const((1, D_MODEL))],
        out_specs=row(D_MODEL),
        out_shape=jax.ShapeDtypeStruct((m, D_MODEL), F32),
        compiler_params=_cparams(1, VMEM_LIMIT),
        name="out_proj_ln",
    )(a, mo, h, w, g.reshape(1, D_MODEL), b.reshape(1, D_MODEL))


def _fused_out_body(*refs, n_parts, n_tiles, has_tail):
    parts = refs[0:n_parts]
    k = n_parts
    lses = refs[k:k + n_parts] if n_parts > 1 else ()
    k += len(lses)
    mq_ref, kv_ref, h_ref, w_ref, g_ref, b_ref = refs[k:k + 6]
    tail_ref = refs[k + 6] if has_tail else None
    o_ref = refs[-1]

    def rows():
        if n_parts == 1:
            a = parts[0][...]
        else:
            ls = [l[...] for l in lses]
            heads = []
            for h in range(HEADS_PER_GROUP):
                hs = slice(h * HEAD_DIM, (h + 1) * HEAD_DIM)
                lh = [l[:, h:h + 1] for l in ls]
                mx = jnp.maximum(jnp.maximum(lh[0], lh[1]), lh[2])
                e = [jnp.exp(x - mx) for x in lh]
                tot = e[0] + e[1] + e[2]
                acc = (e[0] / tot) * parts[0][:, hs].astype(F32)
                acc = acc + (e[1] / tot) * parts[1][:, hs].astype(F32)
                acc = acc + (e[2] / tot) * parts[2][:, hs].astype(F32)
                heads.append(acc)
            a = jnp.concatenate(heads, axis=1).astype(BF16)
        ka = a.shape[1]
        q = mq_ref[...]
        kv = kv_ref[...].astype(BF16)
        mos = []
        for h in range(MEM_HEADS):
            hs = slice(h * HEAD_DIM, (h + 1) * HEAD_DIM)
            vs = slice(MEM_WIDTH + h * HEAD_DIM, MEM_WIDTH + (h + 1) * HEAD_DIM)
            s = _nt_dot(q[:, hs], kv[:, hs])
            m = jnp.max(s, axis=1, keepdims=True)
            p = jnp.exp(s - m)
            den = jnp.sum(p, axis=1, keepdims=True)
            mos.append(_dot((p / den).astype(BF16), kv[:, vs]))
        mo = jnp.concatenate(mos, axis=1).astype(BF16)
        y = _dot(a, w_ref[0:ka, :]) + _dot(mo, w_ref[ka:, :])
        o_ref[...] = _layer_norm_rows(ALPHA * h_ref[...] + y, g_ref[...], b_ref[...])

    if not has_tail:
        rows()
    else:
        pl.when(pl.program_id(0) < n_tiles)(rows)

        @pl.when(pl.program_id(0) == n_tiles)
        def _():
            o_ref[0:tail_ref.shape[0], :] = tail_ref[...]


def _fused_out(parts, lses, mq, mem_kv, h, w_bf, g, b, tile, seq_len, tail):
    m = mq.shape[0]
    nt = m // tile
    last = nt - 1
    tps = seq_len // tile
    clamp = lambda i: jnp.minimum(i, last)
    row = lambda w: pl.BlockSpec((tile, w), lambda i: (clamp(i), 0))
    const = lambda s: pl.BlockSpec(s, lambda i: (0,) * len(s))
    row3 = lambda w: pl.BlockSpec((None, tile, w), lambda i: (clamp(i) // tps, clamp(i) % tps, 0))
    in_specs = [row(p.shape[1]) if p.ndim == 2 else row3(p.shape[2]) for p in parts] + [row(l.shape[1]) for l in lses]
    in_specs += [row(MEM_WIDTH), pl.BlockSpec((None, N_MEM, 2 * MEM_WIDTH), lambda i: (clamp(i) // tps, 0, 0)),
                 row(D_MODEL), const(w_bf.shape), const((1, D_MODEL)), const((1, D_MODEL))]
    args = list(parts) + list(lses) + [mq, mem_kv, h, w_bf, g.reshape(1, D_MODEL), b.reshape(1, D_MODEL)]
    out_rows, steps = m, nt
    if tail is not None:
        assert tail.shape[0] <= tile
        in_specs.append(const(tail.shape))
        args.append(tail)
        out_rows, steps = m + tail.shape[0], nt + 1
    return pl.pallas_call(
        functools.partial(_fused_out_body, n_parts=len(parts), n_tiles=nt, has_tail=tail is not None),
        grid=(steps,),
        in_specs=in_specs,
        out_specs=pl.BlockSpec((tile, D_MODEL), lambda i: (i, 0)),
        out_shape=jax.ShapeDtypeStruct((out_rows, D_MODEL), F32),
        compiler_params=_cparams(1, VMEM_LIMIT),
        name="attn_out_proj_ln",
    )(*args)


def _matmul_body(x_ref, w_ref, o_ref):
    o_ref[...] = _dot(x_ref[...].astype(BF16), w_ref[...].astype(BF16))


def _matmul(x, w, tile):
    m, k = x.shape
    n = w.shape[1]
    return pl.pallas_call(
        _matmul_body,
        grid=(m // tile,),
        in_specs=[pl.BlockSpec((tile, k), lambda i: (i, 0)), pl.BlockSpec((k, n), lambda i: (0, 0))],
        out_specs=pl.BlockSpec((tile, n), lambda i: (i, 0)),
        out_shape=jax.ShapeDtypeStruct((m, n), F32),
        compiler_params=_cparams(1),
        name="mem_kv_proj",
    )(x, w)


def _col_attend(q_row, kmat, vmat, kmask, knew, vnew, nmask):
    pk = kmat * q_row
    pn = None if knew is None else knew * q_row
    outs, lses = [], []
    for h in range(HEADS_PER_GROUP):
        hs = slice(h * HEAD_DIM, (h + 1) * HEAD_DIM)
        s = jnp.sum(pk[:, hs], axis=1, keepdims=True)
        if kmask is not None:
            s = jnp.where(kmask, s, NEG_INF)
        m = jnp.max(s, axis=0, keepdims=True)
        if pn is not None:
            sn = jnp.where(nmask, jnp.sum(pn[:, hs], axis=1, keepdims=True), NEG_INF)
            m = jnp.maximum(m, jnp.max(sn, axis=0, keepdims=True))
        p = jnp.exp(s - m)
        den = jnp.sum(p, axis=0, keepdims=True)
        acc = jnp.sum(p * vmat[:, hs], axis=0, keepdims=True)
        if pn is not None:
            pnw = jnp.exp(sn - m)
            den = den + jnp.sum(pnw, axis=0, keepdims=True)
            acc = acc + jnp.sum(pnw * vnew[:, hs], axis=0, keepdims=True)
        outs.append(acc / den)
        lses.append(m + jnp.log(den))
    return outs, lses


def _sample_attn_body(q1, q2, q3, kn1, kn2, kn3, vn1, vn2, vn3, c1, c2, c3, mix_ref, *, n_new):
    qs = (q1[...], q2[...], q3[...])
    kns = (kn1[...], kn2[...], kn3[...])
    vns = (vn1[...], vn2[...], vn3[...])
    gw = DIL_WIDTH
    t_idx = lax.broadcasted_iota(jnp.int32, (n_new, 1), 0)
    rows = []
    for t in range(n_new):
        per_group = []
        for g, (win, dil) in enumerate(DIL_PAIRS):
            q_row = qs[g][t:t + 1, :]
            cache = (c1, c2, c3)[g]
            if dil == 1:
                kmat, vmat = cache[:, 0:gw], cache[:, gw:2 * gw]
                r_idx = lax.broadcasted_iota(jnp.int32, (kmat.shape[0], 1), 0)
                per_group.append(_col_attend(q_row, kmat, vmat, r_idx >= t, kns[g], vns[g], t_idx <= t))
            else:
                base = t * 2 * gw
                kmat, vmat = cache[:, base:base + gw], cache[:, base + gw:base + 2 * gw]
                per_group.append(_col_attend(q_row, kmat, vmat, None, kns[g], vns[g], t_idx == t))
        heads = []
        for h in range(HEADS_PER_GROUP):
            lh = [per_group[g][1][h] for g in range(N_DIL_GROUPS)]
            mx = jnp.maximum(jnp.maximum(lh[0], lh[1]), lh[2])
            e = [jnp.exp(x - mx) for x in lh]
            tot = e[0] + e[1] + e[2]
            acc = (e[0] / tot) * per_group[0][0][h]
            acc = acc + (e[1] / tot) * per_group[1][0][h]
            acc = acc + (e[2] / tot) * per_group[2][0][h]
            heads.append(acc)
        rows.append(jnp.concatenate(heads, axis=1))
    mix_ref[...] = jnp.concatenate(rows, axis=0)


def _sample_attn(qs, kns, vns, caches, n_b, n_new):
    gw = DIL_WIDTH
    small = pl.BlockSpec((None, n_new, gw), lambda b: (b, 0, 0))
    cviews, cspecs = [], []
    for c, (win, dil) in zip(caches, DIL_PAIRS):
        blocks = win // dil
        res = min(dil, n_new)
        picked = c.reshape((n_b, blocks, dil) + c.shape[2:])[:, :, 0:res]
        cviews.append(picked.reshape(n_b, blocks, res * 2 * gw))
        cspecs.append(pl.BlockSpec((None, blocks, res * 2 * gw), lambda b: (b, 0, 0)))
    return pl.pallas_call(
        functools.partial(_sample_attn_body, n_new=n_new),
        grid=(n_b,),
        in_specs=[small] * 9 + cspecs,
        out_specs=small,
        out_shape=jax.ShapeDtypeStruct((n_b, n_new, gw), F32),
        compiler_params=_cparams(1, VMEM_LIMIT),
        name="sample_dil_attn",
    )(*qs, *kns, *vns, *cviews)


def _sample_mem_attn_body(q_ref, kv_ref, o_ref, *, n_new):
    q = q_ref[...]
    kmat, vmat = kv_ref[:, 0:MEM_WIDTH], kv_ref[:, MEM_WIDTH:2 * MEM_WIDTH]
    rows = []
    for t in range(n_new):
        outs, _ = _col_attend(q[t:t + 1, :], kmat, vmat, None, None, None, None)
        rows.append(jnp.concatenate(outs, axis=1))
    o_ref[...] = jnp.concatenate(rows, axis=0)


def _sample_mem_attn(q, mem_kv, n_b, n_new):
    small = pl.BlockSpec((None, n_new, MEM_WIDTH), lambda b: (b, 0, 0))
    return pl.pallas_call(
        functools.partial(_sample_mem_attn_body, n_new=n_new),
        grid=(n_b,),
        in_specs=[small, pl.BlockSpec((None, N_MEM, 2 * MEM_WIDTH), lambda b: (b, 0, 0))],
        out_specs=small,
        out_shape=jax.ShapeDtypeStruct((n_b, n_new, MEM_WIDTH), F32),
        compiler_params=_cparams(1),
        name="sample_mem_attn",
    )(q, mem_kv)


ROUTE_LANES = V7X_LANES


def _router_body(x_ref, w_ref, b_ref, cnt0_ref, o_ref, cnt_ref):
    xh, xl = _split_bf16(x_ref[...])
    wh, wl = _split_bf16(w_ref[...])
    logits = _dot(xh, wh) + (_dot(xh, wl) + _dot(xl, wh)) + b_ref[...]
    lane = lax.broadcasted_iota(jnp.int32, logits.shape, 1)
    big = jnp.int32(ROUTE_LANES)
    is_grp = (lane >= N_EXPERTS) & (lane < N_EXPERTS + N_GROUPS)
    gl = jnp.where(is_grp, logits, NEG_INF)
    gmax = jnp.max(gl, axis=1, keepdims=True)
    gsel = jnp.min(jnp.where(gl == gmax, lane, big), axis=1, keepdims=True) - N_EXPERTS
    gp = 1.0 / jnp.sum(jnp.exp(gl - gmax), axis=1, keepdims=True)
    in_grp = (lane < N_EXPERTS) & ((lane // EXPERTS_PER_GROUP) == gsel)
    el = jnp.where(in_grp, logits, NEG_INF)
    v1 = jnp.max(el, axis=1, keepdims=True)
    i1 = jnp.min(jnp.where(el == v1, lane, big), axis=1, keepdims=True)
    el2 = jnp.where(lane == i1, NEG_INF, el)
    v2 = jnp.max(el2, axis=1, keepdims=True)
    i2 = jnp.min(jnp.where(el2 == v2, lane, big), axis=1, keepdims=True)
    e2 = jnp.exp(v2 - v1)
    w1 = (1.0 / (1.0 + e2)) * gp
    w2 = (e2 / (1.0 + e2)) * gp
    @pl.when(pl.program_id(0) == 0)
    def _():
        cnt_ref[...] = cnt0_ref[...]

    tm = logits.shape[0]
    oh1 = (lane == i1).astype(F32)
    oh2 = (lane == i2).astype(F32)
    tri = (lax.broadcasted_iota(jnp.int32, (tm, tm), 0) > lax.broadcasted_iota(jnp.int32, (tm, tm), 1)).astype(BF16)
    base = cnt_ref[0:1, :]
    c1 = jnp.sum(oh1, axis=0, keepdims=True)
    c2 = jnp.sum(oh2, axis=0, keepdims=True)
    r1 = jnp.sum(oh1 * (_dot(tri, oh1.astype(BF16)) + base), axis=1, keepdims=True)
    r2 = jnp.sum(oh2 * (_dot(tri, oh2.astype(BF16)) + (base + c1)), axis=1, keepdims=True)
    cnt_ref[...] = jnp.broadcast_to(base + c1 + c2, cnt_ref.shape)
    out = jnp.where(lane == 0, i1.astype(F32), 0.0)
    out = jnp.where(lane == 1, i2.astype(F32), out)
    out = jnp.where(lane == 2, w1, out)
    out = jnp.where(lane == 3, w2, out)
    out = jnp.where(lane == 4, r1, out)
    out = jnp.where(lane == 5, r2, out)
    o_ref[...] = out


def _router(x, row_off, n_rows, w_grp, b_grp, w_exp, b_exp, tile, counts0):
    pad = ROUTE_LANES - N_EXPERTS - N_GROUPS
    w = jnp.concatenate([w_exp, w_grp, jnp.zeros((D_MODEL, pad), F32)], axis=1)
    b = jnp.concatenate([b_exp, b_grp, jnp.zeros((pad,), F32)]).reshape(1, ROUTE_LANES)
    ob = row_off // tile
    const = lambda s: pl.BlockSpec(s, lambda i: (0, 0))
    cshape = (V7X_SUBLANES, ROUTE_LANES)
    return pl.pallas_call(
        _router_body,
        grid=(n_rows // tile,),
        in_specs=[pl.BlockSpec((tile, D_MODEL), lambda i: (i + ob, 0)), const(w.shape), const(b.shape), const(cshape)],
        out_specs=[pl.BlockSpec((tile, ROUTE_LANES), lambda i: (i, 0)), const(cshape)],
        out_shape=[jax.ShapeDtypeStruct((n_rows, ROUTE_LANES), F32), jax.ShapeDtypeStruct(cshape, F32)],
        compiler_params=_cparams(1, VMEM_LIMIT),
        name="moe_router",
    )(x, w, b, counts0)


DMA_UNROLL = True


def _dispatch_body(pos_ref, x_ref, xs_hbm, xbuf, sem, *, nt):
    t = pl.program_id(0)
    s = t % 2
    tile = x_ref.shape[0]
    xbuf[s] = x_ref[...]

    def body(j, c):
        src = xbuf.at[s, pl.ds(j, 1)]
        pltpu.make_async_copy(src, xs_hbm.at[pl.ds(pos_ref[0, 0, j], 1)], sem.at[s]).start(priority=0)
        pltpu.make_async_copy(src, xs_hbm.at[pl.ds(pos_ref[0, 0, tile + j], 1)], sem.at[s]).start(priority=1)
        return c
    lax.fori_loop(0, tile, body, 0, unroll=DMA_UNROLL)

    def drain(slot):
        for _ in range(2):
            pltpu.make_async_copy(xbuf.at[slot], xs_hbm.at[pl.ds(0, tile)], sem.at[slot]).wait()

    @pl.when(t >= 1)
    def _():
        drain(1 - s)

    @pl.when(t == nt - 1)
    def _():
        drain(s)


def _dispatch(x, pos, tile):
    m = x.shape[0]
    return pl.pallas_call(
        functools.partial(_dispatch_body, nt=m // tile),
        grid=(m // tile,),
        in_specs=[pl.BlockSpec((1, 1, 2 * tile), lambda i: (i, 0, 0), memory_space=pltpu.SMEM),
                  pl.BlockSpec((tile, D_MODEL), lambda i: (i, 0))],
        out_specs=pl.BlockSpec(memory_space=pl.ANY),
        out_shape=jax.ShapeDtypeStruct((2 * m, D_MODEL), F32),
        scratch_shapes=[pltpu.VMEM((2, tile, D_MODEL), F32), pltpu.SemaphoreType.DMA((2,))],
        compiler_params=_cparams(1),
        name="moe_dispatch",
    )(pos, x)


def _expert_body(it_ref, ie_ref, lo_ref, hi_ref, x_ref, wg_ref, wu_ref, wd_ref, y_ref, acc, *, n_items):
    w = pl.program_id(0)
    tile = it_ref[w]
    lo, hi = lo_ref[w], hi_ref[w]
    first = (w == 0) | (it_ref[jnp.maximum(w - 1, 0)] != tile)
    last = (w == n_items - 1) | (it_ref[jnp.minimum(w + 1, n_items - 1)] != tile)

    @pl.when(first)
    def _():
        acc[...] = jnp.zeros_like(acc)

    @pl.when(lo < hi)
    def _():
        xb = x_ref[...].astype(BF16)
        hg = _dot(xb, wg_ref[...])
        hu = _dot(xb, wu_ref[...])
        he = (hg * jax.nn.sigmoid(hg)) * hu
        y = _dot(he.astype(BF16), wd_ref[...])
        row = tile * EXPERT_ROWS + lax.broadcasted_iota(jnp.int32, (EXPERT_ROWS, 1), 0)
        acc[...] += jnp.where((row >= lo) & (row < hi), y, 0.0)

    @pl.when(last)
    def _():
        y_ref[...] = acc[...]


def _experts(xs, items, w_gate, w_up, w_down, layer):
    n_items = items[0].shape[0]
    rows = EXPERT_ROWS
    tspec = pl.BlockSpec((rows, D_MODEL), lambda w, it, ie, lo, hi: (it[w], 0))
    wspec = lambda shp: pl.BlockSpec((None, None) + shp, lambda w, it, ie, lo, hi: (layer, ie[w], 0, 0))
    grid_spec = pltpu.PrefetchScalarGridSpec(
        num_scalar_prefetch=4,
        grid=(n_items,),
        in_specs=[tspec, wspec((D_MODEL, D_EXPERT)), wspec((D_MODEL, D_EXPERT)), wspec((D_EXPERT, D_MODEL))],
        out_specs=tspec,
        scratch_shapes=[pltpu.VMEM((rows, D_MODEL), F32)])
    return pl.pallas_call(
        functools.partial(_expert_body, n_items=n_items),
        grid_spec=grid_spec,
        out_shape=jax.ShapeDtypeStruct(xs.shape, F32),
        compiler_params=_cparams(1, VMEM_LIMIT),
        name="moe_experts",
    )(*items, xs, w_gate, w_up, w_down)


def _moe_combine_body(pos_ref, h_ref, r_ref, g_ref, b_ref, ys_hbm, o_ref, ybuf, sem, *, nt):
    t = pl.program_id(0)
    tile = h_ref.shape[0]

    @pl.when(t < nt)
    def _():
        s = t % 2

        def body(j, c):
            for k in range(2):
                jj = j + k * tile
                pltpu.make_async_copy(ys_hbm.at[pl.ds(pos_ref[0, 0, jj], 1)], ybuf.at[s, pl.ds(jj, 1)],
                                      sem.at[s]).start(priority=k)
            return c
        lax.fori_loop(0, tile, body, 0, unroll=DMA_UNROLL)

    @pl.when(t >= 1)
    def _():
        s = (t + 1) % 2
        pltpu.make_async_copy(ys_hbm.at[pl.ds(0, 2 * tile)], ybuf.at[s], sem.at[s]).wait()
        r = r_ref[...]
        moe = r[:, 2:3] * ybuf[s, 0:tile, :] + r[:, 3:4] * ybuf[s, tile:2 * tile, :]
        o_ref[...] = _layer_norm_rows(ALPHA * h_ref[...] + moe, g_ref[...], b_ref[...])


def _moe_combine(h, ys, pos, route, g, b, tile, row_off, n_rows):
    ob = row_off // tile
    nt = n_rows // tile
    done = lambda i: jnp.maximum(i - 1, 0)
    row = lambda w: pl.BlockSpec((tile, w), lambda i: (done(i) + ob, 0))
    const = lambda s: pl.BlockSpec(s, lambda i: (0, 0))
    return pl.pallas_call(
        functools.partial(_moe_combine_body, nt=nt),
        grid=(nt + 1,),
        in_specs=[pl.BlockSpec((1, 1, 2 * tile), lambda i: (jnp.minimum(i, nt - 1) + ob, 0, 0),
                               memory_space=pltpu.SMEM),
                  row(D_MODEL), row(ROUTE_LANES), const((1, D_MODEL)), const((1, D_MODEL)),
                  pl.BlockSpec(memory_space=pl.ANY)],
        out_specs=pl.BlockSpec((tile, D_MODEL), lambda i: (done(i), 0)),
        out_shape=jax.ShapeDtypeStruct((n_rows, D_MODEL), F32),
        scratch_shapes=[pltpu.VMEM((2, 2 * tile, D_MODEL), F32), pltpu.SemaphoreType.DMA((2,))],
        compiler_params=_cparams(1, VMEM_LIMIT),
        name="moe_combine_ln",
    )(pos, h, route, g.reshape(1, D_MODEL), b.reshape(1, D_MODEL), ys)


def _dispatch_plan(route, counts, m, tile):
    rows = EXPERT_ROWS
    nt = (2 * m) // rows
    cnt = counts[0, 0:N_EXPERTS].astype(jnp.int32)
    end = jnp.cumsum(cnt)
    start = end - cnt
    e = route[:, 0:2].astype(jnp.int32)
    rank = route[:, 4:6].astype(jnp.int32)
    onehot = (e[:, :, None] == jnp.arange(N_EXPERTS, dtype=jnp.int32)[None, None, :])
    pos = jnp.sum(jnp.where(onehot, start[None, None, :], 0), axis=2) + rank
    pos = jnp.transpose(pos.reshape(m // tile, tile, 2), (0, 2, 1)).reshape(m // tile, 1, 2 * tile)
    first_t = start // rows
    n_it = jnp.where(cnt > 0, (end - 1) // rows - first_t + 1, 0)
    it_end = jnp.cumsum(n_it)
    n_items = nt + N_EXPERTS
    w = jnp.arange(n_items, dtype=jnp.int32)
    ie = jnp.minimum(jnp.sum((it_end[None, :] <= w[:, None]).astype(jnp.int32), axis=1), N_EXPERTS - 1)
    pick = lambda a: jnp.sum(jnp.where(ie[:, None] == jnp.arange(N_EXPERTS)[None, :], a[None, :], 0), axis=1)
    valid = w < it_end[-1]
    it = pick(first_t) + (w - (pick(it_end) - pick(n_it)))
    it = jnp.where(valid, it, nt - 1).astype(jnp.int32)
    last_e = jnp.sum(jnp.where(w == it_end[-1] - 1, ie, 0))
    ie = jnp.where(valid, ie, last_e).astype(jnp.int32)
    lo = jnp.where(valid, pick(start), 0).astype(jnp.int32)
    hi = jnp.where(valid, pick(end), 0).astype(jnp.int32)
    return pos.astype(jnp.int32), (it, ie, lo, hi)


def _moe_layer(h, n_big, w_grp, b_grp, w_exp, b_exp, w_gate, w_up, w_down, layer, g, b, parts):
    m = h.shape[0]
    tile = SMALL_TILE
    assert m % tile == 0 and (2 * m) % EXPERT_ROWS == 0 and n_big % ROW_TILE == 0
    zero_counts = jnp.zeros((V7X_SUBLANES, ROUTE_LANES), F32)
    route_a, counts = _router(h, 0, n_big, w_grp, b_grp, w_exp, b_exp, ROW_TILE, zero_counts)
    route_b, counts = _router(h, n_big, m - n_big, w_grp, b_grp, w_exp, b_exp, tile, counts)
    route = jnp.concatenate([route_a, route_b], axis=0)
    pos, items = _dispatch_plan(route, counts, m, tile)
    xs = _dispatch(h, pos, tile)
    ys = _experts(xs, items, w_gate, w_up, w_down, layer)
    return [_moe_combine(h, ys, pos, route, g, b, tile, off, n) for off, n in parts]


GATE_LANES = V7X_LANES
GATE_F0 = 2 * V7X_SUBLANES


def _log_sigmoid(x):
    return jnp.minimum(x, 0.0) - jnp.log1p(jnp.exp(-jnp.abs(x)))


def _silu(x):
    return x * jax.nn.sigmoid(x)


def _in_proj_b_body(x_ref, w_ref, cw_ref, bias_ref, p1_ref, p2_ref, p3_ref,
                    q_ref, kt_ref, v_ref, og_ref, gt_ref, gtt_ref, mq_ref, u_ref, carry, *, seq_rows):
    w = MLSTM_WIDTH
    lhs = _lhs(x_ref[...], w_ref.dtype == F32)
    tm = x_ref.shape[0]
    u = _mm(lhs, w_ref[:, 0:2 * w])
    row = lax.broadcasted_iota(jnp.int32, (tm, 1), 0)
    if seq_rows is None:
        @pl.when(pl.program_id(1) == 0)
        def _():
            carry[...] = p1_ref[...]
        uc = jnp.concatenate([carry[...], u], axis=0)
        shifted = [uc[V7X_SUBLANES - k:V7X_SUBLANES - k + tm, :] for k in (1, 2, 3)]
        carry[...] = u[tm - V7X_SUBLANES:, :]
        u_ref[...] = u[tm - (CONV_W - 1):, :]
    else:
        t = row % seq_rows
        prevs = (p1_ref[...], p2_ref[...], p3_ref[...])
        shifted = [jnp.where(t >= k, pltpu.roll(u, k, 0), prevs[k - 1]) for k in (1, 2, 3)]
        u_ref[...] = u
    cw = cw_ref[...]
    y = shifted[2] * cw[0:1, :]
    y = y + shifted[1] * cw[1:2, :]
    y = y + shifted[0] * cw[2:3, :]
    y = y + u * cw[3:4, :]
    qk = _silu(y)
    q_ref[...] = qk[:, 0:w].astype(BF16)
    kt_ref[...] = (qk[:, w:2 * w] * ATTN_SCALE).T.astype(BF16)
    v_ref[...] = _mm(lhs, w_ref[:, 2 * w:3 * w]).astype(BF16)
    og_ref[...] = jax.nn.sigmoid(_mm(lhs, w_ref[:, 3 * w:4 * w])).astype(BF16)
    gates = _mm(lhs, w_ref[:, 4 * w:4 * w + GATE_LANES]) + bias_ref[...]
    lane = lax.broadcasted_iota(jnp.int32, gates.shape, 1)
    is_f = (lane >= GATE_F0) & (lane < GATE_F0 + MLSTM_HEADS)
    gt = jnp.where(is_f, _log_sigmoid(gates), gates)
    gt_ref[...] = gt
    gtt_ref[...] = gt.T
    mq_ref[...] = (_mm(lhs, w_ref[:, 4 * w + GATE_LANES:]) * ATTN_SCALE).astype(mq_ref.dtype)


def _in_proj_b(x, x_off, m, w_bf, conv_w, bias, prevs, n_seq, seq_len, tile, per_tile_seqs):
    w = MLSTM_WIDTH
    tps = max(seq_len // tile, 1)
    ob = x_off // tile
    row = lambda width: pl.BlockSpec((tile, width), lambda b, i: (b * tps + i, 0))
    col = lambda height: pl.BlockSpec((height, tile), lambda b, i: (0, b * tps + i))
    const = lambda s: pl.BlockSpec(s, lambda b, i: (0, 0))
    if per_tile_seqs:
        pspecs = [row(2 * w)] * 3
        u_rows, u_spec = m, row(2 * w)
    else:
        pspecs = [pl.BlockSpec((V7X_SUBLANES, 2 * w), lambda b, i: (b, 0))] * 3
        u_rows = n_seq * (CONV_W - 1)
        u_spec = pl.BlockSpec((None, CONV_W - 1, 2 * w), lambda b, i: (b, 0, 0))
    u_shape = (jax.ShapeDtypeStruct((m, 2 * w), F32) if per_tile_seqs
               else jax.ShapeDtypeStruct((n_seq, CONV_W - 1, 2 * w), F32))
    return pl.pallas_call(
        functools.partial(_in_proj_b_body, seq_rows=seq_len if per_tile_seqs else None),
        grid=(m // (tps * tile), tps),
        in_specs=[pl.BlockSpec((tile, D_MODEL), lambda b, i: (b * tps + i + ob, 0)),
                  const(w_bf.shape), const(conv_w.shape), const(bias.shape)] + pspecs,
        out_specs=[row(w), col(w), row(w), row(w), row(GATE_LANES), col(GATE_LANES), row(MEM_WIDTH), u_spec],
        out_shape=[jax.ShapeDtypeStruct((m, w), BF16), jax.ShapeDtypeStruct((w, m), BF16),
                   jax.ShapeDtypeStruct((m, w), BF16), jax.ShapeDtypeStruct((m, w), BF16),
                   jax.ShapeDtypeStruct((m, GATE_LANES), F32), jax.ShapeDtypeStruct((GATE_LANES, m), F32),
                   jax.ShapeDtypeStruct((m, MEM_WIDTH), BF16), u_shape],
        scratch_shapes=[pltpu.VMEM((V7X_SUBLANES, 2 * w), F32)],
        compiler_params=_cparams(2, VMEM_LIMIT),
        name="in_proj_b",
    )(x, w_bf, conv_w, bias, *prevs)


def _split3_dot(x, sel, terms=3):
    x1 = x.astype(BF16)
    r1 = x - x1.astype(F32)
    x2 = r1.astype(BF16)
    if terms == 2:
        return _dot(x1, sel) + _dot(x2, sel)
    x3 = (r1 - x2.astype(F32)).astype(BF16)
    return _dot(x1, sel) + (_dot(x2, sel) + _dot(x3, sel))


def _scan_rows(x, op, fill):
    n = x.shape[0]
    row = lax.broadcasted_iota(jnp.int32, (n, 1), 0)
    sh = 1
    while sh < n:
        x = op(x, jnp.where(row >= sh, pltpu.roll(x, sh, 0), fill))
        sh *= 2
    return x


def _mlstm_body(q_ref, kt_ref, v_ref, og_ref, gt_ref, gtt_ref, ng_ref, ln_ref, st0_ref, m0_ref,
                cell_ref, st_out, m_out, st, m_rows, qk_s):
    c = pl.program_id(1)
    L = q_ref.shape[0]
    E = HEAD_DIM
    H = MLSTM_HEADS
    W = V7X_LANES

    @pl.when(c == 0)
    def _():
        st[...] = st0_ref[...]
        m_rows[...] = m0_ref[...]

    gt = gt_ref[...]
    b_cols = _scan_rows(gt, jnp.add, 0.0)
    r_cols = pltpu.roll(gt, GATE_F0, 1) - b_cols
    cm = _scan_rows(r_cols, jnp.maximum, NEG_INF)
    m_all = m_rows[...]
    sub = lax.broadcasted_iota(jnp.int32, m_all.shape, 0)
    m_lane = jnp.max(jnp.where(lax.broadcasted_iota(jnp.int32, m_all.shape, 1) == sub + GATE_F0, m_all, NEG_INF),
                     axis=0, keepdims=True)
    a_cols = jnp.maximum(m_lane, cm)
    wi_cols = jnp.exp(m_lane - a_cols)
    em_cols = jnp.exp(-(b_cols + a_cols))

    gtt = gtt_ref[0:2 * GATE_F0, :]
    lane_t = lax.broadcasted_iota(jnp.int32, (1, L), 1)
    b_rows = gtt
    sh = 1
    while sh < L:
        b_rows = b_rows + jnp.where(lane_t >= sh, pltpu.roll(b_rows, sh, 1), 0.0)
        sh *= 2
    r_rows = gtt[0:GATE_F0, :] - b_rows[GATE_F0:2 * GATE_F0, :]
    b_last = b_rows[GATE_F0:2 * GATE_F0, L - 1:L]
    m_prev = m_all[:, 0:1]
    m_new = jnp.maximum(b_last + m_prev, b_last + jnp.max(r_rows, axis=1, keepdims=True))
    decay = jnp.exp(b_last + m_prev - m_new)
    wk_scale = jnp.exp(b_last + r_rows - m_new)
    m_next = jnp.broadcast_to(m_new, m_all.shape)
    m_rows[...] = m_next
    m_out[...] = m_next

    causal = lax.broadcasted_iota(jnp.int32, (L, L), 0) >= lax.broadcasted_iota(jnp.int32, (L, L), 1)
    sub_k = lax.broadcasted_iota(jnp.int32, (W, 1), 0)
    lane_w = lax.broadcasted_iota(jnp.int32, (1, W), 1)
    ones_rhs = jnp.ones((L, W), BF16)

    for h in range(H):
        p, odd = h // 2, h % 2
        own_rows = (sub_k >= E) if odd else (sub_k < E)
        kt_pad = jnp.where(own_rows, kt_ref[p * W:(p + 1) * W, :], jnp.zeros((), BF16))
        s = _dot(q_ref[:, p * W:(p + 1) * W], kt_pad)
        a_bh = jnp.broadcast_to(a_cols[:, GATE_F0 + h:GATE_F0 + h + 1], (L, L))
        w_intra = jnp.exp(jnp.where(causal, r_rows[h:h + 1, :] - a_bh, NEG_INF))
        qk_s[h] = s * w_intra

    ln_sel = ln_ref[...]
    for p in range(H // 2):
        slab = slice(p * W, (p + 1) * W)
        q_pair = q_ref[:, slab]
        v_aug = jnp.concatenate([v_ref[:, slab], ones_rhs], axis=1)
        halves = []
        for odd in range(2):
            h = 2 * p + odd
            qk = qk_s[h].astype(BF16)
            wib = jnp.broadcast_to(wi_cols[:, GATE_F0 + h:GATE_F0 + h + 1], (L, W))
            emb = jnp.broadcast_to(em_cols[:, GATE_F0 + h:GATE_F0 + h + 1], (L, W))
            nd = _dot(qk, v_aug) + jnp.concatenate([wib, wib], axis=1) * _dot(q_pair, st[h].astype(BF16))
            halves.append(nd[:, 0:W] / jnp.maximum(jnp.abs(nd[:, W:2 * W]), emb))
        hv = jnp.where(lane_w < E, halves[0], halves[1])
        mu = _split3_dot(hv, ln_sel, 2) * (1.0 / E)
        hc = hv - mu
        var = _split3_dot(hc * hc, ln_sel, 2) * (1.0 / E)
        cell_ref[:, slab] = (og_ref[:, slab].astype(F32) * (hc * lax.rsqrt(var + LN_EPS) * ng_ref[:, slab])
                             ).astype(cell_ref.dtype)

    for h in range(H):
        p, odd = h // 2, h % 2
        rows = slice(odd * E, (odd + 1) * E)
        own_lanes = (lane_w >= E) if odd else (lane_w < E)
        wkt = (kt_ref[h * E:(h + 1) * E, :].astype(F32) * wk_scale[h:h + 1, :]).astype(BF16)
        d = decay[h:h + 1, :]
        v_aug = jnp.concatenate([v_ref[:, p * W:(p + 1) * W], ones_rhs], axis=1)
        own2 = jnp.concatenate([own_lanes, own_lanes], axis=1)
        st[h, rows, :] = d * st[h, rows, :] + jnp.where(own2, _dot(wkt, v_aug), 0.0)
    st_out[...] = st[...]


def _ln_selector():
    g = jnp.arange(V7X_LANES) // HEAD_DIM
    return (g[:, None] == g[None, :]).astype(BF16)


def _mlstm(q, kt, v, og, gt, gtt, norm_g, c0, n0, m0, n_seq, seq_len, chunk):
    w, hh, e, lanes = MLSTM_WIDTH, MLSTM_HEADS, HEAD_DIM, V7X_LANES
    nc = seq_len // chunk
    mrows = 2 * V7X_SUBLANES
    blk = lambda a: jnp.stack([jnp.pad(a[:, h], ((0, 0), ((h % 2) * e, (1 - h % 2) * e), ((h % 2) * e, (1 - h % 2) * e)))
                               for h in range(hh)], axis=1)
    unblk = lambda s: jnp.stack([s[:, h, (h % 2) * e:(h % 2 + 1) * e, (h % 2) * e:(h % 2 + 1) * e]
                                 for h in range(hh)], axis=1)
    st0 = jnp.concatenate([blk(c0), blk(jnp.broadcast_to(n0[..., None], (n_seq, hh, e, e)))], axis=-1)
    m0b = jnp.pad(jnp.broadcast_to(m0[..., None], (n_seq, hh, lanes)), ((0, 0), (0, mrows - hh), (0, 0)))
    ln = _ln_selector()
    row = lambda width: pl.BlockSpec((chunk, width), lambda b, c: (b * nc + c, 0))
    col = lambda height: pl.BlockSpec((height, chunk), lambda b, c: (0, b * nc + c))
    const = lambda a: pl.BlockSpec(a.shape, lambda b, c: (0,) * a.ndim)
    st = lambda shp: pl.BlockSpec((None,) + shp, lambda b, c: (b,) + (0,) * len(shp))
    slab = (hh, lanes, 2 * lanes)
    cell, st_out, m_out = pl.pallas_call(
        _mlstm_body,
        grid=(n_seq, nc),
        in_specs=[row(w), col(w), row(w), row(w), row(GATE_LANES), col(GATE_LANES),
                  pl.BlockSpec((1, w), lambda b, c: (0, 0)), const(ln),
                  st(slab), st((mrows, lanes))],
        out_specs=[row(w), st(slab), st((mrows, lanes))],
        out_shape=[jax.ShapeDtypeStruct((n_seq * seq_len, w), BF16),
                   jax.ShapeDtypeStruct((n_seq,) + slab, F32),
                   jax.ShapeDtypeStruct((n_seq, mrows, lanes), F32)],
        scratch_shapes=[pltpu.VMEM(slab, F32), pltpu.VMEM((mrows, lanes), F32),
                        pltpu.VMEM((hh, chunk, chunk), F32)],
        compiler_params=_cparams(2, VMEM_LIMIT),
        name="mlstm_chunks",
    )(q, kt, v, og, gt, gtt, norm_g.reshape(1, w), ln, st0, m0b)
    return cell, unblk(st_out[..., 0:lanes]), unblk(st_out[..., lanes:])[..., 0], m_out[:, 0:hh, 0]


def _w_in_b_regroup(w_in, b_if):
    w4 = 4 * MLSTM_WIDTH
    hh = MLSTM_HEADS
    z = lambda n: jnp.zeros((D_MODEL, n), w_in.dtype)
    w = jnp.concatenate([w_in[:, :w4 + hh], z(GATE_F0 - hh), w_in[:, w4 + hh:w4 + 2 * hh],
                         z(GATE_LANES - GATE_F0 - hh), w_in[:, w4 + 2 * hh:]], axis=1)
    zb = lambda n: jnp.zeros((n,), F32)
    bias = jnp.concatenate([b_if[0].astype(F32), zb(GATE_F0 - hh), b_if[1].astype(F32),
                            zb(GATE_LANES - GATE_F0 - hh)]).reshape(1, GATE_LANES)
    return w, bias


def _layer_b_prompt(h_all, w_bf, bias, conv_w, norm_g, w_out, mem_prompt, w_mem, ln_g, ln_b, n_seq, seq_len,
                    tail):
    m = n_seq * seq_len
    tile = min(ROW_TILE, seq_len)
    chunk = min(MLSTM_CHUNK, seq_len)
    zstate = jnp.zeros((n_seq * V7X_SUBLANES, 2 * MLSTM_WIDTH), F32)
    q, kt, v, og, gt, gtt, mq, conv = _in_proj_b(h_all, 0, m, w_bf, conv_w, bias, [zstate] * 3, n_seq, seq_len,
                                                 tile, False)
    hh, e = MLSTM_HEADS, HEAD_DIM
    cell, c_out, n_out, m_out = _mlstm(q, kt, v, og, gt, gtt, norm_g, jnp.zeros((n_seq, hh, e, e), F32),
                                       jnp.zeros((n_seq, hh, e), F32), jnp.zeros((n_seq, hh), F32),
                                       n_seq, seq_len, chunk)
    mem_kv = _matmul(mem_prompt.reshape(n_seq * N_MEM, D_MODEL), w_mem, N_MEM)
    h1 = _fused_out([cell], [], mq, mem_kv.reshape(n_seq, N_MEM, 2 * MEM_WIDTH), h_all, w_out.astype(BF16),
                    ln_g, ln_b, tile, seq_len, tail)
    return h1, conv, c_out, n_out.reshape(n_seq, hh, e), m_out.reshape(n_seq, hh), mem_kv


def _layer_b_sample(h_all, h_off, w_bf, bias, conv_w, norm_g, w_out, mem_kv, conv_state, c0, n0, m0, ln_g, ln_b,
                    n_b, n_new):
    m = n_b * n_new
    w2 = 2 * MLSTM_WIDTH
    hh, e = MLSTM_HEADS, HEAD_DIM
    chunk = MLSTM_CHUNK
    t = jnp.arange(n_new)
    prevs = []
    for kk in (1, 2, 3):
        idx = jnp.clip(CONV_W - 1 - kk + t, 0, CONV_W - 2)
        prevs.append(conv_state[:, idx, :].reshape(m, w2))
    q, kt, v, og, gt, gtt, mq, u = _in_proj_b(h_all, h_off, m, w_bf, conv_w, bias, prevs, n_b, n_new, m, True)
    conv_new = jnp.concatenate([conv_state, u.reshape(n_b, n_new, w2)], axis=1)[:, n_new:]
    npad = chunk - n_new
    pad3 = lambda a: jnp.pad(a.reshape(n_b, n_new, a.shape[1]), ((0, 0), (0, npad), (0, 0)))
    gate_pad = jnp.where(jnp.arange(GATE_LANES) < hh, NEG_INF, 0.0).astype(F32)
    gtp = jnp.concatenate([gt.reshape(n_b, n_new, GATE_LANES),
                           jnp.broadcast_to(gate_pad, (n_b, npad, GATE_LANES))], axis=1)
    gttp = jnp.concatenate([gtt.reshape(GATE_LANES, n_b, n_new),
                            jnp.broadcast_to(gate_pad[:, None, None], (GATE_LANES, n_b, npad))], axis=2)
    ktp = jnp.pad(kt.reshape(MLSTM_WIDTH, n_b, n_new), ((0, 0), (0, 0), (0, npad)))
    flat = lambda a: a.reshape(n_b * chunk, a.shape[2])
    flat_t = lambda a: a.reshape(a.shape[0], n_b * chunk)
    cell, c_out, n_out, m_out = _mlstm(flat(pad3(q)), flat_t(ktp), flat(pad3(v)), flat(pad3(og)), flat(gtp),
                                       flat_t(gttp), norm_g, c0, n0, m0, n_b, chunk, chunk)
    cell = cell.reshape(n_b, chunk, MLSTM_WIDTH)[:, :n_new].reshape(m, MLSTM_WIDTH)
    mo = _sample_mem_attn(mq.astype(F32).reshape(n_b, n_new, MEM_WIDTH),
                          mem_kv.reshape(n_b, N_MEM, 2 * MEM_WIDTH), n_b, n_new)
    h1 = _out_proj(cell, mo.reshape(m, MEM_WIDTH), h_all, h_off, w_out, ln_g, ln_b, m)
    return h1, conv_new, c_out, n_out.reshape(n_b, hh, e), m_out.reshape(n_b, hh)


def _layer_a_sample(h2d, w_in, w_out, mem_kv, caches, ln_g, ln_b, n_b, n_new):
    m = n_b * n_new
    pos = PAST_LEN + (jnp.arange(m, dtype=jnp.int32) % n_new)
    outs = _in_proj_a(h2d, w_in, _rope_tables(pos), n_new, n_b, m, full_kv=True)
    qs, ks, vs, mq, kvs = outs[0:3], outs[3:6], outs[6:9], outs[9], outs[10:13]
    f3 = lambda t: t.reshape(n_b, n_new, t.shape[1])
    mix = _sample_attn([f3(t) for t in qs], [f3(t) for t in ks], [f3(t) for t in vs], caches, n_b, n_new)
    mo = _sample_mem_attn(f3(mq), mem_kv.reshape(n_b, N_MEM, 2 * MEM_WIDTH), n_b, n_new)
    h1 = _out_proj(mix.reshape(m, DIL_WIDTH), mo.reshape(m, MEM_WIDTH), h2d, 0, w_out, ln_g, ln_b, m)
    return h1, kvs


def _layer_a_prompt(h2d, mem_prompt, w_in, w_out, w_mem, ln_g, ln_b, n_seq, seq_len, tail):
    tile = min(ROW_TILE, seq_len)
    tabs = _rope_tables(jnp.arange(seq_len, dtype=jnp.int32))
    outs = _in_proj_a(h2d, w_in.astype(BF16), tabs, seq_len, n_seq, tile, full_kv=False)
    qs, ks, vs, mq, kvs = outs[0:3], outs[3:6], outs[6:9], outs[9], outs[10:13]
    mem_kv = _matmul(mem_prompt.reshape(n_seq * N_MEM, D_MODEL), w_mem, N_MEM)
    os_, lses = [], []
    for g, (win, dil) in enumerate(DIL_PAIRS):
        o, lse = _dil_attn(qs[g], ks[g], vs[g], n_seq, seq_len, dil, win // dil)
        os_.append(o)
        lses.append(lse)
    h1 = _fused_out(os_, lses, mq, mem_kv.reshape(n_seq, N_MEM, 2 * MEM_WIDTH), h2d, w_out.astype(BF16),
                    ln_g, ln_b, tile, seq_len, tail)
    return h1, kvs, mem_kv


def kernel(x_prompt, x_sample, mem_prompt, cache_win1_kv, cache_win2_kv, cache_win3_kv, cache_mem_kv,
           state_mlstm_conv, state_mlstm_C, state_mlstm_n, state_mlstm_m,
           w_in_A, w_out_A, w_in_B, mlstm_conv_w, mlstm_b_if, mlstm_norm_g, w_out_B,
           w_mem_kv, ln_g, ln_b, w_grp, b_grp, w_exp, b_exp, w_gate, w_up, w_down):
    n_p, seq, d = x_prompt.shape
    n_s, t_new, _ = x_sample.shape
    assert d == D_MODEL and w_in_A.shape[0] == 1 and w_in_B.shape[0] == 1
    mp, ms = n_p * seq, n_s * t_new
    m_all = mp + ms
    assert mp % ROW_TILE == 0 and ms % SMALL_TILE == 0 and mp % ms == 0
    xp = x_prompt.reshape(mp, d)
    xs = x_sample.reshape(ms, d)

    wg_bf, wu_bf, wd_bf = w_gate.astype(BF16), w_up.astype(BF16), w_down.astype(BF16)

    def moe(h, i, parts):
        return _moe_layer(h, mp, w_grp[i], b_grp[i], w_exp[i], b_exp[i], wg_bf, wu_bf, wd_bf, i,
                          ln_g[i, 1], ln_b[i, 1], parts)

    caches = (cache_win1_kv[0], cache_win2_kv[0], cache_win3_kv[0])
    hs, kv_s = _layer_a_sample(xs, w_in_A[0], w_out_A[0], cache_mem_kv[0], caches, ln_g[0, 0], ln_b[0, 0],
                               n_s, t_new)
    h, kv_p, mem_kv0 = _layer_a_prompt(xp, mem_prompt, w_in_A[0], w_out_A[0], w_mem_kv[0], ln_g[0, 0], ln_b[0, 0],
                                       n_p, seq, hs)
    (h,) = moe(h, 0, [(0, m_all)])

    w_b, bias = _w_in_b_regroup(w_in_B[0], mlstm_b_if[0])
    w_bf = w_b.astype(BF16)
    hs, conv_s, c_s, nv_s, mm_s = _layer_b_sample(
        h, mp, w_b, bias, mlstm_conv_w[0], mlstm_norm_g[0], w_out_B[0], cache_mem_kv[1],
        state_mlstm_conv[0], state_mlstm_C[0], state_mlstm_n[0], state_mlstm_m[0],
        ln_g[1, 0], ln_b[1, 0], n_s, t_new)
    h1, conv_p, c_p, nv_p, mm_p, mem_kv1 = _layer_b_prompt(
        h, w_bf, bias, mlstm_conv_w[0], mlstm_norm_g[0], w_out_B[0], mem_prompt, w_mem_kv[1],
        ln_g[1, 0], ln_b[1, 0], n_p, seq, hs)
    yp, ys = moe(h1, 1, [(0, mp), (mp, ms)])

    kv_shape = (2, HEADS_PER_GROUP, HEAD_DIM)
    wins = []
    for g in range(N_DIL_GROUPS):
        wins.append(kv_p[g].reshape((1, n_p, -1) + kv_shape))
        wins.append(kv_s[g].reshape((1, n_s, t_new) + kv_shape))
    mem_kv_p = jnp.stack([mem_kv0, mem_kv1]).reshape((2, n_p, N_MEM, 2, MEM_HEADS, HEAD_DIM))
    return (yp.reshape(n_p, seq, d), ys.reshape(n_s, t_new, d), *wins,
            conv_p[None], conv_s[None], c_p[None], c_s[None], nv_p[None], nv_s[None], mm_p[None], mm_s[None],
            mem_kv_p)
```
